```python
import jax, jax.numpy as jnp
from jax import lax
import numpy as np

D_MODEL = 2048
BATCH = 2
SEQ = 4096
DEPTH = 1

GRID_W = 64
CTX_LEN = 256
HEAD_DIM = 128
N_HEADS = D_MODEL // HEAD_DIM
N_KV_HEADS = N_HEADS // 4
Q_GROUP = N_HEADS // N_KV_HEADS
ATT_WIDTH = N_HEADS * HEAD_DIM
KV_WIDTH = N_KV_HEADS * HEAD_DIM
ROPE_FREQS = HEAD_DIM // 4
ROPE_THETA = 10000.0
Q_BLOCK = 128
F_GROUP_DIM = 128
F_WIDTH = D_MODEL // 2
F_GROUPS = F_WIDTH // F_GROUP_DIM
N_EXPERTS = 64
N_EXPERT_GROUPS = 8
TOPK_GROUPS = 4
TOP_K = 8
EXPERT_FF = D_MODEL // 4
SHARED_FF = D_MODEL // 4
ROUTED_SCALE = 2.5
EXPERT_BLOCK = 128
EPS = 1e-6
Q_OFF = F_WIDTH
K_OFF = Q_OFF + ATT_WIDTH
V_OFF = K_OFF + KV_WIDTH
GF_OFF = V_OFF + KV_WIDTH
GA_OFF = GF_OFF + D_MODEL
IN_COLS = GA_OFF + D_MODEL

kernel_name = 'hybrid_fourier_gqa_moe_dit_block'


def rmsnorm(x, w):
    x32 = x.astype(jnp.float32)
    y = x32 * lax.rsqrt(jnp.mean(x32 * x32, axis=-1, keepdims=True) + EPS)
    return (y * w.astype(jnp.float32)).astype(x.dtype)


def modulate(h, shift, scale):
    return h * (1 + scale) + shift


def axial_rope(n_tokens):
    rows = n_tokens // GRID_W
    row = jnp.repeat(jnp.arange(rows), GRID_W)
    col = jnp.tile(jnp.arange(GRID_W), rows)
    pos = jnp.stack([row, col], axis=-1).astype(jnp.float32)
    inv_freq = ROPE_THETA ** (-jnp.arange(ROPE_FREQS, dtype=jnp.float32) / ROPE_FREQS)
    ang = pos[:, :, None] * inv_freq
    return jnp.cos(ang), jnp.sin(ang)


def apply_rope(x, cos, sin):
    b, s, h, _ = x.shape
    xr = x.reshape(b, s, h, 2, 2, ROPE_FREQS)
    xa, xb = xr[..., 0, :], xr[..., 1, :]
    c = cos[None, :, None].astype(x.dtype)
    sn = sin[None, :, None].astype(x.dtype)
    out = jnp.stack([xa * c - xb * sn, xb * c + xa * sn], axis=-2)
    return out.reshape(b, s, h, HEAD_DIM)


def to_heads(t, n_heads):
    return t.reshape(t.shape[0], t.shape[1], n_heads, HEAD_DIM)


def attend(qb, k, v):
    s = jnp.einsum('bqkgd,bskd->bkgqs', qb, k).astype(jnp.float32) * (HEAD_DIM ** -0.5)
    p = jax.nn.softmax(s, axis=-1).astype(v.dtype)
    return jnp.einsum('bkgqs,bskd->bqkgd', p, v)


def latent_attention(q, k, v, kc, vc):
    b, s = q.shape[:2]
    k_all = jnp.concatenate([k, kc], axis=1)
    v_all = jnp.concatenate([v, vc], axis=1)
    qb = q.reshape(b, s // Q_BLOCK, Q_BLOCK, N_KV_HEADS, Q_GROUP, HEAD_DIM).transpose(1, 0, 2, 3, 4, 5)
    o = lax.map(lambda qi: attend(qi, k_all, v_all), qb)
    return o.transpose(1, 0, 2, 3, 4, 5).reshape(b, s, ATT_WIDTH)


def context_attention(qc, kc, vc):
    b, n = qc.shape[:2]
    return attend(qc.reshape(b, n, N_KV_HEADS, Q_GROUP, HEAD_DIM), kc, vc).reshape(b, n, ATT_WIDTH)


def fourier_mix(u):
    b, n, _ = u.shape
    ug = u.astype(jnp.float32).reshape(b, n, F_GROUPS, F_GROUP_DIM)
    y = jnp.fft.fft2(ug, axes=(1, 3), norm='ortho').real
    return y.reshape(b, n, F_WIDTH).astype(u.dtype)


def split_projection(p):
    return (p[..., :Q_OFF], p[..., Q_OFF:K_OFF], p[..., K_OFF:V_OFF],
            p[..., V_OFF:GF_OFF], p[..., GF_OFF:GA_OFF], p[..., GA_OFF:])


def merge_branches(yf, ya, gf, ga, w_fo, w_ao, w_o):
    y = jax.nn.sigmoid(gf) * (yf @ w_fo) + jax.nn.sigmoid(ga) * (ya @ w_ao)
    return y @ w_o


def swiglu(x, wg, wu, wd):
    return (jax.nn.silu(x @ wg) * (x @ wu)) @ wd


def route(h, router_w, router_b):
    scores = jax.nn.sigmoid(h.astype(jnp.float32) @ router_w.astype(jnp.float32))
    biased = scores + router_b.astype(jnp.float32)
    grp = biased.reshape(-1, N_EXPERT_GROUPS, N_EXPERTS // N_EXPERT_GROUPS)
    gscore = lax.top_k(grp, 2)[0].sum(-1)
    _, gidx = lax.top_k(gscore, TOPK_GROUPS)
    gmask = jax.nn.one_hot(gidx, N_EXPERT_GROUPS, dtype=jnp.float32).sum(1) > 0
    emask = jnp.repeat(gmask, N_EXPERTS // N_EXPERT_GROUPS, axis=1)
    _, eidx = lax.top_k(jnp.where(emask, biased, -jnp.inf), TOP_K)
    w = jnp.take_along_axis(scores, eidx, axis=1)
    w = w / jnp.sum(w, axis=-1, keepdims=True) * ROUTED_SCALE
    return eidx, w


def moe(h, router_w, router_b, exp_gate, exp_up, exp_down, sh_gate, sh_up, sh_down):
    t = h.shape[0]
    eidx, w = route(h, router_w, router_b)
    n_assign = t * TOP_K
    flat_e = eidx.reshape(-1)
    flat_tok = jnp.repeat(jnp.arange(t, dtype=jnp.int32), TOP_K)
    flat_w = w.reshape(-1).astype(h.dtype)
    order = jnp.argsort(flat_e)
    se, stok, sw = flat_e[order], flat_tok[order], flat_w[order]
    counts = jnp.bincount(flat_e, length=N_EXPERTS)
    starts = jnp.cumsum(counts) - counts
    padded = (counts + EXPERT_BLOCK - 1) // EXPERT_BLOCK * EXPERT_BLOCK
    pends = jnp.cumsum(padded)
    pstarts = pends - padded
    dest = pstarts[se] + (jnp.arange(n_assign) - starts[se])
    n_blocks = -(-n_assign // EXPERT_BLOCK) + N_EXPERTS
    n_slots = n_blocks * EXPERT_BLOCK
    slot_tok = jnp.full((n_slots,), t, jnp.int32).at[dest].set(stok)
    slot_w = jnp.zeros((n_slots,), h.dtype).at[dest].set(sw)
    block_e = jnp.minimum(jnp.searchsorted(pends, jnp.arange(n_blocks) * EXPERT_BLOCK, side='right'), N_EXPERTS - 1)
    h_pad = jnp.concatenate([h, jnp.zeros((1, h.shape[1]), h.dtype)], axis=0)

    def expert_block(args):
        tok, wt, e = args
        return swiglu(h_pad[tok], exp_gate[e], exp_up[e], exp_down[e]) * wt[:, None]

    y = lax.map(expert_block, (slot_tok.reshape(n_blocks, EXPERT_BLOCK),
                               slot_w.reshape(n_blocks, EXPERT_BLOCK), block_e))
    routed = jax.ops.segment_sum(y.reshape(n_slots, -1), slot_tok, num_segments=t + 1)[:t]
    return routed + swiglu(h, sh_gate, sh_up, sh_down)


def setup_inputs(seed: int = 0) -> dict:
    key = jax.random.key(seed)
    ks = jax.random.split(key, 22)

    def nrm(k, shape, scale=1.0):
        return jax.random.normal(k, shape, jnp.float32) * scale

    D = D_MODEL
    L = DEPTH
    return {
        'x': nrm(ks[0], (BATCH, SEQ, D)),
        'c': nrm(ks[1], (BATCH, D)),
        'ctx': nrm(ks[2], (BATCH, CTX_LEN, D)),
        'c_ctx': nrm(ks[3], (D,)),
        'mod_w': nrm(ks[4], (L, D, 6 * D), 0.5 * D ** -0.5),
        'mod_b': nrm(ks[5], (L, 6 * D), 0.02),
        'norm1_w': 1.0 + nrm(ks[6], (L, D), 0.02),
        'w_in': nrm(ks[7], (L, D, IN_COLS), D ** -0.5),
        'q_norm_w': 1.0 + nrm(ks[8], (L, HEAD_DIM), 0.02),
        'k_norm_w': 1.0 + nrm(ks[9], (L, HEAD_DIM), 0.02),
        'w_fourier_out': nrm(ks[10], (L, F_WIDTH, D), F_WIDTH ** -0.5),
        'w_attn_out': nrm(ks[11], (L, ATT_WIDTH, D), ATT_WIDTH ** -0.5),
        'w_out': nrm(ks[12], (L, D, D), D ** -0.5),
        'norm2_w': 1.0 + nrm(ks[13], (L, D), 0.02),
        'router_w': nrm(ks[14], (L, D, N_EXPERTS), D ** -0.5),
        'router_b': nrm(ks[15], (L, N_EXPERTS), 0.01),
        'exp_gate': nrm(ks[16], (L, N_EXPERTS, D, EXPERT_FF), D ** -0.5),
        'exp_up': nrm(ks[17], (L, N_EXPERTS, D, EXPERT_FF), D ** -0.5),
        'exp_down': nrm(ks[18], (L, N_EXPERTS, EXPERT_FF, D), EXPERT_FF ** -0.5),
        'shared_gate': nrm(ks[19], (L, D, SHARED_FF), D ** -0.5),
        'shared_up': nrm(ks[20], (L, D, SHARED_FF), D ** -0.5),
        'shared_down': nrm(ks[21], (L, SHARED_FF, D), SHARED_FF ** -0.5),
    }


def reference(x, c, ctx, c_ctx, mod_w, mod_b, norm1_w, w_in, q_norm_w, k_norm_w,
              w_fourier_out, w_attn_out, w_out, norm2_w, router_w, router_b,
              exp_gate, exp_up, exp_down, shared_gate, shared_up, shared_down):
    b, s, d = x.shape
    n_ctx = ctx.shape[1]
    cos, sin = axial_rope(s)
    cond = jnp.concatenate([c, c_ctx[None]], axis=0)
    cx = ctx
    for l in range(DEPTH):
        last = l == DEPTH - 1
        mod = jax.nn.silu(cond) @ mod_w[l] + mod_b[l]
        sh1, sc1, g1, sh2, sc2, g2 = jnp.split(mod[:b][:, None], 6, axis=-1)
        csh1, csc1, cg1, csh2, csc2, cg2 = jnp.split(mod[b], 6, axis=-1)

        h = modulate(rmsnorm(x, norm1_w[l]), sh1, sc1)
        hc = modulate(rmsnorm(cx, norm1_w[l]), csh1, csc1)
        u, q, k, v, gf, ga = split_projection(h @ w_in[l])
        if last:
            kvc = hc @ w_in[l][:, K_OFF:GF_OFF]
        else:
            pc = hc @ w_in[l]
            kvc = pc[..., K_OFF:GF_OFF]
        kc = rmsnorm(to_heads(kvc[..., :KV_WIDTH], N_KV_HEADS), k_norm_w[l])
        vc = to_heads(kvc[..., KV_WIDTH:], N_KV_HEADS)
        q = apply_rope(rmsnorm(to_heads(q, N_HEADS), q_norm_w[l]), cos, sin)
        k = apply_rope(rmsnorm(to_heads(k, N_KV_HEADS), k_norm_w[l]), cos, sin)
        v = to_heads(v, N_KV_HEADS)
        ya = latent_attention(q, k, v, kc, vc)
        yf = fourier_mix(u)
        x = x + g1 * merge_branches(yf, ya, gf, ga, w_fourier_out[l], w_attn_out[l], w_out[l])

        if not last:
            uc, qc, _, _, gfc, gac = split_projection(pc)
            qc = rmsnorm(to_heads(qc, N_HEADS), q_norm_w[l])
            yac = context_attention(qc, kc, vc)
            yfc = fourier_mix(uc)
            cx = cx + cg1 * merge_branches(yfc, yac, gfc, gac, w_fourier_out[l], w_attn_out[l], w_out[l])

        h2 = modulate(rmsnorm(x, norm2_w[l]), sh2, sc2)
        x = x + g2 * moe(h2.reshape(b * s, d), router_w[l], router_b[l], exp_gate[l], exp_up[l],
                         exp_down[l], shared_gate[l], shared_up[l], shared_down[l]).reshape(b, s, d)
        if not last:
            h2c = modulate(rmsnorm(cx, norm2_w[l]), csh2, csc2)
            cx = cx + cg2 * moe(h2c.reshape(b * n_ctx, d), router_w[l], router_b[l], exp_gate[l], exp_up[l],
                                exp_down[l], shared_gate[l], shared_up[l], shared_down[l]).reshape(b, n_ctx, d)
    return x
```

```python
import functools
import math

import jax
import jax.numpy as jnp
from jax import lax
from jax.experimental import pallas as pl
from jax.experimental.pallas import tpu as pltpu

F32 = jnp.float32
BF16 = jnp.bfloat16
I32 = jnp.int32

GRID_W = 64
HEAD_DIM = 128
Q_GROUP = 4
ROPE_FREQS = HEAD_DIM // 4
ROPE_THETA = 10000.0
F_GROUP_DIM = 128
N_EXPERT_GROUPS = 8
TOPK_GROUPS = 4
TOP_K = 8
ROUTED_SCALE = 2.5
EPS = 1e-6
LANE = 128
EXPERT_BLOCK = 256
VMEM_LIMIT = 56 * 1024 * 1024


def _cparams(*sem):
    return pltpu.CompilerParams(dimension_semantics=sem, vmem_limit_bytes=VMEM_LIMIT)


def _silu(v):
    return v * jax.nn.sigmoid(v)


def _adaln_kernel(c_ref, w_ref, b_ref, o_ref):
    a = _silu(c_ref[...])
    o_ref[...] = jnp.dot(a.astype(BF16), w_ref[...].astype(BF16), preferred_element_type=F32) + b_ref[...]


def _adaln(cond_pad, mod_w, mod_b):
    d, n = mod_w.shape
    tn = min(n, 1024)
    return pl.pallas_call(
        _adaln_kernel,
        grid=(n // tn,),
        in_specs=[pl.BlockSpec((8, d), lambda j: (0, 0)),
                  pl.BlockSpec((d, tn), lambda j: (0, j)),
                  pl.BlockSpec((1, tn), lambda j: (0, j))],
        out_specs=pl.BlockSpec((8, tn), lambda j: (0, j)),
        out_shape=jax.ShapeDtypeStruct((8, n), F32),
        compiler_params=_cparams("arbitrary"),
        name="adaln",
    )(cond_pad, mod_w, mod_b.reshape(1, n))


def _head_norm_rope(a, w, cos, sin, scale):
    ms = jnp.mean(a * a, axis=-1, keepdims=True)
    a = a * lax.rsqrt(ms + EPS) * w
    if cos is not None:
        lane = lax.broadcasted_iota(I32, a.shape, 1)
        first = (lane % (2 * ROPE_FREQS)) < ROPE_FREQS
        partner = jnp.where(first, pltpu.roll(a, HEAD_DIM - ROPE_FREQS, 1), pltpu.roll(a, ROPE_FREQS, 1))
        a = a * cos + partner * sin
    if scale is not None:
        a = a * scale
    return a


def _inproj_kernel(x_ref, sh_ref, sc_ref, nw_ref, w_ref, qw_ref, kw_ref, cos_ref, sin_ref, o_ref, h_scr,
                   *, modes, tn):
    j = pl.program_id(1)

    @pl.when(j == 0)
    def _():
        x = x_ref[...]
        ms = jnp.mean(x * x, axis=-1, keepdims=True)
        y = x * lax.rsqrt(ms + EPS) * nw_ref[...]
        h_scr[...] = (y * (1.0 + sc_ref[...]) + sh_ref[...]).astype(BF16)

    acc = jnp.dot(h_scr[...], w_ref[...], preferred_element_type=F32)
    for mode, j0, j1 in modes:
        @pl.when((j >= j0) & (j < j1))
        def _(mode=mode):
            if mode == "plain":
                o_ref[...] = acc.astype(BF16)
            elif mode == "sigmoid":
                o_ref[...] = jax.nn.sigmoid(acc).astype(BF16)
            else:
                rope = mode in ("q", "k")
                w = qw_ref[...] if mode == "q" else kw_ref[...]
                scale = HEAD_DIM ** -0.5 if mode == "q" else None
                for h in range(tn // HEAD_DIM):
                    sl = slice(h * HEAD_DIM, (h + 1) * HEAD_DIM)
                    a = _head_norm_rope(acc[:, sl], w, cos_ref[...] if rope else None,
                                        sin_ref[...] if rope else None, scale)
                    o_ref[:, sl] = a.astype(BF16)


def _inproj(x2d, shift, scale, norm_w, w_bf16, q_norm_w, k_norm_w, cos_t, sin_t, modes, seq, tm, tn):
    m, d = x2d.shape
    n = w_bf16.shape[1]
    per_b = seq // tm
    pos_blocks = cos_t.shape[0] // tm
    kern = functools.partial(_inproj_kernel, modes=modes, tn=tn)
    return pl.pallas_call(
        kern,
        grid=(m // tm, n // tn),
        in_specs=[pl.BlockSpec((tm, d), lambda i, j: (i, 0)),
                  pl.BlockSpec((None, 1, d), lambda i, j: (i // per_b, 0, 0)),
                  pl.BlockSpec((None, 1, d), lambda i, j: (i // per_b, 0, 0)),
                  pl.BlockSpec((1, d), lambda i, j: (0, 0)),
                  pl.BlockSpec((d, tn), lambda i, j: (0, j)),
                  pl.BlockSpec((1, HEAD_DIM), lambda i, j: (0, 0)),
                  pl.BlockSpec((1, HEAD_DIM), lambda i, j: (0, 0)),
                  pl.BlockSpec((tm, HEAD_DIM), lambda i, j: (i % pos_blocks, 0)),
                  pl.BlockSpec((tm, HEAD_DIM), lambda i, j: (i % pos_blocks, 0))],
        out_specs=pl.BlockSpec((tm, tn), lambda i, j: (i, j)),
        out_shape=jax.ShapeDtypeStruct((m, n), BF16),
        scratch_shapes=[pltpu.VMEM((tm, d), BF16)],
        compiler_params=_cparams("arbitrary", "arbitrary"),
        name="inproj",
    )(x2d, shift, scale, norm_w.reshape(1, d), w_bf16, q_norm_w.reshape(1, HEAD_DIM),
      k_norm_w.reshape(1, HEAD_DIM), cos_t, sin_t)


def _rope_tables(seq):
    rows = seq // GRID_W
    row = jnp.repeat(jnp.arange(rows), GRID_W)
    col = jnp.tile(jnp.arange(GRID_W), rows)
    pos = jnp.stack([row, col], axis=-1).astype(F32)
    inv_freq = ROPE_THETA ** (-jnp.arange(ROPE_FREQS, dtype=F32) / ROPE_FREQS)
    ang = pos[:, :, None] * inv_freq
    cos, sin = jnp.cos(ang), jnp.sin(ang)
    cos_t = jnp.concatenate([cos[:, 0], cos[:, 0], cos[:, 1], cos[:, 1]], axis=-1)
    sin_t = jnp.concatenate([-sin[:, 0], sin[:, 0], -sin[:, 1], sin[:, 1]], axis=-1)
    return cos_t, sin_t


def _attn_kernel(q_ref, k_ref, v_ref, kc_ref, vc_ref, o_ref, *, tq, tk, n_chunks):
    q = jnp.concatenate([q_ref[:, h * HEAD_DIM:(h + 1) * HEAD_DIM] for h in range(Q_GROUP)], axis=0)
    m_rows = Q_GROUP * tq

    def step(kb, vb, carry):
        m, l, acc = carry
        s = lax.dot_general(q, kb, (((1,), (1,)), ((), ())), preferred_element_type=F32)
        m_new = jnp.maximum(m, jnp.max(s, axis=-1, keepdims=True))
        alpha = jnp.exp(m - m_new)
        p = jnp.exp(s - m_new)
        l = alpha * l + jnp.sum(p, axis=-1, keepdims=True)
        acc = alpha * acc + jnp.dot(p.astype(BF16), vb, preferred_element_type=F32)
        return m_new, l, acc

    def body(c, carry):
        off = pl.multiple_of(c * tk, tk)
        return step(k_ref[pl.ds(off, tk), :], v_ref[pl.ds(off, tk), :], carry)

    init = (jnp.full((m_rows, 1), -jnp.inf, F32), jnp.zeros((m_rows, 1), F32),
            jnp.zeros((m_rows, HEAD_DIM), F32))
    carry = lax.fori_loop(0, n_chunks, body, init)
    _, l, acc = step(kc_ref[...], vc_ref[...], carry)
    o = acc / l
    for h in range(Q_GROUP):
        o_ref[:, h * HEAD_DIM:(h + 1) * HEAD_DIM] = o[h * tq:(h + 1) * tq].astype(BF16)


def _attention(proj, kvc, batch, seq, n_ctx, n_kv, q_off, k_off, v_off, tq, tk):
    gw = Q_GROUP * HEAD_DIM
    per_b = seq // tq
    kern = functools.partial(_attn_kernel, tq=tq, tk=tk, n_chunks=seq // tk)
    return pl.pallas_call(
        kern,
        grid=(batch, n_kv, per_b),
        in_specs=[pl.BlockSpec((tq, gw), lambda b, g, i: (b * per_b + i, q_off // gw + g)),
                  pl.BlockSpec((seq, HEAD_DIM), lambda b, g, i: (b, k_off // HEAD_DIM + g)),
                  pl.BlockSpec((seq, HEAD_DIM), lambda b, g, i: (b, v_off // HEAD_DIM + g)),
                  pl.BlockSpec((n_ctx, HEAD_DIM), lambda b, g, i: (b, g)),
                  pl.BlockSpec((n_ctx, HEAD_DIM), lambda b, g, i: (b, n_kv + g))],
        out_specs=pl.BlockSpec((tq, gw), lambda b, g, i: (b * per_b + i, g)),
        out_shape=jax.ShapeDtypeStruct((batch * seq, n_kv * gw), BF16),
        compiler_params=_cparams("arbitrary", "arbitrary", "arbitrary"),
        name="attention",
    )(proj, proj, proj, kvc, kvc)


def _dft_mats(n_rows, n_cols):
    def cs(k, n):
        ang = (2.0 * math.pi / n) * (k % n).astype(F32)
        return jnp.cos(ang), jnp.sin(ang)

    ch = jnp.arange(F_GROUP_DIM)
    cc, sc = cs(ch[:, None] * ch[None, :], F_GROUP_DIM)
    mat_a = jnp.concatenate([cc, -sc], axis=1).astype(BF16)
    r = jnp.arange(n_rows)
    cr, sr = cs(r[:, None] * r[None, :], n_rows)
    mat_1 = jnp.concatenate([jnp.concatenate([cr, sr], axis=1),
                             jnp.concatenate([-sr, cr], axis=1)], axis=0).astype(BF16)
    c = jnp.arange(n_cols)
    c2, s2 = cs(c[:, None] * c[None, :], n_cols)
    mat_2 = jnp.concatenate([c2, s2], axis=1).astype(BF16)
    tr, ts = cs(r[:, None] * c[None, :], n_rows * n_cols)
    tw_r = jnp.repeat(tr, LANE, axis=1)
    tw_i = jnp.repeat(-ts, LANE, axis=1)
    return mat_a, mat_1, mat_2, tw_r, tw_i


def _fourier_a_kernel(u_ref, m_ref, vr_ref, vi_ref, *, groups):
    for g in range(groups):
        sl = slice(g * F_GROUP_DIM, (g + 1) * F_GROUP_DIM)
        r = jnp.dot(u_ref[:, sl], m_ref[...], preferred_element_type=F32)
        vr_ref[:, sl] = r[:, :F_GROUP_DIM].astype(BF16)
        vi_ref[:, sl] = r[:, F_GROUP_DIM:].astype(BF16)


def _fourier_1_kernel(ar_ref, ai_ref, m_ref, twr_ref, twi_ref, pr_ref, pi_ref, *, nseg, fw, n_rows):
    rhs = jnp.concatenate([ar_ref[...], ai_ref[...]], axis=0)
    z = jnp.dot(m_ref[...], rhs, preferred_element_type=F32)
    reps = fw // LANE
    for s in range(nseg):
        zr = z[:n_rows, s * fw:(s + 1) * fw]
        zi = z[n_rows:, s * fw:(s + 1) * fw]
        tr = jnp.tile(twr_ref[:, s * LANE:(s + 1) * LANE], (1, reps))
        ti = jnp.tile(twi_ref[:, s * LANE:(s + 1) * LANE], (1, reps))
        pr_ref[s] = (zr * tr - zi * ti).astype(BF16)
        pi_ref[s] = (zr * ti + zi * tr).astype(BF16)


def _fourier_2_kernel(ar_ref, ai_ref, m_ref, o_ref, *, norm):
    rhs = jnp.concatenate([ar_ref[...], ai_ref[...]], axis=0)
    o_ref[...] = (jnp.dot(m_ref[...], rhs, preferred_element_type=F32) * norm).astype(BF16)


def _fourier_mix(proj, batch, seq, fw):
    n_cols = GRID_W
    n_rows = seq // GRID_W
    groups = fw // F_GROUP_DIM
    mat_a, mat_1, mat_2, tw_r, tw_i = _dft_mats(n_rows, n_cols)
    m = batch * seq
    tm = min(seq, 1024)
    vr, vi = pl.pallas_call(
        functools.partial(_fourier_a_kernel, groups=groups),
        grid=(m // tm,),
        in_specs=[pl.BlockSpec((tm, fw), lambda i: (i, 0)),
                  pl.BlockSpec((F_GROUP_DIM, 2 * F_GROUP_DIM), lambda i: (0, 0))],
        out_specs=[pl.BlockSpec((tm, fw), lambda i: (i, 0))] * 2,
        out_shape=[jax.ShapeDtypeStruct((m, fw), BF16)] * 2,
        compiler_params=_cparams("arbitrary"),
        name="fourier_channels",
    )(proj, mat_a)
    wide = n_cols * fw
    vr = vr.reshape(batch * n_rows, wide)
    vi = vi.reshape(batch * n_rows, wide)
    nseg = min(n_cols, 4)
    pr, pi = pl.pallas_call(
        functools.partial(_fourier_1_kernel, nseg=nseg, fw=fw, n_rows=n_rows),
        grid=(batch, n_cols // nseg),
        in_specs=[pl.BlockSpec((n_rows, nseg * fw), lambda b, j: (b, j)),
                  pl.BlockSpec((n_rows, nseg * fw), lambda b, j: (b, j)),
                  pl.BlockSpec((2 * n_rows, 2 * n_rows), lambda b, j: (0, 0)),
                  pl.BlockSpec((n_rows, nseg * LANE), lambda b, j: (0, j)),
                  pl.BlockSpec((n_rows, nseg * LANE), lambda b, j: (0, j))],
        out_specs=[pl.BlockSpec((nseg, n_rows, fw), lambda b, j: (b * (n_cols // nseg) + j, 0, 0))] * 2,
        out_shape=[jax.ShapeDtypeStruct((batch * n_cols, n_rows, fw), BF16)] * 2,
        compiler_params=_cparams("arbitrary", "arbitrary"),
        name="fourier_rows",
    )(vr, vi, mat_1, tw_r, tw_i)
    wide2 = n_rows * fw
    pr = pr.reshape(batch * n_cols, wide2)
    pi = pi.reshape(batch * n_cols, wide2)
    tc = min(wide2, 8192)
    norm = 1.0 / math.sqrt(seq * F_GROUP_DIM)
    yf = pl.pallas_call(
        functools.partial(_fourier_2_kernel, norm=norm),
        grid=(batch, wide2 // tc),
        in_specs=[pl.BlockSpec((n_cols, tc), lambda b, j: (b, j)),
                  pl.BlockSpec((n_cols, tc), lambda b, j: (b, j)),
                  pl.BlockSpec((n_cols, 2 * n_cols), lambda b, j: (0, 0))],
        out_specs=pl.BlockSpec((n_cols, tc), lambda b, j: (b, j)),
        out_shape=jax.ShapeDtypeStruct((batch * n_cols, wide2), BF16),
        compiler_params=_cparams("arbitrary", "arbitrary"),
        name="fourier_cols",
    )(pr, pi, mat_2)
    return yf.reshape(m, fw)


def _merge_kernel(yf_ref, ya_ref, wfo_ref, wao_ref, gf_ref, ga_ref, o_ref):
    a = jnp.dot(yf_ref[...], wfo_ref[...], preferred_element_type=F32)
    b = jnp.dot(ya_ref[...], wao_ref[...], preferred_element_type=F32)
    o_ref[...] = (gf_ref[...].astype(F32) * a + ga_ref[...].astype(F32) * b).astype(BF16)


def _merge(yf, ya, wfo, wao, proj, gf_off, ga_off, tm, tn):
    m, fw = yf.shape
    aw = ya.shape[1]
    d = wfo.shape[1]
    return pl.pallas_call(
        _merge_kernel,
        grid=(m // tm, d // tn),
        in_specs=[pl.BlockSpec((tm, fw), lambda i, j: (i, 0)),
                  pl.BlockSpec((tm, aw), lambda i, j: (i, 0)),
                  pl.BlockSpec((fw, tn), lambda i, j: (0, j)),
                  pl.BlockSpec((aw, tn), lambda i, j: (0, j)),
                  pl.BlockSpec((tm, tn), lambda i, j: (i, gf_off // tn + j)),
                  pl.BlockSpec((tm, tn), lambda i, j: (i, ga_off // tn + j))],
        out_specs=pl.BlockSpec((tm, tn), lambda i, j: (i, j)),
        out_shape=jax.ShapeDtypeStruct((m, d), BF16),
        compiler_params=_cparams("arbitrary", "arbitrary"),
        name="merge",
    )(yf, ya, wfo, wao, proj, proj)


def _route(logits_t, bias, n_exp):
    gsz = n_exp // N_EXPERT_GROUPS
    tm = logits_t.shape[1]
    neg = -jnp.inf
    scores = jax.nn.sigmoid(logits_t)
    biased = scores + bias
    io_g = lax.broadcasted_iota(I32, (gsz, tm), 0).astype(F32)
    gs = []
    for g in range(N_EXPERT_GROUPS):
        grp = biased[g * gsz:(g + 1) * gsz, :]
        m1 = jnp.max(grp, axis=0, keepdims=True)
        i1 = jnp.min(jnp.where(grp == m1, io_g, float(gsz)), axis=0, keepdims=True)
        m2 = jnp.max(jnp.where(io_g == i1, neg, grp), axis=0, keepdims=True)
        gs.append(m1 + m2)
    cur = jnp.concatenate(gs, axis=0)
    io_n = lax.broadcasted_iota(I32, (N_EXPERT_GROUPS, tm), 0).astype(F32)
    sel = jnp.zeros((N_EXPERT_GROUPS, tm), F32)
    for _ in range(TOPK_GROUPS):
        mx = jnp.max(cur, axis=0, keepdims=True)
        ix = jnp.min(jnp.where(cur == mx, io_n, float(N_EXPERT_GROUPS)), axis=0, keepdims=True)
        hit = io_n == ix
        sel = jnp.where(hit, 1.0, sel)
        cur = jnp.where(hit, neg, cur)
    cur = jnp.concatenate([jnp.where(sel[g:g + 1, :] > 0.0, biased[g * gsz:(g + 1) * gsz, :], neg)
                           for g in range(N_EXPERT_GROUPS)], axis=0)
    io_e = lax.broadcasted_iota(I32, (n_exp, tm), 0).astype(F32)
    eidx, wts = [], []
    for _ in range(TOP_K):
        mx = jnp.max(cur, axis=0, keepdims=True)
        ix = jnp.min(jnp.where(cur == mx, io_e, float(n_exp)), axis=0, keepdims=True)
        hit = io_e == ix
        wts.append(jnp.sum(jnp.where(hit, scores, 0.0), axis=0, keepdims=True))
        eidx.append(ix)
        cur = jnp.where(hit, neg, cur)
    eidx = jnp.concatenate(eidx, axis=0).astype(I32)
    wts = jnp.concatenate(wts, axis=0)
    wts = wts / jnp.sum(wts, axis=0, keepdims=True) * ROUTED_SCALE
    return eidx, wts


def _outproj_kernel(y_ref, wo_ref, x_ref, g1_ref, nw_ref, sh_ref, sc_ref, rw_ref, rb_ref, tri_ref,
                    x1_ref, h2_ref, h2p_ref, eidx_ref, wts_ref, rank_ref, cnt_ref, carry_scr, *, n_exp):
    i = pl.program_id(0)
    tm, d = x_ref.shape

    @pl.when(i == 0)
    def _():
        carry_scr[...] = jnp.zeros_like(carry_scr)

    x1 = x_ref[...] + g1_ref[...] * jnp.dot(y_ref[...], wo_ref[...], preferred_element_type=F32)
    x1_ref[...] = x1
    ms = jnp.mean(x1 * x1, axis=-1, keepdims=True)
    h2 = (x1 * lax.rsqrt(ms + EPS) * nw_ref[...]) * (1.0 + sc_ref[...]) + sh_ref[...]
    h2b = h2.astype(BF16)
    h2_ref[...] = h2b
    n_sub = d // LANE
    for s in range(n_sub):
        h2p_ref[pl.ds(s, tm, stride=n_sub), :] = h2[:, s * LANE:(s + 1) * LANE]
    logits = jnp.dot(h2, rw_ref[...], preferred_element_type=F32, precision=lax.Precision.HIGHEST)
    logits_t = jnp.transpose(logits)[:n_exp, :]
    eidx, wts = _route(logits_t, rb_ref[...], n_exp)
    eidx_ref[...] = eidx
    wts_ref[...] = wts
    io_e = lax.broadcasted_iota(I32, (n_exp, tm), 0)
    onehot = jnp.zeros((n_exp, tm), F32)
    for k in range(TOP_K):
        onehot = onehot + jnp.where(io_e == eidx[k:k + 1, :], 1.0, 0.0)
    before = jnp.dot(onehot.astype(BF16), tri_ref[...], preferred_element_type=F32) + carry_scr[:, 0:1]
    rank_ref[...] = jnp.concatenate(
        [jnp.sum(jnp.where(io_e == eidx[k:k + 1, :], before, 0.0), axis=0, keepdims=True)
         for k in range(TOP_K)], axis=0).astype(I32)
    carry_scr[...] = carry_scr[...] + jnp.sum(onehot, axis=1, keepdims=True)
    cnt_ref[...] = carry_scr[...].astype(I32)


def _outproj_route(y, wo, x2d, g1, norm_w, shift, scale, rw_pad, rb, seq, n_exp, tm):
    m, d = x2d.shape
    per_b = seq // tm
    n_sub = d // LANE
    tri = jnp.triu(jnp.ones((tm, tm), F32), k=1).astype(BF16)
    bspec = pl.BlockSpec((None, 1, d), lambda i: (i // per_b, 0, 0))
    row = pl.BlockSpec((tm, d), lambda i: (i, 0))
    tok = pl.BlockSpec((TOP_K, tm), lambda i: (0, i))
    return pl.pallas_call(
        functools.partial(_outproj_kernel, n_exp=n_exp),
        grid=(m // tm,),
        in_specs=[row,
                  pl.BlockSpec((d, d), lambda i: (0, 0)),
                  row, bspec,
                  pl.BlockSpec((1, d), lambda i: (0, 0)),
                  bspec, bspec,
                  pl.BlockSpec((d, LANE), lambda i: (0, 0)),
                  pl.BlockSpec((n_exp, 1), lambda i: (0, 0)),
                  pl.BlockSpec((tm, tm), lambda i: (0, 0))],
        out_specs=[row, row,
                   pl.BlockSpec((tm * n_sub, LANE), lambda i: (i, 0)),
                   tok, tok, tok,
                   pl.BlockSpec((n_exp, LANE), lambda i: (0, 0))],
        out_shape=[jax.ShapeDtypeStruct((m, d), F32),
                   jax.ShapeDtypeStruct((m, d), BF16),
                   jax.ShapeDtypeStruct((m * n_sub, LANE), F32),
                   jax.ShapeDtypeStruct((TOP_K, m), I32),
                   jax.ShapeDtypeStruct((TOP_K, m), F32),
                   jax.ShapeDtypeStruct((TOP_K, m), I32),
                   jax.ShapeDtypeStruct((n_exp, LANE), I32)],
        scratch_shapes=[pltpu.VMEM((n_exp, LANE), F32)],
        compiler_params=_cparams("arbitrary"),
        name="outproj_route",
    )(y, wo, x2d, g1, norm_w.reshape(1, d), shift, scale, rw_pad, rb.reshape(n_exp, 1), tri)


def _dispatch_kernel(pends_ref, padded_ref, nact_ref, dest_ref, h2p_ref, xs_hbm, zero_scr, sem_z, sem,
                     *, n_exp, n_blocks, tmd, n_sub):
    i = pl.program_id(0)
    blk_rows = EXPERT_BLOCK * n_sub

    def zero_block(blk):
        start = pl.multiple_of(blk * blk_rows, blk_rows)
        return pltpu.make_async_copy(zero_scr, xs_hbm.at[pl.ds(start, blk_rows), :], sem_z)

    def row_copy(t, slot):
        return pltpu.make_async_copy(h2p_ref.at[pl.ds(pl.multiple_of(t * n_sub, n_sub), n_sub), :],
                                     xs_hbm.at[pl.ds(pl.multiple_of(slot * n_sub, n_sub), n_sub), :], sem)

    @pl.when(i == 0)
    def _():
        zero_scr[...] = jnp.zeros_like(zero_scr)

        @pl.loop(0, n_exp)
        def _(e):
            @pl.when(padded_ref[e] > 0)
            def _():
                zero_block(pends_ref[e] // EXPERT_BLOCK - 1).start()

        @pl.loop(nact_ref[0], n_blocks)
        def _(blk):
            zero_block(blk).start()

        @pl.loop(0, n_exp)
        def _(e):
            @pl.when(padded_ref[e] > 0)
            def _():
                zero_block(0).wait()

        @pl.loop(nact_ref[0], n_blocks)
        def _(blk):
            zero_block(0).wait()

    @pl.loop(0, tmd)
    def _(t):
        for k in range(TOP_K):
            row_copy(t, dest_ref[k, t]).start()

    @pl.loop(0, tmd)
    def _(t):
        for k in range(TOP_K):
            row_copy(0, 0).wait()


def _dispatch(h2p, dest, pends, padded, n_active, n_blocks, n_exp, n_sub, tmd):
    t_tokens = dest.shape[1]
    n_slots = n_blocks * EXPERT_BLOCK
    grid_spec = pltpu.PrefetchScalarGridSpec(
        num_scalar_prefetch=3,
        grid=(t_tokens // tmd,),
        in_specs=[pl.BlockSpec((TOP_K, tmd), lambda i, *_: (0, i), memory_space=pltpu.SMEM),
                  pl.BlockSpec((tmd * n_sub, LANE), lambda i, *_: (i, 0))],
        out_specs=pl.BlockSpec(memory_space=pl.ANY),
        scratch_shapes=[pltpu.VMEM((EXPERT_BLOCK * n_sub, LANE), F32),
                        pltpu.SemaphoreType.DMA(()), pltpu.SemaphoreType.DMA(())],
    )
    return pl.pallas_call(
        functools.partial(_dispatch_kernel, n_exp=n_exp, n_blocks=n_blocks, tmd=tmd, n_sub=n_sub),
        grid_spec=grid_spec,
        out_shape=jax.ShapeDtypeStruct((n_slots * n_sub, LANE), F32),
        compiler_params=_cparams("arbitrary"),
        name="dispatch",
    )(pends, padded, n_active, dest, h2p)


def _expert_kernel(be_ref, nact_ref, xs_ref, wg_ref, wu_ref, wd_ref, ys_ref, wg_s, wu_s, wd_s, *, n_sub, n_out):
    i = pl.program_id(0)
    active = i < nact_ref[0]
    changed = (i == 0) | (be_ref[i] != be_ref[jnp.maximum(i - 1, 0)])

    @pl.when(active & changed)
    def _():
        wg_s[...] = wg_ref[...].astype(BF16)
        wu_s[...] = wu_ref[...].astype(BF16)
        wd_s[...] = wd_ref[...].astype(BF16)

    @pl.when(active)
    def _():
        x = jnp.concatenate([xs_ref[pl.ds(s, EXPERT_BLOCK, stride=n_sub), :] for s in range(n_sub)],
                            axis=1).astype(BF16)
        g = jnp.dot(x, wg_s[...], preferred_element_type=F32)
        u = jnp.dot(x, wu_s[...], preferred_element_type=F32)
        y = jnp.dot((_silu(g) * u).astype(BF16), wd_s[...], preferred_element_type=F32)
        for s in range(n_out):
            ys_ref[pl.ds(s, EXPERT_BLOCK, stride=n_out), :] = y[:, s * LANE:(s + 1) * LANE]

    @pl.when(jnp.logical_not(active))
    def _():
        ys_ref[...] = jnp.zeros_like(ys_ref)


def _experts(xs, blk_expert, n_active, exp_gate, exp_up, exp_down, n_blocks, n_sub):
    n_exp, d, ff = exp_gate.shape
    n_out = d // LANE

    def blk(i, be, na):
        return (jnp.minimum(i, na[0] - 1), 0)

    grid_spec = pltpu.PrefetchScalarGridSpec(
        num_scalar_prefetch=2,
        grid=(n_blocks,),
        in_specs=[pl.BlockSpec((EXPERT_BLOCK * n_sub, LANE), blk),
                  pl.BlockSpec((None, d, ff), lambda i, be, na: (be[i], 0, 0)),
                  pl.BlockSpec((None, d, ff), lambda i, be, na: (be[i], 0, 0)),
                  pl.BlockSpec((None, ff, d), lambda i, be, na: (be[i], 0, 0))],
        out_specs=pl.BlockSpec((EXPERT_BLOCK * n_out, LANE), lambda i, be, na: (i, 0)),
        scratch_shapes=[pltpu.VMEM((d, ff), BF16), pltpu.VMEM((d, ff), BF16), pltpu.VMEM((ff, d), BF16)],
    )
    return pl.pallas_call(
        functools.partial(_expert_kernel, n_sub=n_sub, n_out=n_out),
        grid_spec=grid_spec,
        out_shape=jax.ShapeDtypeStruct((n_blocks * EXPERT_BLOCK * n_out, LANE), F32),
        compiler_params=_cparams("arbitrary"),
        name="experts",
    )(blk_expert, n_active, xs, exp_gate, exp_up, exp_down)


def _combine_kernel(dest_ref, ys_hbm, wt_ref, h2_ref, x1_ref, g2_ref, sg_ref, su_ref, sd_ref, o_ref, buf, sem,
                    *, tmc, n_out):
    @pl.loop(0, tmc)
    def _(t):
        for k in range(TOP_K):
            src = ys_hbm.at[pl.ds(pl.multiple_of(dest_ref[k, t] * n_out, n_out), n_out), :]
            dst = buf.at[pl.ds(pl.multiple_of((k * tmc + t) * n_out, n_out), n_out), :]
            pltpu.make_async_copy(src, dst, sem).start()

    h = h2_ref[...]
    g = jnp.dot(h, sg_ref[...], preferred_element_type=F32)
    u = jnp.dot(h, su_ref[...], preferred_element_type=F32)
    shared = jnp.dot((_silu(g) * u).astype(BF16), sd_ref[...], preferred_element_type=F32)

    @pl.loop(0, tmc)
    def _(t):
        for k in range(TOP_K):
            pltpu.make_async_copy(ys_hbm.at[pl.ds(0, n_out), :], buf.at[pl.ds(0, n_out), :], sem).wait()

    routed = shared
    for k in range(TOP_K):
        rows = jnp.concatenate([buf[pl.ds(k * tmc * n_out + s, tmc, stride=n_out), :] for s in range(n_out)],
                               axis=1)
        routed = routed + wt_ref[:, k:k + 1] * rows
    o_ref[...] = x1_ref[...] + g2_ref[...] * routed


def _combine(dest, ys, wts_t, h2, x1, g2, sg, su, sd, seq, tmc):
    m, d = x1.shape
    ff = sg.shape[1]
    n_out = d // LANE
    per_b = seq // tmc
    row = lambda dt=None: pl.BlockSpec((tmc, d), lambda i: (i, 0))
    return pl.pallas_call(
        functools.partial(_combine_kernel, tmc=tmc, n_out=n_out),
        grid=(m // tmc,),
        in_specs=[pl.BlockSpec((TOP_K, tmc), lambda i: (0, i), memory_space=pltpu.SMEM),
                  pl.BlockSpec(memory_space=pl.ANY),
                  pl.BlockSpec((tmc, TOP_K), lambda i: (i, 0)),
                  row(), row(),
                  pl.BlockSpec((None, 1, d), lambda i: (i // per_b, 0, 0)),
                  pl.BlockSpec((d, ff), lambda i: (0, 0)),
                  pl.BlockSpec((d, ff), lambda i: (0, 0)),
                  pl.BlockSpec((ff, d), lambda i: (0, 0))],
        out_specs=row(),
        out_shape=jax.ShapeDtypeStruct((m, d), F32),
        scratch_shapes=[pltpu.VMEM((TOP_K * tmc * n_out, LANE), F32), pltpu.SemaphoreType.DMA(())],
        compiler_params=_cparams("arbitrary"),
        name="combine",
    )(dest, ys, wts_t, h2, x1, g2, sg, su, sd)


def kernel(x, c, ctx, c_ctx, mod_w, mod_b, norm1_w, w_in, q_norm_w, k_norm_w, w_fourier_out, w_attn_out, w_out,
           norm2_w, router_w, router_b, exp_gate, exp_up, exp_down, shared_gate, shared_up, shared_down):
    batch, seq, d = x.shape
    n_ctx = ctx.shape[1]
    assert mod_w.shape[0] == 1, "single-layer block only"
    assert batch + 1 <= 8 and seq % GRID_W == 0
    fw = w_fourier_out.shape[1]
    att = w_attn_out.shape[1]
    in_cols = w_in.shape[2]
    kvw = (in_cols - fw - att - 2 * d) // 2
    n_kv = kvw // HEAD_DIM
    n_exp = router_w.shape[2]
    assert att == n_kv * Q_GROUP * HEAD_DIM and n_exp <= LANE
    q_off, k_off = fw, fw + att
    v_off = k_off + kvw
    gf_off = v_off + kvw
    ga_off = gf_off + d
    m = batch * seq
    assert q_off % (Q_GROUP * HEAD_DIM) == 0 and kvw % HEAD_DIM == 0 and fw % F_GROUP_DIM == 0
    assert gf_off % min(512, d) == 0 and d % (2 * LANE) == 0

    cond = jnp.concatenate([c, c_ctx[None], jnp.zeros((8 - batch - 1, d), F32)], axis=0)
    mod = _adaln(cond, mod_w[0], mod_b[0])
    sh1, sc1, g1, sh2, sc2, g2 = [mod[:batch, j * d:(j + 1) * d].reshape(batch, 1, d) for j in range(6)]
    csh1 = jnp.broadcast_to(mod[batch, 0:d], (batch, 1, d))
    csc1 = jnp.broadcast_to(mod[batch, d:2 * d], (batch, 1, d))

    w_in_b = w_in[0].astype(BF16)
    cos_t, sin_t = _rope_tables(seq)
    tn = min(512, kvw)
    tm = min(1024, seq)
    modes = (("plain", 0, q_off // tn), ("q", q_off // tn, k_off // tn), ("k", k_off // tn, v_off // tn),
             ("plain", v_off // tn, gf_off // tn), ("sigmoid", gf_off // tn, in_cols // tn))
    proj = _inproj(x.reshape(m, d), sh1, sc1, norm1_w[0], w_in_b, q_norm_w[0], k_norm_w[0], cos_t, sin_t,
                   modes, seq, tm, tn)
    cmodes = (("k_norope", 0, kvw // tn), ("plain", kvw // tn, 2 * kvw // tn))
    tmc_ctx = min(n_ctx, 256)
    kvc = _inproj(ctx.reshape(batch * n_ctx, d), csh1, csc1, norm1_w[0], w_in_b[:, k_off:gf_off], q_norm_w[0],
                  k_norm_w[0], cos_t[:tmc_ctx], sin_t[:tmc_ctx], cmodes, n_ctx, tmc_ctx, tn)

    ya = _attention(proj, kvc, batch, seq, n_ctx, n_kv, q_off, k_off, v_off, tq=min(256, seq), tk=min(512, seq))
    yf = _fourier_mix(proj, batch, seq, fw)
    y = _merge(yf, ya, w_fourier_out[0].astype(BF16), w_attn_out[0].astype(BF16), proj, gf_off, ga_off,
               tm=min(1024, seq), tn=min(512, d))

    rw_pad = jnp.pad(router_w[0], ((0, 0), (0, LANE - n_exp)))
    x1, h2, h2p, eidx, wts, rank, cnt = _outproj_route(
        y, w_out[0].astype(BF16), x.reshape(m, d), g1, norm2_w[0], sh2, sc2, rw_pad, router_b[0], seq, n_exp,
        tm=min(256, seq))

    counts = cnt[:, 0]
    padded = (counts + EXPERT_BLOCK - 1) // EXPERT_BLOCK * EXPERT_BLOCK
    pends = jnp.cumsum(padded).astype(I32)
    pstarts = pends - padded
    n_blocks = (m * TOP_K) // EXPERT_BLOCK + n_exp
    n_active = (pends[-1] // EXPERT_BLOCK).astype(I32)
    blk_start = jnp.minimum(jnp.arange(n_blocks, dtype=I32), n_active - 1) * EXPERT_BLOCK
    blk_expert = jnp.minimum(jnp.sum(blk_start[:, None] >= pends[None, :], axis=1), n_exp - 1).astype(I32)
    dest = rank + jnp.sum(jnp.where(eidx[:, :, None] == jnp.arange(n_exp, dtype=I32), pstarts, 0), axis=-1)

    n_sub = d // LANE
    n_active = n_active.reshape(1)
    xs = _dispatch(h2p, dest, pends, padded.astype(I32), n_active, n_blocks, n_exp, n_sub, tmd=min(128, seq))
    ys = _experts(xs, blk_expert, n_active, exp_gate[0], exp_up[0], exp_down[0], n_blocks, n_sub)
    out = _combine(dest, ys, wts.T, h2, x1, g2, shared_gate[0].astype(BF16), shared_up[0].astype(BF16),
                   shared_down[0].astype(BF16), seq, tmc=min(128, seq))
    return out.reshape(batch, seq, d)
```

```python
import functools
import math

import jax
import jax.numpy as jnp
from jax import lax
from jax.experimental import pallas as pl
from jax.experimental.pallas import tpu as pltpu

F32 = jnp.float32
BF16 = jnp.bfloat16
I32 = jnp.int32

GRID_W = 64
HEAD_DIM = 128
Q_GROUP = 4
ROPE_FREQS = HEAD_DIM // 4
ROPE_THETA = 10000.0
F_GROUP_DIM = 128
N_EXPERT_GROUPS = 8
TOPK_GROUPS = 4
TOP_K = 8
ROUTED_SCALE = 2.5
EPS = 1e-6
LANE = 128
EXPERT_BLOCK = 256
VMEM_LIMIT = 56 * 1024 * 1024


def _cparams(*sem):
    return pltpu.CompilerParams(dimension_semantics=sem, vmem_limit_bytes=VMEM_LIMIT)


def _silu(v):
    return v * jax.nn.sigmoid(v)


def _adaln_kernel(c_ref, w_ref, b_ref, o_ref):
    a = _silu(c_ref[...])
    o_ref[...] = jnp.dot(a.astype(BF16), w_ref[...].astype(BF16), preferred_element_type=F32) + b_ref[...]


def _adaln(cond_pad, mod_w, mod_b):
    d, n = mod_w.shape
    tn = min(n, 1024)
    return pl.pallas_call(
        _adaln_kernel,
        grid=(n // tn,),
        in_specs=[pl.BlockSpec((8, d), lambda j: (0, 0)),
                  pl.BlockSpec((d, tn), lambda j: (0, j)),
                  pl.BlockSpec((1, tn), lambda j: (0, j))],
        out_specs=pl.BlockSpec((8, tn), lambda j: (0, j)),
        out_shape=jax.ShapeDtypeStruct((8, n), F32),
        compiler_params=_cparams("arbitrary"),
        name="adaln",
    )(cond_pad, mod_w, mod_b.reshape(1, n))


def _head_norm_rope(a, w, cos, sin, scale):
    ms = jnp.mean(a * a, axis=-1, keepdims=True)
    a = a * lax.rsqrt(ms + EPS) * w
    if cos is not None:
        lane = lax.broadcasted_iota(I32, a.shape, 1)
        first = (lane % (2 * ROPE_FREQS)) < ROPE_FREQS
        partner = jnp.where(first, pltpu.roll(a, HEAD_DIM - ROPE_FREQS, 1), pltpu.roll(a, ROPE_FREQS, 1))
        a = a * cos + partner * sin
    if scale is not None:
        a = a * scale
    return a


def _inproj_kernel(x_ref, sh_ref, sc_ref, nw_ref, w_ref, qw_ref, kw_ref, cos_ref, sin_ref, o_ref, h_scr,
                   *, modes, tn):
    j = pl.program_id(1)

    @pl.when(j == 0)
    def _():
        x = x_ref[...]
        ms = jnp.mean(x * x, axis=-1, keepdims=True)
        y = x * lax.rsqrt(ms + EPS) * nw_ref[...]
        h_scr[...] = (y * (1.0 + sc_ref[...]) + sh_ref[...]).astype(BF16)

    acc = jnp.dot(h_scr[...], w_ref[...], preferred_element_type=F32)
    for mode, j0, j1 in modes:
        @pl.when((j >= j0) & (j < j1))
        def _(mode=mode):
            if mode == "plain":
                o_ref[...] = acc.astype(BF16)
            elif mode == "sigmoid":
                o_ref[...] = jax.nn.sigmoid(acc).astype(BF16)
            else:
                rope = mode in ("q", "k")
                w = qw_ref[...] if mode == "q" else kw_ref[...]
                scale = HEAD_DIM ** -0.5 * math.log2(math.e) if mode == "q" else None
                for h in range(tn // HEAD_DIM):
                    sl = slice(h * HEAD_DIM, (h + 1) * HEAD_DIM)
                    a = _head_norm_rope(acc[:, sl], w, cos_ref[...] if rope else None,
                                        sin_ref[...] if rope else None, scale)
                    o_ref[:, sl] = a.astype(BF16)


def _inproj(x2d, shift, scale, norm_w, w_bf16, q_norm_w, k_norm_w, cos_t, sin_t, modes, seq, tm, tn):
    m, d = x2d.shape
    n = w_bf16.shape[1]
    per_b = seq // tm
    pos_blocks = cos_t.shape[0] // tm
    kern = functools.partial(_inproj_kernel, modes=modes, tn=tn)
    return pl.pallas_call(
        kern,
        grid=(m // tm, n // tn),
        in_specs=[pl.BlockSpec((tm, d), lambda i, j: (i, 0)),
                  pl.BlockSpec((None, 1, d), lambda i, j: (i // per_b, 0, 0)),
                  pl.BlockSpec((None, 1, d), lambda i, j: (i // per_b, 0, 0)),
                  pl.BlockSpec((1, d), lambda i, j: (0, 0)),
                  pl.BlockSpec((d, tn), lambda i, j: (0, j)),
                  pl.BlockSpec((1, HEAD_DIM), lambda i, j: (0, 0)),
                  pl.BlockSpec((1, HEAD_DIM), lambda i, j: (0, 0)),
                  pl.BlockSpec((tm, HEAD_DIM), lambda i, j: (i % pos_blocks, 0)),
                  pl.BlockSpec((tm, HEAD_DIM), lambda i, j: (i % pos_blocks, 0))],
        out_specs=pl.BlockSpec((tm, tn), lambda i, j: (i, j)),
        out_shape=jax.ShapeDtypeStruct((m, n), BF16),
        scratch_shapes=[pltpu.VMEM((tm, d), BF16)],
        compiler_params=_cparams("arbitrary", "arbitrary"),
        name="inproj",
    )(x2d, shift, scale, norm_w.reshape(1, d), w_bf16, q_norm_w.reshape(1, HEAD_DIM),
      k_norm_w.reshape(1, HEAD_DIM), cos_t, sin_t)


def _rope_tables(seq):
    rows = seq // GRID_W
    row = jnp.repeat(jnp.arange(rows), GRID_W)
    col = jnp.tile(jnp.arange(GRID_W), rows)
    pos = jnp.stack([row, col], axis=-1).astype(F32)
    inv_freq = ROPE_THETA ** (-jnp.arange(ROPE_FREQS, dtype=F32) / ROPE_FREQS)
    ang = pos[:, :, None] * inv_freq
    cos, sin = jnp.cos(ang), jnp.sin(ang)
    cos_t = jnp.concatenate([cos[:, 0], cos[:, 0], cos[:, 1], cos[:, 1]], axis=-1)
    sin_t = jnp.concatenate([-sin[:, 0], sin[:, 0], -sin[:, 1], sin[:, 1]], axis=-1)
    return cos_t, sin_t


def _attn_kernel(q_ref, k_ref, v_ref, kc_ref, vc_ref, o_ref, vt_scr, vct_scr, *, tq, tk, n_chunks):
    @pl.when(pl.program_id(2) == 0)
    def _():
        for c in range(n_chunks):
            vt_scr[c] = jnp.transpose(v_ref[c * tk:(c + 1) * tk, :].astype(F32)).astype(BF16)
        vct_scr[...] = jnp.transpose(vc_ref[...].astype(F32)).astype(BF16)

    q = jnp.concatenate([q_ref[:, h * HEAD_DIM:(h + 1) * HEAD_DIM] for h in range(Q_GROUP)], axis=0)
    qt = jnp.transpose(q.astype(F32)).astype(BF16)
    m_cols = Q_GROUP * tq

    def scores(kb):
        return jnp.dot(kb, qt, preferred_element_type=F32)

    def update(st, vtb, carry):
        m, l, acc = carry
        m_new = jnp.maximum(m, jnp.max(st, axis=0, keepdims=True))
        alpha = jnp.exp2(m - m_new)
        pt = jnp.exp2(st - m_new)
        l = alpha * l + jnp.sum(pt, axis=0, keepdims=True)
        acc = alpha * acc + jnp.dot(vtb, pt.astype(BF16), preferred_element_type=F32)
        return m_new, l, acc

    carry = (jnp.full((1, m_cols), -jnp.inf, F32), jnp.zeros((1, m_cols), F32),
             jnp.zeros((HEAD_DIM, m_cols), F32))
    st = scores(k_ref[0:tk, :])
    for c in range(n_chunks):
        st_next = scores(k_ref[(c + 1) * tk:(c + 2) * tk, :]) if c + 1 < n_chunks else scores(kc_ref[...])
        carry = update(st, vt_scr[c], carry)
        st = st_next
    _, l, acc = update(st, vct_scr[...], carry)
    o = jnp.transpose(acc / l)
    for h in range(Q_GROUP):
        o_ref[:, h * HEAD_DIM:(h + 1) * HEAD_DIM] = o[h * tq:(h + 1) * tq].astype(BF16)


def _attention(proj, kvc, batch, seq, n_ctx, n_kv, q_off, k_off, v_off, tq, tk):
    gw = Q_GROUP * HEAD_DIM
    per_b = seq // tq
    n_chunks = seq // tk
    kern = functools.partial(_attn_kernel, tq=tq, tk=tk, n_chunks=n_chunks)
    return pl.pallas_call(
        kern,
        grid=(batch, n_kv, per_b),
        in_specs=[pl.BlockSpec((tq, gw), lambda b, g, i: (b * per_b + i, q_off // gw + g)),
                  pl.BlockSpec((seq, HEAD_DIM), lambda b, g, i: (b, k_off // HEAD_DIM + g)),
                  pl.BlockSpec((seq, HEAD_DIM), lambda b, g, i: (b, v_off // HEAD_DIM + g)),
                  pl.BlockSpec((n_ctx, HEAD_DIM), lambda b, g, i: (b, g)),
                  pl.BlockSpec((n_ctx, HEAD_DIM), lambda b, g, i: (b, n_kv + g))],
        out_specs=pl.BlockSpec((tq, gw), lambda b, g, i: (b * per_b + i, g)),
        out_shape=jax.ShapeDtypeStruct((batch * seq, n_kv * gw), BF16),
        scratch_shapes=[pltpu.VMEM((n_chunks, HEAD_DIM, tk), BF16), pltpu.VMEM((HEAD_DIM, n_ctx), BF16)],
        compiler_params=_cparams("arbitrary", "arbitrary", "arbitrary"),
        name="attention",
    )(proj, proj, proj, kvc, kvc)


def _dft_mats(n_rows, n_cols):
    def cs(k, n):
        ang = (2.0 * math.pi / n) * (k % n).astype(F32)
        return jnp.cos(ang), jnp.sin(ang)

    ch = jnp.arange(F_GROUP_DIM)
    cc, sc = cs(ch[:, None] * ch[None, :], F_GROUP_DIM)
    mat_a = jnp.concatenate([cc, -sc], axis=1).astype(BF16)
    r = jnp.arange(n_rows)
    cr, sr = cs(r[:, None] * r[None, :], n_rows)
    mat_1 = jnp.concatenate([jnp.concatenate([cr, sr], axis=1),
                             jnp.concatenate([-sr, cr], axis=1)], axis=0).astype(BF16)
    c = jnp.arange(n_cols)
    c2, s2 = cs(c[:, None] * c[None, :], n_cols)
    mat_2 = jnp.concatenate([c2, s2], axis=1).astype(BF16)
    tr, ts = cs(r[:, None] * c[None, :], n_rows * n_cols)
    tw_r = jnp.repeat(tr, LANE, axis=1)
    tw_i = jnp.repeat(-ts, LANE, axis=1)
    return mat_a, mat_1, mat_2, tw_r, tw_i


def _fourier_a_kernel(u_ref, m_ref, vr_ref, vi_ref, *, groups):
    for g in range(groups):
        sl = slice(g * F_GROUP_DIM, (g + 1) * F_GROUP_DIM)
        r = jnp.dot(u_ref[:, sl], m_ref[...], preferred_element_type=F32)
        vr_ref[:, sl] = r[:, :F_GROUP_DIM].astype(BF16)
        vi_ref[:, sl] = r[:, F_GROUP_DIM:].astype(BF16)


def _fourier_1_kernel(ar_ref, ai_ref, m_ref, twr_ref, twi_ref, pr_ref, pi_ref, *, nseg, fw, n_rows):
    rhs = jnp.concatenate([ar_ref[...], ai_ref[...]], axis=0)
    z = jnp.dot(m_ref[...], rhs, preferred_element_type=F32)
    reps = fw // LANE
    for s in range(nseg):
        zr = z[:n_rows, s * fw:(s + 1) * fw]
        zi = z[n_rows:, s * fw:(s + 1) * fw]
        tr = jnp.tile(twr_ref[:, s * LANE:(s + 1) * LANE], (1, reps))
        ti = jnp.tile(twi_ref[:, s * LANE:(s + 1) * LANE], (1, reps))
        pr_ref[s] = (zr * tr - zi * ti).astype(BF16)
        pi_ref[s] = (zr * ti + zi * tr).astype(BF16)


def _fourier_2_kernel(ar_ref, ai_ref, m_ref, o_ref, *, norm):
    rhs = jnp.concatenate([ar_ref[...], ai_ref[...]], axis=0)
    o_ref[...] = (jnp.dot(m_ref[...], rhs, preferred_element_type=F32) * norm).astype(BF16)


def _fourier_mix(proj, batch, seq, fw):
    n_cols = GRID_W
    n_rows = seq // GRID_W
    groups = fw // F_GROUP_DIM
    mat_a, mat_1, mat_2, tw_r, tw_i = _dft_mats(n_rows, n_cols)
    m = batch * seq
    tm = min(seq, 1024)
    vr, vi = pl.pallas_call(
        functools.partial(_fourier_a_kernel, groups=groups),
        grid=(m // tm,),
        in_specs=[pl.BlockSpec((tm, fw), lambda i: (i, 0)),
                  pl.BlockSpec((F_GROUP_DIM, 2 * F_GROUP_DIM), lambda i: (0, 0))],
        out_specs=[pl.BlockSpec((tm, fw), lambda i: (i, 0))] * 2,
        out_shape=[jax.ShapeDtypeStruct((m, fw), BF16)] * 2,
        compiler_params=_cparams("arbitrary"),
        name="fourier_channels",
    )(proj, mat_a)
    wide = n_cols * fw
    vr = vr.reshape(batch * n_rows, wide)
    vi = vi.reshape(batch * n_rows, wide)
    nseg = min(n_cols, 4)
    pr, pi = pl.pallas_call(
        functools.partial(_fourier_1_kernel, nseg=nseg, fw=fw, n_rows=n_rows),
        grid=(batch, n_cols // nseg),
        in_specs=[pl.BlockSpec((n_rows, nseg * fw), lambda b, j: (b, j)),
                  pl.BlockSpec((n_rows, nseg * fw), lambda b, j: (b, j)),
                  pl.BlockSpec((2 * n_rows, 2 * n_rows), lambda b, j: (0, 0)),
                  pl.BlockSpec((n_rows, nseg * LANE), lambda b, j: (0, j)),
                  pl.BlockSpec((n_rows, nseg * LANE), lambda b, j: (0, j))],
        out_specs=[pl.BlockSpec((nseg, n_rows, fw), lambda b, j: (b * (n_cols // nseg) + j, 0, 0))] * 2,
        out_shape=[jax.ShapeDtypeStruct((batch * n_cols, n_rows, fw), BF16)] * 2,
        compiler_params=_cparams("arbitrary", "arbitrary"),
        name="fourier_rows",
    )(vr, vi, mat_1, tw_r, tw_i)
    wide2 = n_rows * fw
    pr = pr.reshape(batch * n_cols, wide2)
    pi = pi.reshape(batch * n_cols, wide2)
    tc = min(wide2, 8192)
    norm = 1.0 / math.sqrt(seq * F_GROUP_DIM)
    yf = pl.pallas_call(
        functools.partial(_fourier_2_kernel, norm=norm),
        grid=(batch, wide2 // tc),
        in_specs=[pl.BlockSpec((n_cols, tc), lambda b, j: (b, j)),
                  pl.BlockSpec((n_cols, tc), lambda b, j: (b, j)),
                  pl.BlockSpec((n_cols, 2 * n_cols), lambda b, j: (0, 0))],
        out_specs=pl.BlockSpec((n_cols, tc), lambda b, j: (b, j)),
        out_shape=jax.ShapeDtypeStruct((batch * n_cols, wide2), BF16),
        compiler_params=_cparams("arbitrary", "arbitrary"),
        name="fourier_cols",
    )(pr, pi, mat_2)
    return yf.reshape(m, fw)


def _merge_kernel(yf_ref, ya_ref, wfo_ref, wao_ref, gf_ref, ga_ref, o_ref):
    a = jnp.dot(yf_ref[...], wfo_ref[...], preferred_element_type=F32)
    b = jnp.dot(ya_ref[...], wao_ref[...], preferred_element_type=F32)
    o_ref[...] = (gf_ref[...].astype(F32) * a + ga_ref[...].astype(F32) * b).astype(BF16)


def _merge(yf, ya, wfo, wao, proj, gf_off, ga_off, tm, tn):
    m, fw = yf.shape
    aw = ya.shape[1]
    d = wfo.shape[1]
    return pl.pallas_call(
        _merge_kernel,
        grid=(m // tm, d // tn),
        in_specs=[pl.BlockSpec((tm, fw), lambda i, j: (i, 0)),
                  pl.BlockSpec((tm, aw), lambda i, j: (i, 0)),
                  pl.BlockSpec((fw, tn), lambda i, j: (0, j)),
                  pl.BlockSpec((aw, tn), lambda i, j: (0, j)),
                  pl.BlockSpec((tm, tn), lambda i, j: (i, gf_off // tn + j)),
                  pl.BlockSpec((tm, tn), lambda i, j: (i, ga_off // tn + j))],
        out_specs=pl.BlockSpec((tm, tn), lambda i, j: (i, j)),
        out_shape=jax.ShapeDtypeStruct((m, d), BF16),
        compiler_params=_cparams("arbitrary", "arbitrary"),
        name="merge",
    )(yf, ya, wfo, wao, proj, proj)


def _route(logits_t, bias, n_exp):
    gsz = n_exp // N_EXPERT_GROUPS
    tm = logits_t.shape[1]
    neg = -jnp.inf
    scores = jax.nn.sigmoid(logits_t)
    biased = scores + bias
    io_g = lax.broadcasted_iota(I32, (gsz, tm), 0).astype(F32)
    gs = []
    for g in range(N_EXPERT_GROUPS):
        grp = biased[g * gsz:(g + 1) * gsz, :]
        m1 = jnp.max(grp, axis=0, keepdims=True)
        i1 = jnp.min(jnp.where(grp == m1, io_g, float(gsz)), axis=0, keepdims=True)
        m2 = jnp.max(jnp.where(io_g == i1, neg, grp), axis=0, keepdims=True)
        gs.append(m1 + m2)
    cur = jnp.concatenate(gs, axis=0)
    io_n = lax.broadcasted_iota(I32, (N_EXPERT_GROUPS, tm), 0).astype(F32)
    sel = jnp.zeros((N_EXPERT_GROUPS, tm), F32)
    for _ in range(TOPK_GROUPS):
        mx = jnp.max(cur, axis=0, keepdims=True)
        ix = jnp.min(jnp.where(cur == mx, io_n, float(N_EXPERT_GROUPS)), axis=0, keepdims=True)
        hit = io_n == ix
        sel = jnp.where(hit, 1.0, sel)
        cur = jnp.where(hit, neg, cur)
    cur = jnp.concatenate([jnp.where(sel[g:g + 1, :] > 0.0, biased[g * gsz:(g + 1) * gsz, :], neg)
                           for g in range(N_EXPERT_GROUPS)], axis=0)
    io_e = lax.broadcasted_iota(I32, (n_exp, tm), 0).astype(F32)
    eidx, wts = [], []
    for _ in range(TOP_K):
        mx = jnp.max(cur, axis=0, keepdims=True)
        ix = jnp.min(jnp.where(cur == mx, io_e, float(n_exp)), axis=0, keepdims=True)
        hit = io_e == ix
        wts.append(jnp.sum(jnp.where(hit, scores, 0.0), axis=0, keepdims=True))
        eidx.append(ix)
        cur = jnp.where(hit, neg, cur)
    eidx = jnp.concatenate(eidx, axis=0).astype(I32)
    wts = jnp.concatenate(wts, axis=0)
    wts = wts / jnp.sum(wts, axis=0, keepdims=True) * ROUTED_SCALE
    return eidx, wts


def _outproj_kernel(y_ref, wo_ref, x_ref, g1_ref, nw_ref, sh_ref, sc_ref, rw_ref, rb_ref,
                    x1_ref, h2_ref, h2p_ref, eidx_ref, wts_ref, cnt_ref, carry_scr, *, n_exp):
    i = pl.program_id(0)
    tm, d = x_ref.shape

    @pl.when(i == 0)
    def _():
        carry_scr[...] = jnp.zeros_like(carry_scr)

    x1 = x_ref[...] + g1_ref[...] * jnp.dot(y_ref[...], wo_ref[...], preferred_element_type=F32)
    x1_ref[...] = x1
    ms = jnp.mean(x1 * x1, axis=-1, keepdims=True)
    h2 = (x1 * lax.rsqrt(ms + EPS) * nw_ref[...]) * (1.0 + sc_ref[...]) + sh_ref[...]
    h2b = h2.astype(BF16)
    h2_ref[...] = h2b
    n_sub = d // LANE
    for s in range(n_sub):
        h2p_ref[pl.ds(s, tm, stride=n_sub), :] = h2[:, s * LANE:(s + 1) * LANE]
    logits = jnp.dot(h2, rw_ref[...], preferred_element_type=F32, precision=lax.Precision.HIGHEST)
    logits_t = jnp.transpose(logits)[:n_exp, :]
    eidx, wts = _route(logits_t, rb_ref[...], n_exp)
    eidx_ref[...] = eidx
    wts_ref[...] = wts
    io_e = lax.broadcasted_iota(I32, (n_exp, tm), 0)
    onehot = jnp.zeros((n_exp, tm), F32)
    for k in range(TOP_K):
        onehot = onehot + jnp.where(io_e == eidx[k:k + 1, :], 1.0, 0.0)
    carry_scr[...] = carry_scr[...] + jnp.sum(onehot, axis=1, keepdims=True)
    cnt_ref[...] = carry_scr[...].astype(I32)


def _outproj_route(y, wo, x2d, g1, norm_w, shift, scale, rw_pad, rb, seq, n_exp, tm):
    m, d = x2d.shape
    per_b = seq // tm
    n_sub = d // LANE
    bspec = pl.BlockSpec((None, 1, d), lambda i: (i // per_b, 0, 0))
    row = pl.BlockSpec((tm, d), lambda i: (i, 0))
    tok = pl.BlockSpec((TOP_K, tm), lambda i: (0, i))
    return pl.pallas_call(
        functools.partial(_outproj_kernel, n_exp=n_exp),
        grid=(m // tm,),
        in_specs=[row,
                  pl.BlockSpec((d, d), lambda i: (0, 0)),
                  row, bspec,
                  pl.BlockSpec((1, d), lambda i: (0, 0)),
                  bspec, bspec,
                  pl.BlockSpec((d, LANE), lambda i: (0, 0)),
                  pl.BlockSpec((n_exp, 1), lambda i: (0, 0))],
        out_specs=[row, row,
                   pl.BlockSpec((tm * n_sub, LANE), lambda i: (i, 0)),
                   tok, tok,
                   pl.BlockSpec((n_exp, LANE), lambda i: (0, 0))],
        out_shape=[jax.ShapeDtypeStruct((m, d), F32),
                   jax.ShapeDtypeStruct((m, d), BF16),
                   jax.ShapeDtypeStruct((m * n_sub, LANE), F32),
                   jax.ShapeDtypeStruct((TOP_K, m), I32),
                   jax.ShapeDtypeStruct((TOP_K, m), F32),
                   jax.ShapeDtypeStruct((n_exp, LANE), I32)],
        scratch_shapes=[pltpu.VMEM((n_exp, LANE), F32)],
        compiler_params=_cparams("arbitrary"),
        name="outproj_route",
    )(y, wo, x2d, g1, norm_w.reshape(1, d), shift, scale, rw_pad, rb.reshape(n_exp, 1))


def _expert_kernel(be_ref, bwin_ref, brem_ref, bval_ref, nact_ref, cur_win, nxt_win,
                   h2p_hbm, wg_ref, wu_ref, wd_ref, ytk_hbm,
                   wg_s, wu_s, wd_s, xbuf, ybuf, sem_g, sem_s, sem_z, *, n_sub, n_tok, trash_base):
    i = pl.program_id(0)
    nact = nact_ref[0]
    p = lax.rem(i, 2)
    bm = EXPERT_BLOCK
    unroll = 8

    def token_of(f):
        return f & (n_tok - 1) if n_tok & (n_tok - 1) == 0 else lax.rem(f, n_tok)

    def gather_copy(tok, r, par):
        return pltpu.make_async_copy(h2p_hbm.at[pl.ds(pl.multiple_of(tok * n_sub, n_sub), n_sub), :],
                                     xbuf.at[par, pl.ds(pl.multiple_of(r * n_sub, n_sub), n_sub), :],
                                     sem_g.at[par])

    def scatter_copy(r, row, par):
        return pltpu.make_async_copy(ybuf.at[par, pl.ds(pl.multiple_of(r * n_sub, n_sub), n_sub), :],
                                     ytk_hbm.at[pl.ds(pl.multiple_of(row * n_sub, n_sub), n_sub), :],
                                     sem_s.at[par])

    def start_gathers(win_ref, blk, par):
        rem = brem_ref[blk]

        @pl.loop(0, bm // unroll)
        def _(q):
            for u in range(unroll):
                r = q * unroll + u
                gather_copy(token_of(win_ref[0, rem + r]), r, par).start()

    def wait_gathers(par):
        @pl.loop(0, bm // unroll)
        def _(q):
            for u in range(unroll):
                gather_copy(0, 0, par).wait()

    def start_scatters(par):
        rem = brem_ref[i]
        valid = bval_ref[i]
        trash = trash_base + par * bm

        @pl.loop(0, bm // unroll)
        def _(q):
            for u in range(unroll):
                r = q * unroll + u
                scatter_copy(r, jnp.where(r < valid, cur_win[0, rem + r], trash + r), par).start()

    def wait_scatters(par):
        @pl.loop(0, bm // unroll)
        def _(q):
            for u in range(unroll):
                scatter_copy(0, 0, par).wait()

    @pl.when(i == 0)
    def _():
        ybuf[0] = jnp.zeros(ybuf.shape[1:], F32)
        for par in range(2):
            start = (trash_base + par * bm) * n_sub
            cp = pltpu.make_async_copy(ybuf.at[0], ytk_hbm.at[pl.ds(start, bm * n_sub), :], sem_z)
            cp.start()
            cp.wait()
        start_gathers(cur_win, 0, 0)

    @pl.when(i < nact)
    def _():
        @pl.when(i + 1 < nact)
        def _():
            start_gathers(nxt_win, i + 1, 1 - p)

        wait_gathers(p)

        @pl.when(i >= 2)
        def _():
            wait_scatters(p)

        @pl.when((i == 0) | (be_ref[i] != be_ref[jnp.maximum(i - 1, 0)]))
        def _():
            wg_s[...] = wg_ref[...].astype(BF16)
            wu_s[...] = wu_ref[...].astype(BF16)
            wd_s[...] = wd_ref[...].astype(BF16)

        x = jnp.concatenate([xbuf[p, pl.ds(s, bm, stride=n_sub), :] for s in range(n_sub)], axis=1).astype(BF16)
        g = jnp.dot(x, wg_s[...], preferred_element_type=F32)
        u = jnp.dot(x, wu_s[...], preferred_element_type=F32)
        y = jnp.dot((_silu(g) * u).astype(BF16), wd_s[...], preferred_element_type=F32)
        for s in range(n_sub):
            ybuf[p, pl.ds(s, bm, stride=n_sub), :] = y[:, s * LANE:(s + 1) * LANE]
        start_scatters(p)

        @pl.when(i == nact - 1)
        def _():
            @pl.when(i >= 1)
            def _():
                wait_scatters(1 - p)

            wait_scatters(p)


def _experts(h2p, order, blk_expert, blk_off, blk_valid, n_active, exp_gate, exp_up, exp_down, n_blocks, n_tok):
    n_exp, d, ff = exp_gate.shape
    n_sub = d // LANE
    bm = EXPERT_BLOCK
    n_win = order.shape[0] // bm
    pieces = order.reshape(n_win, 1, bm)
    windows = jnp.concatenate([pieces, jnp.roll(pieces, -1, axis=0)], axis=-1)
    blk_win = blk_off // bm
    blk_rem = blk_off - blk_win * bm
    trash_base = TOP_K * n_tok

    def win(step_shift):
        def index_map(i, be, bwin, *_):
            return (bwin[jnp.minimum(i + step_shift, n_blocks - 1)], 0, 0)
        return pl.BlockSpec((None, 1, 2 * bm), index_map, memory_space=pltpu.SMEM)

    grid_spec = pltpu.PrefetchScalarGridSpec(
        num_scalar_prefetch=5,
        grid=(n_blocks,),
        in_specs=[win(0), win(1),
                  pl.BlockSpec(memory_space=pl.ANY),
                  pl.BlockSpec((None, d, ff), lambda i, be, *_: (be[i], 0, 0)),
                  pl.BlockSpec((None, d, ff), lambda i, be, *_: (be[i], 0, 0)),
                  pl.BlockSpec((None, ff, d), lambda i, be, *_: (be[i], 0, 0))],
        out_specs=pl.BlockSpec(memory_space=pl.ANY),
        scratch_shapes=[pltpu.VMEM((d, ff), BF16), pltpu.VMEM((d, ff), BF16), pltpu.VMEM((ff, d), BF16),
                        pltpu.VMEM((2, bm * n_sub, LANE), F32), pltpu.VMEM((2, bm * n_sub, LANE), F32),
                        pltpu.SemaphoreType.DMA((2,)), pltpu.SemaphoreType.DMA((2,)),
                        pltpu.SemaphoreType.DMA(())],
    )
    return pl.pallas_call(
        functools.partial(_expert_kernel, n_sub=n_sub, n_tok=n_tok, trash_base=trash_base),
        grid_spec=grid_spec,
        out_shape=jax.ShapeDtypeStruct(((trash_base + 2 * bm) * n_sub, LANE), F32),
        compiler_params=_cparams("arbitrary"),
        name="experts",
    )(blk_expert, blk_win, blk_rem, blk_valid, n_active, windows, windows, h2p, exp_gate, exp_up, exp_down)


def _combine_kernel(*refs, tmc, n_out):
    yk_refs = refs[:TOP_K]
    wt_ref, h2_ref, x1_ref, g2_ref, sg_ref, su_ref, sd_ref, o_ref = refs[TOP_K:]
    h = h2_ref[...]
    g = jnp.dot(h, sg_ref[...], preferred_element_type=F32)
    u = jnp.dot(h, su_ref[...], preferred_element_type=F32)
    routed = jnp.dot((_silu(g) * u).astype(BF16), sd_ref[...], preferred_element_type=F32)
    for k in range(TOP_K):
        rows = jnp.concatenate([yk_refs[k][pl.ds(s, tmc, stride=n_out), :] for s in range(n_out)], axis=1)
        routed = routed + wt_ref[:, k:k + 1] * rows
    o_ref[...] = x1_ref[...] + g2_ref[...] * routed


def _combine(ytk, wts_t, h2, x1, g2, sg, su, sd, seq, tmc):
    m, d = x1.shape
    ff = sg.shape[1]
    n_out = d // LANE
    per_b = seq // tmc
    tiles = m // tmc
    row = pl.BlockSpec((tmc, d), lambda i: (i, 0))
    yk_specs = [pl.BlockSpec((tmc * n_out, LANE), lambda i, k=k: (k * tiles + i, 0)) for k in range(TOP_K)]
    return pl.pallas_call(
        functools.partial(_combine_kernel, tmc=tmc, n_out=n_out),
        grid=(tiles,),
        in_specs=yk_specs + [pl.BlockSpec((tmc, TOP_K), lambda i: (i, 0)),
                             row, row,
                             pl.BlockSpec((None, 1, d), lambda i: (i // per_b, 0, 0)),
                             pl.BlockSpec((d, ff), lambda i: (0, 0)),
                             pl.BlockSpec((d, ff), lambda i: (0, 0)),
                             pl.BlockSpec((ff, d), lambda i: (0, 0))],
        out_specs=row,
        out_shape=jax.ShapeDtypeStruct((m, d), F32),
        compiler_params=_cparams("arbitrary"),
        name="combine",
    )(*([ytk] * TOP_K), wts_t, h2, x1, g2, sg, su, sd)


def kernel(x, c, ctx, c_ctx, mod_w, mod_b, norm1_w, w_in, q_norm_w, k_norm_w, w_fourier_out, w_attn_out, w_out,
           norm2_w, router_w, router_b, exp_gate, exp_up, exp_down, shared_gate, shared_up, shared_down):
    batch, seq, d = x.shape
    n_ctx = ctx.shape[1]
    assert mod_w.shape[0] == 1, "single-layer block only"
    assert batch + 1 <= 8 and seq % GRID_W == 0
    fw = w_fourier_out.shape[1]
    att = w_attn_out.shape[1]
    in_cols = w_in.shape[2]
    kvw = (in_cols - fw - att - 2 * d) // 2
    n_kv = kvw // HEAD_DIM
    n_exp = router_w.shape[2]
    assert att == n_kv * Q_GROUP * HEAD_DIM and n_exp <= LANE
    q_off, k_off = fw, fw + att
    v_off = k_off + kvw
    gf_off = v_off + kvw
    ga_off = gf_off + d
    m = batch * seq
    assert q_off % (Q_GROUP * HEAD_DIM) == 0 and kvw % HEAD_DIM == 0 and fw % F_GROUP_DIM == 0
    assert gf_off % min(512, d) == 0 and d % (2 * LANE) == 0

    cond = jnp.concatenate([c, c_ctx[None], jnp.zeros((8 - batch - 1, d), F32)], axis=0)
    mod = _adaln(cond, mod_w[0], mod_b[0])
    sh1, sc1, g1, sh2, sc2, g2 = [mod[:batch, j * d:(j + 1) * d].reshape(batch, 1, d) for j in range(6)]
    csh1 = jnp.broadcast_to(mod[batch, 0:d], (batch, 1, d))
    csc1 = jnp.broadcast_to(mod[batch, d:2 * d], (batch, 1, d))

    w_in_b = w_in[0].astype(BF16)
    cos_t, sin_t = _rope_tables(seq)
    tn = min(512, kvw)
    tm = min(1024, seq)
    modes = (("plain", 0, q_off // tn), ("q", q_off // tn, k_off // tn), ("k", k_off // tn, v_off // tn),
             ("plain", v_off // tn, gf_off // tn), ("sigmoid", gf_off // tn, in_cols // tn))
    proj = _inproj(x.reshape(m, d), sh1, sc1, norm1_w[0], w_in_b, q_norm_w[0], k_norm_w[0], cos_t, sin_t,
                   modes, seq, tm, tn)
    cmodes = (("k_norope", 0, kvw // tn), ("plain", kvw // tn, 2 * kvw // tn))
    tmc_ctx = min(n_ctx, 256)
    kvc = _inproj(ctx.reshape(batch * n_ctx, d), csh1, csc1, norm1_w[0], w_in_b[:, k_off:gf_off], q_norm_w[0],
                  k_norm_w[0], cos_t[:tmc_ctx], sin_t[:tmc_ctx], cmodes, n_ctx, tmc_ctx, tn)

    ya = _attention(proj, kvc, batch, seq, n_ctx, n_kv, q_off, k_off, v_off, tq=min(256, seq), tk=min(1024, seq))
    yf = _fourier_mix(proj, batch, seq, fw)
    y = _merge(yf, ya, w_fourier_out[0].astype(BF16), w_attn_out[0].astype(BF16), proj, gf_off, ga_off,
               tm=min(1024, seq), tn=min(512, d))

    rw_pad = jnp.pad(router_w[0], ((0, 0), (0, LANE - n_exp)))
    x1, h2, h2p, eidx, wts, cnt = _outproj_route(
        y, w_out[0].astype(BF16), x.reshape(m, d), g1, norm2_w[0], sh2, sc2, rw_pad, router_b[0], seq, n_exp,
        tm=min(256, seq))

    n_assign = m * TOP_K
    assert n_assign % EXPERT_BLOCK == 0
    _, order = lax.sort((eidx.reshape(-1), jnp.arange(n_assign, dtype=I32)), num_keys=1)
    counts = cnt[:, 0]
    starts = jnp.cumsum(counts) - counts
    nb_e = (counts + EXPERT_BLOCK - 1) // EXPERT_BLOCK
    cum_nb = jnp.cumsum(nb_e)
    n_blocks = n_assign // EXPERT_BLOCK + n_exp
    n_active = jnp.maximum(cum_nb[-1], 1).astype(I32)
    blk = jnp.minimum(jnp.arange(n_blocks, dtype=I32), n_active - 1)
    blk_expert = jnp.minimum(jnp.sum(blk[:, None] >= cum_nb[None, :], axis=1), n_exp - 1).astype(I32)
    blk_j = blk - jnp.take(cum_nb - nb_e, blk_expert)
    blk_off = (jnp.take(starts, blk_expert) + blk_j * EXPERT_BLOCK).astype(I32)
    blk_valid = jnp.clip(jnp.take(counts, blk_expert) - blk_j * EXPERT_BLOCK, 0, EXPERT_BLOCK).astype(I32)

    ytk = _experts(h2p, order, blk_expert, blk_off, blk_valid, n_active.reshape(1), exp_gate[0], exp_up[0],
                   exp_down[0], n_blocks, m)
    out = _combine(ytk, wts.T, h2, x1, g2, shared_gate[0].astype(BF16), shared_up[0].astype(BF16),
                   shared_down[0].astype(BF16), seq, tmc=min(128, seq))
    return out.reshape(batch, seq, d)
```

```python
import functools
import math

import jax
import jax.numpy as jnp
from jax import lax
from jax.experimental import pallas as pl
from jax.experimental.pallas import tpu as pltpu

F32 = jnp.float32
BF16 = jnp.bfloat16
I32 = jnp.int32

GRID_W = 64
HEAD_DIM = 128
Q_GROUP = 4
ROPE_FREQS = HEAD_DIM // 4
ROPE_THETA = 10000.0
F_GROUP_DIM = 128
N_EXPERT_GROUPS = 8
TOPK_GROUPS = 4
TOP_K = 8
ROUTED_SCALE = 2.5
EPS = 1e-6
LANE = 128
EXPERT_BLOCK = 256
VMEM_LIMIT = 56 * 1024 * 1024


def _cparams(*sem):
    return pltpu.CompilerParams(dimension_semantics=sem, vmem_limit_bytes=VMEM_LIMIT)


def _silu(v):
    return v * jax.nn.sigmoid(v)


def _adaln_kernel(c_ref, w_ref, b_ref, o_ref):
    a = _silu(c_ref[...])
    o_ref[...] = jnp.dot(a.astype(BF16), w_ref[...].astype(BF16), preferred_element_type=F32) + b_ref[...]


def _adaln(cond_pad, mod_w, mod_b):
    d, n = mod_w.shape
    tn = min(n, 1024)
    return pl.pallas_call(
        _adaln_kernel,
        grid=(n // tn,),
        in_specs=[pl.BlockSpec((8, d), lambda j: (0, 0)),
                  pl.BlockSpec((d, tn), lambda j: (0, j)),
                  pl.BlockSpec((1, tn), lambda j: (0, j))],
        out_specs=pl.BlockSpec((8, tn), lambda j: (0, j)),
        out_shape=jax.ShapeDtypeStruct((8, n), F32),
        compiler_params=_cparams("arbitrary"),
        name="adaln",
    )(cond_pad, mod_w, mod_b.reshape(1, n))


def _head_norm_rope(a, w, cos, sin, scale):
    ms = jnp.mean(a * a, axis=-1, keepdims=True)
    a = a * lax.rsqrt(ms + EPS) * w
    if cos is not None:
        lane = lax.broadcasted_iota(I32, a.shape, 1)
        first = (lane % (2 * ROPE_FREQS)) < ROPE_FREQS
        partner = jnp.where(first, pltpu.roll(a, HEAD_DIM - ROPE_FREQS, 1), pltpu.roll(a, ROPE_FREQS, 1))
        a = a * cos + partner * sin
    if scale is not None:
        a = a * scale
    return a


def _inproj_kernel(x_ref, sh_ref, sc_ref, nw_ref, w_ref, qw_ref, kw_ref, cos_ref, sin_ref, o_ref, h_scr,
                   *, modes, tn):
    j = pl.program_id(1)

    @pl.when(j == 0)
    def _():
        x = x_ref[...]
        ms = jnp.mean(x * x, axis=-1, keepdims=True)
        y = x * lax.rsqrt(ms + EPS) * nw_ref[...]
        h_scr[...] = (y * (1.0 + sc_ref[...]) + sh_ref[...]).astype(BF16)

    def tile(c0, c1):
        return jnp.dot(h_scr[...], w_ref[:, c0:c1], preferred_element_type=F32)

    piece = 2 * HEAD_DIM if tn % (2 * HEAD_DIM) == 0 else HEAD_DIM
    for mode, j0, j1 in modes:
        @pl.when((j >= j0) & (j < j1))
        def _(mode=mode):
            if mode == "plain":
                o_ref[...] = tile(0, tn).astype(BF16)
                return
            rope = mode in ("q", "k")
            w = qw_ref[...] if mode == "q" else kw_ref[...]
            scale = HEAD_DIM ** -0.5 * math.log2(math.e) if mode == "q" else None
            for c0 in range(0, tn, piece):
                acc = tile(c0, c0 + piece)
                if mode == "sigmoid":
                    o_ref[:, c0:c0 + piece] = jax.nn.sigmoid(acc).astype(BF16)
                    continue
                for h in range(piece // HEAD_DIM):
                    a = _head_norm_rope(acc[:, h * HEAD_DIM:(h + 1) * HEAD_DIM], w, cos_ref[...] if rope else None,
                                        sin_ref[...] if rope else None, scale)
                    o_ref[:, c0 + h * HEAD_DIM:c0 + (h + 1) * HEAD_DIM] = a.astype(BF16)


def _inproj(x2d, shift, scale, norm_w, w_bf16, q_norm_w, k_norm_w, cos_t, sin_t, modes, seq, tm, tn):
    m, d = x2d.shape
    n = w_bf16.shape[1]
    per_b = seq // tm
    pos_blocks = cos_t.shape[0] // tm
    kern = functools.partial(_inproj_kernel, modes=modes, tn=tn)
    return pl.pallas_call(
        kern,
        grid=(m // tm, n // tn),
        in_specs=[pl.BlockSpec((tm, d), lambda i, j: (i, 0)),
                  pl.BlockSpec((None, 1, d), lambda i, j: (i // per_b, 0, 0)),
                  pl.BlockSpec((None, 1, d), lambda i, j: (i // per_b, 0, 0)),
                  pl.BlockSpec((1, d), lambda i, j: (0, 0)),
                  pl.BlockSpec((d, tn), lambda i, j: (0, j)),
                  pl.BlockSpec((1, HEAD_DIM), lambda i, j: (0, 0)),
                  pl.BlockSpec((1, HEAD_DIM), lambda i, j: (0, 0)),
                  pl.BlockSpec((tm, HEAD_DIM), lambda i, j: (i % pos_blocks, 0)),
                  pl.BlockSpec((tm, HEAD_DIM), lambda i, j: (i % pos_blocks, 0))],
        out_specs=pl.BlockSpec((tm, tn), lambda i, j: (i, j)),
        out_shape=jax.ShapeDtypeStruct((m, n), BF16),
        scratch_shapes=[pltpu.VMEM((tm, d), BF16)],
        compiler_params=_cparams("arbitrary", "arbitrary"),
        name="inproj",
    )(x2d, shift, scale, norm_w.reshape(1, d), w_bf16, q_norm_w.reshape(1, HEAD_DIM),
      k_norm_w.reshape(1, HEAD_DIM), cos_t, sin_t)


def _rope_tables(seq):
    rows = seq // GRID_W
    row = jnp.repeat(jnp.arange(rows), GRID_W)
    col = jnp.tile(jnp.arange(GRID_W), rows)
    pos = jnp.stack([row, col], axis=-1).astype(F32)
    inv_freq = ROPE_THETA ** (-jnp.arange(ROPE_FREQS, dtype=F32) / ROPE_FREQS)
    ang = pos[:, :, None] * inv_freq
    cos, sin = jnp.cos(ang), jnp.sin(ang)
    cos_t = jnp.concatenate([cos[:, 0], cos[:, 0], cos[:, 1], cos[:, 1]], axis=-1)
    sin_t = jnp.concatenate([-sin[:, 0], sin[:, 0], -sin[:, 1], sin[:, 1]], axis=-1)
    return cos_t, sin_t


def _attn_kernel(q_ref, k_ref, v_ref, kc_ref, vc_ref, o_ref, vt_scr, vct_scr, *, tq, tk, n_chunks):
    @pl.when(pl.program_id(2) == 0)
    def _():
        for c in range(n_chunks):
            vt_scr[c] = jnp.transpose(v_ref[c * tk:(c + 1) * tk, :].astype(F32)).astype(BF16)
        vct_scr[...] = jnp.transpose(vc_ref[...].astype(F32)).astype(BF16)

    q = jnp.concatenate([q_ref[:, h * HEAD_DIM:(h + 1) * HEAD_DIM] for h in range(Q_GROUP)], axis=0)
    qt = jnp.transpose(q.astype(F32)).astype(BF16)
    m_cols = Q_GROUP * tq

    def scores(kb):
        return jnp.dot(kb, qt, preferred_element_type=F32)

    def update(st, vtb, carry):
        m, l, acc = carry
        m_new = jnp.maximum(m, jnp.max(st, axis=0, keepdims=True))
        alpha = jnp.exp2(m - m_new)
        pt = jnp.exp2(st - m_new)
        l = alpha * l + jnp.sum(pt, axis=0, keepdims=True)
        acc = alpha * acc + jnp.dot(vtb, pt.astype(BF16), preferred_element_type=F32)
        return m_new, l, acc

    carry = (jnp.full((1, m_cols), -jnp.inf, F32), jnp.zeros((1, m_cols), F32),
             jnp.zeros((HEAD_DIM, m_cols), F32))
    st = scores(k_ref[0:tk, :])
    for c in range(n_chunks):
        st_next = scores(k_ref[(c + 1) * tk:(c + 2) * tk, :]) if c + 1 < n_chunks else scores(kc_ref[...])
        carry = update(st, vt_scr[c], carry)
        st = st_next
    _, l, acc = update(st, vct_scr[...], carry)
    o = jnp.transpose(acc / l)
    for h in range(Q_GROUP):
        o_ref[:, h * HEAD_DIM:(h + 1) * HEAD_DIM] = o[h * tq:(h + 1) * tq].astype(BF16)


def _attention(proj, kvc, batch, seq, n_ctx, n_kv, q_off, k_off, v_off, tq, tk):
    gw = Q_GROUP * HEAD_DIM
    per_b = seq // tq
    n_chunks = seq // tk
    kern = functools.partial(_attn_kernel, tq=tq, tk=tk, n_chunks=n_chunks)
    return pl.pallas_call(
        kern,
        grid=(batch, n_kv, per_b),
        in_specs=[pl.BlockSpec((tq, gw), lambda b, g, i: (b * per_b + i, q_off // gw + g)),
                  pl.BlockSpec((seq, HEAD_DIM), lambda b, g, i: (b, k_off // HEAD_DIM + g)),
                  pl.BlockSpec((seq, HEAD_DIM), lambda b, g, i: (b, v_off // HEAD_DIM + g)),
                  pl.BlockSpec((n_ctx, HEAD_DIM), lambda b, g, i: (b, g)),
                  pl.BlockSpec((n_ctx, HEAD_DIM), lambda b, g, i: (b, n_kv + g))],
        out_specs=pl.BlockSpec((tq, gw), lambda b, g, i: (b * per_b + i, g)),
        out_shape=jax.ShapeDtypeStruct((batch * seq, n_kv * gw), BF16),
        scratch_shapes=[pltpu.VMEM((n_chunks, HEAD_DIM, tk), BF16), pltpu.VMEM((HEAD_DIM, n_ctx), BF16)],
        compiler_params=_cparams("arbitrary", "arbitrary", "arbitrary"),
        name="attention",
    )(proj, proj, proj, kvc, kvc)


def _dft_mats(n_rows, n_cols):
    def cs(k, n):
        ang = (2.0 * math.pi / n) * (k % n).astype(F32)
        return jnp.cos(ang), jnp.sin(ang)

    ch = jnp.arange(F_GROUP_DIM)
    cc, sc = cs(ch[:, None] * ch[None, :], F_GROUP_DIM)
    mat_a = jnp.concatenate([cc, -sc], axis=1).astype(BF16)
    r = jnp.arange(n_rows)
    cr, sr = cs(r[:, None] * r[None, :], n_rows)
    mat_1 = jnp.concatenate([jnp.concatenate([cr, sr], axis=1),
                             jnp.concatenate([-sr, cr], axis=1)], axis=0).astype(BF16)
    c = jnp.arange(n_cols)
    c2, s2 = cs(c[:, None] * c[None, :], n_cols)
    mat_2 = jnp.concatenate([c2, s2], axis=1).astype(BF16)
    tr, ts = cs(r[:, None] * c[None, :], n_rows * n_cols)
    tw_r = jnp.repeat(tr, LANE, axis=1)
    tw_i = jnp.repeat(-ts, LANE, axis=1)
    return mat_a, mat_1, mat_2, tw_r, tw_i


def _fourier_a_kernel(u_ref, m_ref, vr_ref, vi_ref, *, groups):
    for g in range(groups):
        sl = slice(g * F_GROUP_DIM, (g + 1) * F_GROUP_DIM)
        r = jnp.dot(u_ref[:, sl], m_ref[...], preferred_element_type=F32)
        vr_ref[:, sl] = r[:, :F_GROUP_DIM].astype(BF16)
        vi_ref[:, sl] = r[:, F_GROUP_DIM:].astype(BF16)


def _fourier_1_kernel(ar_ref, ai_ref, m_ref, twr_ref, twi_ref, pr_ref, pi_ref, *, nseg, fw, n_rows):
    rhs = jnp.concatenate([ar_ref[...], ai_ref[...]], axis=0)
    z = jnp.dot(m_ref[...], rhs, preferred_element_type=F32)
    reps = fw // LANE
    for s in range(nseg):
        zr = z[:n_rows, s * fw:(s + 1) * fw]
        zi = z[n_rows:, s * fw:(s + 1) * fw]
        tr = jnp.tile(twr_ref[:, s * LANE:(s + 1) * LANE], (1, reps))
        ti = jnp.tile(twi_ref[:, s * LANE:(s + 1) * LANE], (1, reps))
        pr_ref[s] = (zr * tr - zi * ti).astype(BF16)
        pi_ref[s] = (zr * ti + zi * tr).astype(BF16)


def _fourier_2_kernel(ar_ref, ai_ref, m_ref, o_ref, *, norm):
    rhs = jnp.concatenate([ar_ref[...], ai_ref[...]], axis=0)
    o_ref[...] = (jnp.dot(m_ref[...], rhs, preferred_element_type=F32) * norm).astype(BF16)


def _fourier_mix(proj, batch, seq, fw):
    n_cols = GRID_W
    n_rows = seq // GRID_W
    groups = fw // F_GROUP_DIM
    mat_a, mat_1, mat_2, tw_r, tw_i = _dft_mats(n_rows, n_cols)
    m = batch * seq
    tm = min(seq, 1024)
    vr, vi = pl.pallas_call(
        functools.partial(_fourier_a_kernel, groups=groups),
        grid=(m // tm,),
        in_specs=[pl.BlockSpec((tm, fw), lambda i: (i, 0)),
                  pl.BlockSpec((F_GROUP_DIM, 2 * F_GROUP_DIM), lambda i: (0, 0))],
        out_specs=[pl.BlockSpec((tm, fw), lambda i: (i, 0))] * 2,
        out_shape=[jax.ShapeDtypeStruct((m, fw), BF16)] * 2,
        compiler_params=_cparams("arbitrary"),
        name="fourier_channels",
    )(proj, mat_a)
    wide = n_cols * fw
    vr = vr.reshape(batch * n_rows, wide)
    vi = vi.reshape(batch * n_rows, wide)
    nseg = min(n_cols, 4)
    pr, pi = pl.pallas_call(
        functools.partial(_fourier_1_kernel, nseg=nseg, fw=fw, n_rows=n_rows),
        grid=(batch, n_cols // nseg),
        in_specs=[pl.BlockSpec((n_rows, nseg * fw), lambda b, j: (b, j)),
                  pl.BlockSpec((n_rows, nseg * fw), lambda b, j: (b, j)),
                  pl.BlockSpec((2 * n_rows, 2 * n_rows), lambda b, j: (0, 0)),
                  pl.BlockSpec((n_rows, nseg * LANE), lambda b, j: (0, j)),
                  pl.BlockSpec((n_rows, nseg * LANE), lambda b, j: (0, j))],
        out_specs=[pl.BlockSpec((nseg, n_rows, fw), lambda b, j: (b * (n_cols // nseg) + j, 0, 0))] * 2,
        out_shape=[jax.ShapeDtypeStruct((batch * n_cols, n_rows, fw), BF16)] * 2,
        compiler_params=_cparams("arbitrary", "arbitrary"),
        name="fourier_rows",
    )(vr, vi, mat_1, tw_r, tw_i)
    wide2 = n_rows * fw
    pr = pr.reshape(batch * n_cols, wide2)
    pi = pi.reshape(batch * n_cols, wide2)
    tc = min(wide2, 8192)
    norm = 1.0 / math.sqrt(seq * F_GROUP_DIM)
    yf = pl.pallas_call(
        functools.partial(_fourier_2_kernel, norm=norm),
        grid=(batch, wide2 // tc),
        in_specs=[pl.BlockSpec((n_cols, tc), lambda b, j: (b, j)),
                  pl.BlockSpec((n_cols, tc), lambda b, j: (b, j)),
                  pl.BlockSpec((n_cols, 2 * n_cols), lambda b, j: (0, 0))],
        out_specs=pl.BlockSpec((n_cols, tc), lambda b, j: (b, j)),
        out_shape=jax.ShapeDtypeStruct((batch * n_cols, wide2), BF16),
        compiler_params=_cparams("arbitrary", "arbitrary"),
        name="fourier_cols",
    )(pr, pi, mat_2)
    return yf.reshape(m, fw)


def _merge_kernel(yf_ref, ya_ref, wfo_ref, wao_ref, gf_ref, ga_ref, o_ref):
    a = jnp.dot(yf_ref[...], wfo_ref[...], preferred_element_type=F32)
    b = jnp.dot(ya_ref[...], wao_ref[...], preferred_element_type=F32)
    o_ref[...] = (gf_ref[...].astype(F32) * a + ga_ref[...].astype(F32) * b).astype(BF16)


def _merge(yf, ya, wfo, wao, proj, gf_off, ga_off, tm, tn):
    m, fw = yf.shape
    aw = ya.shape[1]
    d = wfo.shape[1]
    return pl.pallas_call(
        _merge_kernel,
        grid=(m // tm, d // tn),
        in_specs=[pl.BlockSpec((tm, fw), lambda i, j: (i, 0)),
                  pl.BlockSpec((tm, aw), lambda i, j: (i, 0)),
                  pl.BlockSpec((fw, tn), lambda i, j: (0, j)),
                  pl.BlockSpec((aw, tn), lambda i, j: (0, j)),
                  pl.BlockSpec((tm, tn), lambda i, j: (i, gf_off // tn + j)),
                  pl.BlockSpec((tm, tn), lambda i, j: (i, ga_off // tn + j))],
        out_specs=pl.BlockSpec((tm, tn), lambda i, j: (i, j)),
        out_shape=jax.ShapeDtypeStruct((m, d), BF16),
        compiler_params=_cparams("arbitrary", "arbitrary"),
        name="merge",
    )(yf, ya, wfo, wao, proj, proj)


def _route(logits_t, bias, n_exp):
    gsz = n_exp // N_EXPERT_GROUPS
    tm = logits_t.shape[1]
    neg = -jnp.inf
    scores = jax.nn.sigmoid(logits_t)
    biased = scores + bias
    io_g = lax.broadcasted_iota(I32, (gsz, tm), 0).astype(F32)
    gs = []
    for g in range(N_EXPERT_GROUPS):
        grp = biased[g * gsz:(g + 1) * gsz, :]
        m1 = jnp.max(grp, axis=0, keepdims=True)
        i1 = jnp.min(jnp.where(grp == m1, io_g, float(gsz)), axis=0, keepdims=True)
        m2 = jnp.max(jnp.where(io_g == i1, neg, grp), axis=0, keepdims=True)
        gs.append(m1 + m2)
    cur = jnp.concatenate(gs, axis=0)
    io_n = lax.broadcasted_iota(I32, (N_EXPERT_GROUPS, tm), 0).astype(F32)
    sel = jnp.zeros((N_EXPERT_GROUPS, tm), F32)
    for _ in range(TOPK_GROUPS):
        mx = jnp.max(cur, axis=0, keepdims=True)
        ix = jnp.min(jnp.where(cur == mx, io_n, float(N_EXPERT_GROUPS)), axis=0, keepdims=True)
        hit = io_n == ix
        sel = jnp.where(hit, 1.0, sel)
        cur = jnp.where(hit, neg, cur)
    cur = jnp.concatenate([jnp.where(sel[g:g + 1, :] > 0.0, biased[g * gsz:(g + 1) * gsz, :], neg)
                           for g in range(N_EXPERT_GROUPS)], axis=0)
    io_e = lax.broadcasted_iota(I32, (n_exp, tm), 0).astype(F32)
    eidx, wts = [], []
    for _ in range(TOP_K):
        mx = jnp.max(cur, axis=0, keepdims=True)
        ix = jnp.min(jnp.where(cur == mx, io_e, float(n_exp)), axis=0, keepdims=True)
        hit = io_e == ix
        wts.append(jnp.sum(jnp.where(hit, scores, 0.0), axis=0, keepdims=True))
        eidx.append(ix)
        cur = jnp.where(hit, neg, cur)
    eidx = jnp.concatenate(eidx, axis=0).astype(I32)
    wts = jnp.concatenate(wts, axis=0)
    wts = wts / jnp.sum(wts, axis=0, keepdims=True) * ROUTED_SCALE
    return eidx, wts


def _outproj_kernel(y_ref, wo_ref, x_ref, g1_ref, nw_ref, sh_ref, sc_ref, rw_ref, rb_ref,
                    x1_ref, h2_ref, h2p_ref, eidx_ref, wts_ref, cnt_ref, carry_scr, *, n_exp):
    i = pl.program_id(0)
    tm, d = x_ref.shape

    @pl.when(i == 0)
    def _():
        carry_scr[...] = jnp.zeros_like(carry_scr)

    x1 = x_ref[...] + g1_ref[...] * jnp.dot(y_ref[...], wo_ref[...], preferred_element_type=F32)
    x1_ref[...] = x1
    ms = jnp.mean(x1 * x1, axis=-1, keepdims=True)
    h2 = (x1 * lax.rsqrt(ms + EPS) * nw_ref[...]) * (1.0 + sc_ref[...]) + sh_ref[...]
    h2b = h2.astype(BF16)
    h2_ref[...] = h2b
    n_sub = d // LANE
    for s in range(n_sub):
        h2p_ref[pl.ds(s, tm, stride=n_sub), :] = h2[:, s * LANE:(s + 1) * LANE]
    h_hi = h2b
    h_lo = (h2 - h_hi.astype(F32)).astype(BF16)
    part = (jnp.dot(h_hi, rw_ref[...], preferred_element_type=F32)
            + jnp.dot(h_lo, rw_ref[...], preferred_element_type=F32))
    logits = part[:, :LANE] + part[:, LANE:]
    logits_t = jnp.transpose(logits)[:n_exp, :]
    eidx, wts = _route(logits_t, rb_ref[...], n_exp)
    eidx_ref[...] = eidx
    wts_ref[...] = wts
    io_e = lax.broadcasted_iota(I32, (n_exp, tm), 0)
    onehot = jnp.zeros((n_exp, tm), F32)
    for k in range(TOP_K):
        onehot = onehot + jnp.where(io_e == eidx[k:k + 1, :], 1.0, 0.0)
    carry_scr[...] = carry_scr[...] + jnp.sum(onehot, axis=1, keepdims=True)
    cnt_ref[...] = carry_scr[...].astype(I32)


def _outproj_route(y, wo, x2d, g1, norm_w, shift, scale, rw_pad, rb, seq, n_exp, tm):
    m, d = x2d.shape
    per_b = seq // tm
    n_sub = d // LANE
    bspec = pl.BlockSpec((None, 1, d), lambda i: (i // per_b, 0, 0))
    row = pl.BlockSpec((tm, d), lambda i: (i, 0))
    tok = pl.BlockSpec((TOP_K, tm), lambda i: (0, i))
    return pl.pallas_call(
        functools.partial(_outproj_kernel, n_exp=n_exp),
        grid=(m // tm,),
        in_specs=[row,
                  pl.BlockSpec((d, d), lambda i: (0, 0)),
                  row, bspec,
                  pl.BlockSpec((1, d), lambda i: (0, 0)),
                  bspec, bspec,
                  pl.BlockSpec((d, 2 * LANE), lambda i: (0, 0)),
                  pl.BlockSpec((n_exp, 1), lambda i: (0, 0))],
        out_specs=[row, row,
                   pl.BlockSpec((tm * n_sub, LANE), lambda i: (i, 0)),
                   tok, tok,
                   pl.BlockSpec((n_exp, LANE), lambda i: (0, 0))],
        out_shape=[jax.ShapeDtypeStruct((m, d), F32),
                   jax.ShapeDtypeStruct((m, d), BF16),
                   jax.ShapeDtypeStruct((m * n_sub, LANE), F32),
                   jax.ShapeDtypeStruct((TOP_K, m), I32),
                   jax.ShapeDtypeStruct((TOP_K, m), F32),
                   jax.ShapeDtypeStruct((n_exp, LANE), I32)],
        scratch_shapes=[pltpu.VMEM((n_exp, LANE), F32)],
        compiler_params=_cparams("arbitrary"),
        name="outproj_route",
    )(y, wo, x2d, g1, norm_w.reshape(1, d), shift, scale, rw_pad, rb.reshape(n_exp, 1))


def _expert_kernel(be_ref, bwin_ref, brem_ref, bval_ref, nact_ref, cur_win, nxt_win,
                   h2p_hbm, wg_ref, wu_ref, wd_ref, ytk_hbm,
                   wg_s, wu_s, wd_s, xbuf, ybuf, sem_g, sem_s, sem_z, *, n_sub, n_tok, trash_base):
    i = pl.program_id(0)
    nact = nact_ref[0]
    p = lax.rem(i, 2)
    bm = EXPERT_BLOCK
    unroll = 8

    def token_of(f):
        return f & (n_tok - 1) if n_tok & (n_tok - 1) == 0 else lax.rem(f, n_tok)

    def gather_copy(tok, r, par):
        return pltpu.make_async_copy(h2p_hbm.at[pl.ds(pl.multiple_of(tok * n_sub, n_sub), n_sub), :],
                                     xbuf.at[par, pl.ds(pl.multiple_of(r * n_sub, n_sub), n_sub), :],
                                     sem_g.at[par])

    def scatter_copy(r, row, par):
        return pltpu.make_async_copy(ybuf.at[par, pl.ds(pl.multiple_of(r * n_sub, n_sub), n_sub), :],
                                     ytk_hbm.at[pl.ds(pl.multiple_of(row * n_sub, n_sub), n_sub), :],
                                     sem_s.at[par])

    def start_gathers(win_ref, blk, par):
        rem = brem_ref[blk]

        @pl.loop(0, bm // unroll)
        def _(q):
            for u in range(unroll):
                r = q * unroll + u
                gather_copy(token_of(win_ref[0, rem + r]), r, par).start(priority=u % 2)

    def wait_gathers(par):
        @pl.loop(0, bm // unroll)
        def _(q):
            for u in range(unroll):
                gather_copy(0, 0, par).wait()

    def start_scatters(par):
        rem = brem_ref[i]
        valid = bval_ref[i]
        trash = trash_base + par * bm

        @pl.loop(0, bm // unroll)
        def _(q):
            for u in range(unroll):
                r = q * unroll + u
                scatter_copy(r, jnp.where(r < valid, cur_win[0, rem + r], trash + r), par).start(priority=u % 2)

    def wait_scatters(par):
        @pl.loop(0, bm // unroll)
        def _(q):
            for u in range(unroll):
                scatter_copy(0, 0, par).wait()

    @pl.when(i == 0)
    def _():
        ybuf[0] = jnp.zeros(ybuf.shape[1:], F32)
        for par in range(2):
            start = (trash_base + par * bm) * n_sub
            cp = pltpu.make_async_copy(ybuf.at[0], ytk_hbm.at[pl.ds(start, bm * n_sub), :], sem_z)
            cp.start()
            cp.wait()
        start_gathers(cur_win, 0, 0)

    @pl.when(i < nact)
    def _():
        @pl.when(i + 1 < nact)
        def _():
            start_gathers(nxt_win, i + 1, 1 - p)

        wait_gathers(p)

        @pl.when(i >= 2)
        def _():
            wait_scatters(p)

        @pl.when((i == 0) | (be_ref[i] != be_ref[jnp.maximum(i - 1, 0)]))
        def _():
            wg_s[...] = wg_ref[...].astype(BF16)
            wu_s[...] = wu_ref[...].astype(BF16)
            wd_s[...] = wd_ref[...].astype(BF16)

        x = jnp.concatenate([xbuf[p, pl.ds(s, bm, stride=n_sub), :] for s in range(n_sub)], axis=1).astype(BF16)
        g = jnp.dot(x, wg_s[...], preferred_element_type=F32)
        u = jnp.dot(x, wu_s[...], preferred_element_type=F32)
        y = jnp.dot((_silu(g) * u).astype(BF16), wd_s[...], preferred_element_type=F32)
        for s in range(n_sub):
            ybuf[p, pl.ds(s, bm, stride=n_sub), :] = y[:, s * LANE:(s + 1) * LANE]
        start_scatters(p)

        @pl.when(i == nact - 1)
        def _():
            @pl.when(i >= 1)
            def _():
                wait_scatters(1 - p)

            wait_scatters(p)


def _experts(h2p, order, blk_expert, blk_off, blk_valid, n_active, exp_gate, exp_up, exp_down, n_blocks, n_tok):
    n_exp, d, ff = exp_gate.shape
    n_sub = d // LANE
    bm = EXPERT_BLOCK
    n_win = order.shape[0] // bm
    pieces = order.reshape(n_win, 1, bm)
    windows = jnp.concatenate([pieces, jnp.roll(pieces, -1, axis=0)], axis=-1)
    blk_win = blk_off // bm
    blk_rem = blk_off - blk_win * bm
    trash_base = TOP_K * n_tok

    def win(step_shift):
        def index_map(i, be, bwin, *_):
            return (bwin[jnp.minimum(i + step_shift, n_blocks - 1)], 0, 0)
        return pl.BlockSpec((None, 1, 2 * bm), index_map, memory_space=pltpu.SMEM)

    grid_spec = pltpu.PrefetchScalarGridSpec(
        num_scalar_prefetch=5,
        grid=(n_blocks,),
        in_specs=[win(0), win(1),
                  pl.BlockSpec(memory_space=pl.ANY),
                  pl.BlockSpec((None, d, ff), lambda i, be, *_: (be[i], 0, 0)),
                  pl.BlockSpec((None, d, ff), lambda i, be, *_: (be[i], 0, 0)),
                  pl.BlockSpec((None, ff, d), lambda i, be, *_: (be[i], 0, 0))],
        out_specs=pl.BlockSpec(memory_space=pl.ANY),
        scratch_shapes=[pltpu.VMEM((d, ff), BF16), pltpu.VMEM((d, ff), BF16), pltpu.VMEM((ff, d), BF16),
                        pltpu.VMEM((2, bm * n_sub, LANE), F32), pltpu.VMEM((2, bm * n_sub, LANE), F32),
                        pltpu.SemaphoreType.DMA((2,)), pltpu.SemaphoreType.DMA((2,)),
                        pltpu.SemaphoreType.DMA(())],
    )
    return pl.pallas_call(
        functools.partial(_expert_kernel, n_sub=n_sub, n_tok=n_tok, trash_base=trash_base),
        grid_spec=grid_spec,
        out_shape=jax.ShapeDtypeStruct(((trash_base + 2 * bm) * n_sub, LANE), F32),
        compiler_params=_cparams("arbitrary"),
        name="experts",
    )(blk_expert, blk_win, blk_rem, blk_valid, n_active, windows, windows, h2p, exp_gate, exp_up, exp_down)


def _combine_kernel(*refs, tmc, n_out):
    yk_refs = refs[:TOP_K]
    wt_ref, h2_ref, x1_ref, g2_ref, sg_ref, su_ref, sd_ref, o_ref = refs[TOP_K:]
    h = h2_ref[...]
    g = jnp.dot(h, sg_ref[...], preferred_element_type=F32)
    u = jnp.dot(h, su_ref[...], preferred_element_type=F32)
    routed = jnp.dot((_silu(g) * u).astype(BF16), sd_ref[...], preferred_element_type=F32)
    for k in range(TOP_K):
        rows = jnp.concatenate([yk_refs[k][pl.ds(s, tmc, stride=n_out), :] for s in range(n_out)], axis=1)
        routed = routed + wt_ref[:, k:k + 1] * rows
    o_ref[...] = x1_ref[...] + g2_ref[...] * routed


def _combine(ytk, wts_t, h2, x1, g2, sg, su, sd, seq, tmc):
    m, d = x1.shape
    ff = sg.shape[1]
    n_out = d // LANE
    per_b = seq // tmc
    tiles = m // tmc
    row = pl.BlockSpec((tmc, d), lambda i: (i, 0))
    yk_specs = [pl.BlockSpec((tmc * n_out, LANE), lambda i, k=k: (k * tiles + i, 0)) for k in range(TOP_K)]
    return pl.pallas_call(
        functools.partial(_combine_kernel, tmc=tmc, n_out=n_out),
        grid=(tiles,),
        in_specs=yk_specs + [pl.BlockSpec((tmc, TOP_K), lambda i: (i, 0)),
                             row, row,
                             pl.BlockSpec((None, 1, d), lambda i: (i // per_b, 0, 0)),
                             pl.BlockSpec((d, ff), lambda i: (0, 0)),
                             pl.BlockSpec((d, ff), lambda i: (0, 0)),
                             pl.BlockSpec((ff, d), lambda i: (0, 0))],
        out_specs=row,
        out_shape=jax.ShapeDtypeStruct((m, d), F32),
        compiler_params=_cparams("arbitrary"),
        name="combine",
    )(*([ytk] * TOP_K), wts_t, h2, x1, g2, sg, su, sd)


def kernel(x, c, ctx, c_ctx, mod_w, mod_b, norm1_w, w_in, q_norm_w, k_norm_w, w_fourier_out, w_attn_out, w_out,
           norm2_w, router_w, router_b, exp_gate, exp_up, exp_down, shared_gate, shared_up, shared_down):
    batch, seq, d = x.shape
    n_ctx = ctx.shape[1]
    assert mod_w.shape[0] == 1, "single-layer block only"
    assert batch + 1 <= 8 and seq % GRID_W == 0
    fw = w_fourier_out.shape[1]
    att = w_attn_out.shape[1]
    in_cols = w_in.shape[2]
    kvw = (in_cols - fw - att - 2 * d) // 2
    n_kv = kvw // HEAD_DIM
    n_exp = router_w.shape[2]
    assert att == n_kv * Q_GROUP * HEAD_DIM and n_exp <= LANE
    q_off, k_off = fw, fw + att
    v_off = k_off + kvw
    gf_off = v_off + kvw
    ga_off = gf_off + d
    m = batch * seq
    assert q_off % (Q_GROUP * HEAD_DIM) == 0 and kvw % HEAD_DIM == 0 and fw % F_GROUP_DIM == 0
    assert gf_off % min(512, d) == 0 and d % (2 * LANE) == 0

    cond = jnp.concatenate([c, c_ctx[None], jnp.zeros((8 - batch - 1, d), F32)], axis=0)
    mod = _adaln(cond, mod_w[0], mod_b[0])
    sh1, sc1, g1, sh2, sc2, g2 = [mod[:batch, j * d:(j + 1) * d].reshape(batch, 1, d) for j in range(6)]
    csh1 = jnp.broadcast_to(mod[batch, 0:d], (batch, 1, d))
    csc1 = jnp.broadcast_to(mod[batch, d:2 * d], (batch, 1, d))

    w_in_b = w_in[0].astype(BF16)
    cos_t, sin_t = _rope_tables(seq)
    tn = min(512, kvw)
    tm = min(1024, seq)
    modes = (("plain", 0, q_off // tn), ("q", q_off // tn, k_off // tn), ("k", k_off // tn, v_off // tn),
             ("plain", v_off // tn, gf_off // tn), ("sigmoid", gf_off // tn, in_cols // tn))
    proj = _inproj(x.reshape(m, d), sh1, sc1, norm1_w[0], w_in_b, q_norm_w[0], k_norm_w[0], cos_t, sin_t,
                   modes, seq, tm, tn)
    cmodes = (("k_norope", 0, kvw // tn), ("plain", kvw // tn, 2 * kvw // tn))
    tmc_ctx = min(n_ctx, 256)
    kvc = _inproj(ctx.reshape(batch * n_ctx, d), csh1, csc1, norm1_w[0], w_in_b[:, k_off:gf_off], q_norm_w[0],
                  k_norm_w[0], cos_t[:tmc_ctx], sin_t[:tmc_ctx], cmodes, n_ctx, tmc_ctx, tn)

    ya = _attention(proj, kvc, batch, seq, n_ctx, n_kv, q_off, k_off, v_off, tq=min(256, seq), tk=min(1024, seq))
    yf = _fourier_mix(proj, batch, seq, fw)
    y = _merge(yf, ya, w_fourier_out[0].astype(BF16), w_attn_out[0].astype(BF16), proj, gf_off, ga_off,
               tm=min(1024, seq), tn=min(512, d))

    rw_pad = jnp.pad(router_w[0], ((0, 0), (0, LANE - n_exp)))
    rw_hi = rw_pad.astype(BF16)
    rw_pad = jnp.concatenate([rw_hi, (rw_pad - rw_hi.astype(F32)).astype(BF16)], axis=1)
    x1, h2, h2p, eidx, wts, cnt = _outproj_route(
        y, w_out[0].astype(BF16), x.reshape(m, d), g1, norm2_w[0], sh2, sc2, rw_pad, router_b[0], seq, n_exp,
        tm=min(256, seq))

    n_assign = m * TOP_K
    assert n_assign % EXPERT_BLOCK == 0
    _, order = lax.sort((eidx.reshape(-1), jnp.arange(n_assign, dtype=I32)), num_keys=1)
    counts = cnt[:, 0]
    starts = jnp.cumsum(counts) - counts
    nb_e = (counts + EXPERT_BLOCK - 1) // EXPERT_BLOCK
    cum_nb = jnp.cumsum(nb_e)
    n_blocks = n_assign // EXPERT_BLOCK + n_exp
    n_active = jnp.maximum(cum_nb[-1], 1).astype(I32)
    blk = jnp.minimum(jnp.arange(n_blocks, dtype=I32), n_active - 1)
    blk_expert = jnp.minimum(jnp.sum(blk[:, None] >= cum_nb[None, :], axis=1), n_exp - 1).astype(I32)
    is_e = blk_expert[:, None] == jnp.arange(n_exp, dtype=I32)[None, :]
    lookup = lambda table: jnp.sum(jnp.where(is_e, table[None, :], 0), axis=1)
    blk_j = blk - lookup(cum_nb - nb_e)
    blk_off = (lookup(starts) + blk_j * EXPERT_BLOCK).astype(I32)
    blk_valid = jnp.clip(lookup(counts) - blk_j * EXPERT_BLOCK, 0, EXPERT_BLOCK).astype(I32)

    ytk = _experts(h2p, order, blk_expert, blk_off, blk_valid, n_active.reshape(1), exp_gate[0], exp_up[0],
                   exp_down[0], n_blocks, m)
    out = _combine(ytk, wts.T, h2, x1, g2, shared_gate[0].astype(BF16), shared_up[0].astype(BF16),
                   shared_down[0].astype(BF16), seq, tmc=min(128, seq))
    return out.reshape(batch, seq, d)
```

```python
import functools
import math

import jax
import jax.numpy as jnp
from jax import lax
from jax.experimental import pallas as pl
from jax.experimental.pallas import tpu as pltpu

F32 = jnp.float32
BF16 = jnp.bfloat16
I32 = jnp.int32

GRID_W = 64
HEAD_DIM = 128
Q_GROUP = 4
ROPE_FREQS = HEAD_DIM // 4
ROPE_THETA = 10000.0
F_GROUP_DIM = 128
N_EXPERT_GROUPS = 8
TOPK_GROUPS = 4
TOP_K = 8
ROUTED_SCALE = 2.5
EPS = 1e-6
LANE = 128
EXPERT_BLOCK = 256
VMEM_LIMIT = 56 * 1024 * 1024


def _cparams(*sem):
    return pltpu.CompilerParams(dimension_semantics=sem, vmem_limit_bytes=VMEM_LIMIT)


def _silu(v):
    return v * jax.nn.sigmoid(v)


def _adaln_kernel(c_ref, w_ref, b_ref, o_ref):
    a = _silu(c_ref[...])
    o_ref[...] = jnp.dot(a.astype(BF16), w_ref[...].astype(BF16), preferred_element_type=F32) + b_ref[...]


def _adaln(cond_pad, mod_w, mod_b):
    d, n = mod_w.shape
    tn = min(n, 1024)
    return pl.pallas_call(
        _adaln_kernel,
        grid=(n // tn,),
        in_specs=[pl.BlockSpec((8, d), lambda j: (0, 0)),
                  pl.BlockSpec((d, tn), lambda j: (0, j)),
                  pl.BlockSpec((1, tn), lambda j: (0, j))],
        out_specs=pl.BlockSpec((8, tn), lambda j: (0, j)),
        out_shape=jax.ShapeDtypeStruct((8, n), F32),
        compiler_params=_cparams("arbitrary"),
        name="adaln",
    )(cond_pad, mod_w, mod_b.reshape(1, n))


def _head_norm_rope(a, w, cos, sin, scale):
    ms = jnp.mean(a * a, axis=-1, keepdims=True)
    a = a * lax.rsqrt(ms + EPS) * w
    if cos is not None:
        lane = lax.broadcasted_iota(I32, a.shape, 1)
        first = (lane % (2 * ROPE_FREQS)) < ROPE_FREQS
        partner = jnp.where(first, pltpu.roll(a, HEAD_DIM - ROPE_FREQS, 1), pltpu.roll(a, ROPE_FREQS, 1))
        a = a * cos + partner * sin
    if scale is not None:
        a = a * scale
    return a


def _inproj_kernel(x_ref, sh_ref, sc_ref, nw_ref, w_ref, qw_ref, kw_ref, cos_ref, sin_ref, o_ref, h_scr,
                   *, modes, tn):
    j = pl.program_id(1)

    @pl.when(j == 0)
    def _():
        x = x_ref[...]
        ms = jnp.mean(x * x, axis=-1, keepdims=True)
        y = x * lax.rsqrt(ms + EPS) * nw_ref[...]
        h_scr[...] = (y * (1.0 + sc_ref[...]) + sh_ref[...]).astype(BF16)

    def tile(c0, c1):
        return jnp.dot(h_scr[...], w_ref[:, c0:c1], preferred_element_type=F32)

    piece = 2 * HEAD_DIM if tn % (2 * HEAD_DIM) == 0 else HEAD_DIM
    for mode, j0, j1 in modes:
        @pl.when((j >= j0) & (j < j1))
        def _(mode=mode):
            if mode == "plain":
                o_ref[...] = tile(0, tn).astype(BF16)
            elif mode == "sigmoid":
                for c0 in range(0, tn, piece):
                    o_ref[:, c0:c0 + piece] = jax.nn.sigmoid(tile(c0, c0 + piece)).astype(BF16)
            else:
                acc = tile(0, tn)
                rope = mode in ("q", "k")
                w = qw_ref[...] if mode == "q" else kw_ref[...]
                scale = HEAD_DIM ** -0.5 * math.log2(math.e) if mode == "q" else None
                for h in range(tn // HEAD_DIM):
                    sl = slice(h * HEAD_DIM, (h + 1) * HEAD_DIM)
                    a = _head_norm_rope(acc[:, sl], w, cos_ref[...] if rope else None,
                                        sin_ref[...] if rope else None, scale)
                    o_ref[:, sl] = a.astype(BF16)


def _inproj(x2d, shift, scale, norm_w, w_bf16, q_norm_w, k_norm_w, cos_t, sin_t, modes, seq, tm, tn):
    m, d = x2d.shape
    n = w_bf16.shape[1]
    per_b = seq // tm
    pos_blocks = cos_t.shape[0] // tm
    kern = functools.partial(_inproj_kernel, modes=modes, tn=tn)
    return pl.pallas_call(
        kern,
        grid=(m // tm, n // tn),
        in_specs=[pl.BlockSpec((tm, d), lambda i, j: (i, 0)),
                  pl.BlockSpec((None, 1, d), lambda i, j: (i // per_b, 0, 0)),
                  pl.BlockSpec((None, 1, d), lambda i, j: (i // per_b, 0, 0)),
                  pl.BlockSpec((1, d), lambda i, j: (0, 0)),
                  pl.BlockSpec((d, tn), lambda i, j: (0, j)),
                  pl.BlockSpec((1, HEAD_DIM), lambda i, j: (0, 0)),
                  pl.BlockSpec((1, HEAD_DIM), lambda i, j: (0, 0)),
                  pl.BlockSpec((tm, HEAD_DIM), lambda i, j: (i % pos_blocks, 0)),
                  pl.BlockSpec((tm, HEAD_DIM), lambda i, j: (i % pos_blocks, 0))],
        out_specs=pl.BlockSpec((tm, tn), lambda i, j: (i, j)),
        out_shape=jax.ShapeDtypeStruct((m, n), BF16),
        scratch_shapes=[pltpu.VMEM((tm, d), BF16)],
        compiler_params=_cparams("arbitrary", "arbitrary"),
        name="inproj",
    )(x2d, shift, scale, norm_w.reshape(1, d), w_bf16, q_norm_w.reshape(1, HEAD_DIM),
      k_norm_w.reshape(1, HEAD_DIM), cos_t, sin_t)


def _rope_tables(seq):
    rows = seq // GRID_W
    row = jnp.repeat(jnp.arange(rows), GRID_W)
    col = jnp.tile(jnp.arange(GRID_W), rows)
    pos = jnp.stack([row, col], axis=-1).astype(F32)
    inv_freq = ROPE_THETA ** (-jnp.arange(ROPE_FREQS, dtype=F32) / ROPE_FREQS)
    ang = pos[:, :, None] * inv_freq
    cos, sin = jnp.cos(ang), jnp.sin(ang)
    cos_t = jnp.concatenate([cos[:, 0], cos[:, 0], cos[:, 1], cos[:, 1]], axis=-1)
    sin_t = jnp.concatenate([-sin[:, 0], sin[:, 0], -sin[:, 1], sin[:, 1]], axis=-1)
    return cos_t, sin_t


def _attn_kernel(q_ref, k_ref, v_ref, kc_ref, vc_ref, o_ref, vt_scr, vct_scr, *, tq, tk, n_chunks):
    @pl.when(pl.program_id(2) == 0)
    def _():
        for c in range(n_chunks):
            vt_scr[c] = jnp.transpose(v_ref[c * tk:(c + 1) * tk, :].astype(F32)).astype(BF16)
        vct_scr[...] = jnp.transpose(vc_ref[...].astype(F32)).astype(BF16)

    q = jnp.concatenate([q_ref[:, h * HEAD_DIM:(h + 1) * HEAD_DIM] for h in range(Q_GROUP)], axis=0)
    qt = jnp.transpose(q.astype(F32)).astype(BF16)
    m_cols = Q_GROUP * tq

    def scores(kb):
        return jnp.dot(kb, qt, preferred_element_type=F32)

    def update(st, vtb, carry):
        m, l, acc = carry
        m_new = jnp.maximum(m, jnp.max(st, axis=0, keepdims=True))
        alpha = jnp.exp2(m - m_new)
        pt = jnp.exp2(st - m_new)
        l = alpha * l + jnp.sum(pt, axis=0, keepdims=True)
        acc = alpha * acc + jnp.dot(vtb, pt.astype(BF16), preferred_element_type=F32)
        return m_new, l, acc

    carry = (jnp.full((1, m_cols), -jnp.inf, F32), jnp.zeros((1, m_cols), F32),
             jnp.zeros((HEAD_DIM, m_cols), F32))
    st = scores(k_ref[0:tk, :])
    for c in range(n_chunks):
        st_next = scores(k_ref[(c + 1) * tk:(c + 2) * tk, :]) if c + 1 < n_chunks else scores(kc_ref[...])
        carry = update(st, vt_scr[c], carry)
        st = st_next
    _, l, acc = update(st, vct_scr[...], carry)
    o = jnp.transpose(acc / l)
    for h in range(Q_GROUP):
        o_ref[:, h * HEAD_DIM:(h + 1) * HEAD_DIM] = o[h * tq:(h + 1) * tq].astype(BF16)


def _attention(proj, kvc, batch, seq, n_ctx, n_kv, q_off, k_off, v_off, tq, tk):
    gw = Q_GROUP * HEAD_DIM
    per_b = seq // tq
    n_chunks = seq // tk
    kern = functools.partial(_attn_kernel, tq=tq, tk=tk, n_chunks=n_chunks)
    return pl.pallas_call(
        kern,
        grid=(batch, n_kv, per_b),
        in_specs=[pl.BlockSpec((tq, gw), lambda b, g, i: (b * per_b + i, q_off // gw + g)),
                  pl.BlockSpec((seq, HEAD_DIM), lambda b, g, i: (b, k_off // HEAD_DIM + g)),
                  pl.BlockSpec((seq, HEAD_DIM), lambda b, g, i: (b, v_off // HEAD_DIM + g)),
                  pl.BlockSpec((n_ctx, HEAD_DIM), lambda b, g, i: (b, g)),
                  pl.BlockSpec((n_ctx, HEAD_DIM), lambda b, g, i: (b, n_kv + g))],
        out_specs=pl.BlockSpec((tq, gw), lambda b, g, i: (b * per_b + i, g)),
        out_shape=jax.ShapeDtypeStruct((batch * seq, n_kv * gw), BF16),
        scratch_shapes=[pltpu.VMEM((n_chunks, HEAD_DIM, tk), BF16), pltpu.VMEM((HEAD_DIM, n_ctx), BF16)],
        compiler_params=_cparams("arbitrary", "arbitrary", "arbitrary"),
        name="attention",
    )(proj, proj, proj, kvc, kvc)


def _dft_mats(n_rows, n_cols):
    def cs(k, n):
        ang = (2.0 * math.pi / n) * (k % n).astype(F32)
        return jnp.cos(ang), jnp.sin(ang)

    ch = jnp.arange(F_GROUP_DIM)
    cc, sc = cs(ch[:, None] * ch[None, :], F_GROUP_DIM)
    mat_a = jnp.concatenate([cc, -sc], axis=1).astype(BF16)
    r = jnp.arange(n_rows)
    cr, sr = cs(r[:, None] * r[None, :], n_rows)
    mat_1 = jnp.concatenate([jnp.concatenate([cr, sr], axis=1),
                             jnp.concatenate([-sr, cr], axis=1)], axis=0).astype(BF16)
    c = jnp.arange(n_cols)
    c2, s2 = cs(c[:, None] * c[None, :], n_cols)
    mat_2 = jnp.concatenate([c2, s2], axis=1).astype(BF16)
    tr, ts = cs(r[:, None] * c[None, :], n_rows * n_cols)
    tw_r = jnp.repeat(tr, LANE, axis=1)
    tw_i = jnp.repeat(-ts, LANE, axis=1)
    return mat_a, mat_1, mat_2, tw_r, tw_i


def _fourier_a_kernel(u_ref, m_ref, vr_ref, vi_ref, *, groups):
    for g in range(groups):
        sl = slice(g * F_GROUP_DIM, (g + 1) * F_GROUP_DIM)
        r = jnp.dot(u_ref[:, sl], m_ref[...], preferred_element_type=F32)
        vr_ref[:, sl] = r[:, :F_GROUP_DIM].astype(BF16)
        vi_ref[:, sl] = r[:, F_GROUP_DIM:].astype(BF16)


def _fourier_1_kernel(ar_ref, ai_ref, m_ref, twr_ref, twi_ref, pr_ref, pi_ref, *, nseg, fw, n_rows):
    rhs = jnp.concatenate([ar_ref[...], ai_ref[...]], axis=0)
    z = jnp.dot(m_ref[...], rhs, preferred_element_type=F32)
    reps = fw // LANE
    for s in range(nseg):
        zr = z[:n_rows, s * fw:(s + 1) * fw]
        zi = z[n_rows:, s * fw:(s + 1) * fw]
        tr = jnp.tile(twr_ref[:, s * LANE:(s + 1) * LANE], (1, reps))
        ti = jnp.tile(twi_ref[:, s * LANE:(s + 1) * LANE], (1, reps))
        pr_ref[s] = (zr * tr - zi * ti).astype(BF16)
        pi_ref[s] = (zr * ti + zi * tr).astype(BF16)


def _fourier_2_kernel(ar_ref, ai_ref, m_ref, o_ref, *, norm):
    rhs = jnp.concatenate([ar_ref[...], ai_ref[...]], axis=0)
    o_ref[...] = (jnp.dot(m_ref[...], rhs, preferred_element_type=F32) * norm).astype(BF16)


def _fourier_mix(proj, batch, seq, fw):
    n_cols = GRID_W
    n_rows = seq // GRID_W
    groups = fw // F_GROUP_DIM
    mat_a, mat_1, mat_2, tw_r, tw_i = _dft_mats(n_rows, n_cols)
    m = batch * seq
    tm = min(seq, 1024)
    vr, vi = pl.pallas_call(
        functools.partial(_fourier_a_kernel, groups=groups),
        grid=(m // tm,),
        in_specs=[pl.BlockSpec((tm, fw), lambda i: (i, 0)),
                  pl.BlockSpec((F_GROUP_DIM, 2 * F_GROUP_DIM), lambda i: (0, 0))],
        out_specs=[pl.BlockSpec((tm, fw), lambda i: (i, 0))] * 2,
        out_shape=[jax.ShapeDtypeStruct((m, fw), BF16)] * 2,
        compiler_params=_cparams("arbitrary"),
        name="fourier_channels",
    )(proj, mat_a)
    wide = n_cols * fw
    vr = vr.reshape(batch * n_rows, wide)
    vi = vi.reshape(batch * n_rows, wide)
    nseg = min(n_cols, 4)
    pr, pi = pl.pallas_call(
        functools.partial(_fourier_1_kernel, nseg=nseg, fw=fw, n_rows=n_rows),
        grid=(batch, n_cols // nseg),
        in_specs=[pl.BlockSpec((n_rows, nseg * fw), lambda b, j: (b, j)),
                  pl.BlockSpec((n_rows, nseg * fw), lambda b, j: (b, j)),
                  pl.BlockSpec((2 * n_rows, 2 * n_rows), lambda b, j: (0, 0)),
                  pl.BlockSpec((n_rows, nseg * LANE), lambda b, j: (0, j)),
                  pl.BlockSpec((n_rows, nseg * LANE), lambda b, j: (0, j))],
        out_specs=[pl.BlockSpec((nseg, n_rows, fw), lambda b, j: (b * (n_cols // nseg) + j, 0, 0))] * 2,
        out_shape=[jax.ShapeDtypeStruct((batch * n_cols, n_rows, fw), BF16)] * 2,
        compiler_params=_cparams("arbitrary", "arbitrary"),
        name="fourier_rows",
    )(vr, vi, mat_1, tw_r, tw_i)
    wide2 = n_rows * fw
    pr = pr.reshape(batch * n_cols, wide2)
    pi = pi.reshape(batch * n_cols, wide2)
    tc = min(wide2, 8192)
    norm = 1.0 / math.sqrt(seq * F_GROUP_DIM)
    yf = pl.pallas_call(
        functools.partial(_fourier_2_kernel, norm=norm),
        grid=(batch, wide2 // tc),
        in_specs=[pl.BlockSpec((n_cols, tc), lambda b, j: (b, j)),
                  pl.BlockSpec((n_cols, tc), lambda b, j: (b, j)),
                  pl.BlockSpec((n_cols, 2 * n_cols), lambda b, j: (0, 0))],
        out_specs=pl.BlockSpec((n_cols, tc), lambda b, j: (b, j)),
        out_shape=jax.ShapeDtypeStruct((batch * n_cols, wide2), BF16),
        compiler_params=_cparams("arbitrary", "arbitrary"),
        name="fourier_cols",
    )(pr, pi, mat_2)
    return yf.reshape(m, fw)


def _merge_kernel(yf_ref, ya_ref, wfo_ref, wao_ref, gf_ref, ga_ref, o_ref):
    a = jnp.dot(yf_ref[...], wfo_ref[...], preferred_element_type=F32)
    b = jnp.dot(ya_ref[...], wao_ref[...], preferred_element_type=F32)
    o_ref[...] = (gf_ref[...].astype(F32) * a + ga_ref[...].astype(F32) * b).astype(BF16)


def _merge(yf, ya, wfo, wao, proj, gf_off, ga_off, tm, tn):
    m, fw = yf.shape
    aw = ya.shape[1]
    d = wfo.shape[1]
    return pl.pallas_call(
        _merge_kernel,
        grid=(m // tm, d // tn),
        in_specs=[pl.BlockSpec((tm, fw), lambda i, j: (i, 0)),
                  pl.BlockSpec((tm, aw), lambda i, j: (i, 0)),
                  pl.BlockSpec((fw, tn), lambda i, j: (0, j)),
                  pl.BlockSpec((aw, tn), lambda i, j: (0, j)),
                  pl.BlockSpec((tm, tn), lambda i, j: (i, gf_off // tn + j)),
                  pl.BlockSpec((tm, tn), lambda i, j: (i, ga_off // tn + j))],
        out_specs=pl.BlockSpec((tm, tn), lambda i, j: (i, j)),
        out_shape=jax.ShapeDtypeStruct((m, d), BF16),
        compiler_params=_cparams("arbitrary", "arbitrary"),
        name="merge",
    )(yf, ya, wfo, wao, proj, proj)


def _route(logits_t, bias, n_exp):
    gsz = n_exp // N_EXPERT_GROUPS
    tm = logits_t.shape[1]
    neg = -jnp.inf
    scores = jax.nn.sigmoid(logits_t)
    biased = scores + bias
    io_g = lax.broadcasted_iota(I32, (gsz, tm), 0).astype(F32)
    gs = []
    for g in range(N_EXPERT_GROUPS):
        grp = biased[g * gsz:(g + 1) * gsz, :]
        m1 = jnp.max(grp, axis=0, keepdims=True)
        i1 = jnp.min(jnp.where(grp == m1, io_g, float(gsz)), axis=0, keepdims=True)
        m2 = jnp.max(jnp.where(io_g == i1, neg, grp), axis=0, keepdims=True)
        gs.append(m1 + m2)
    cur = jnp.concatenate(gs, axis=0)
    io_n = lax.broadcasted_iota(I32, (N_EXPERT_GROUPS, tm), 0).astype(F32)
    sel = jnp.zeros((N_EXPERT_GROUPS, tm), F32)
    for _ in range(TOPK_GROUPS):
        mx = jnp.max(cur, axis=0, keepdims=True)
        ix = jnp.min(jnp.where(cur == mx, io_n, float(N_EXPERT_GROUPS)), axis=0, keepdims=True)
        hit = io_n == ix
        sel = jnp.where(hit, 1.0, sel)
        cur = jnp.where(hit, neg, cur)
    cur = jnp.concatenate([jnp.where(sel[g:g + 1, :] > 0.0, biased[g * gsz:(g + 1) * gsz, :], neg)
                           for g in range(N_EXPERT_GROUPS)], axis=0)
    io_e = lax.broadcasted_iota(I32, (n_exp, tm), 0).astype(F32)
    eidx, wts = [], []
    for _ in range(TOP_K):
        mx = jnp.max(cur, axis=0, keepdims=True)
        ix = jnp.min(jnp.where(cur == mx, io_e, float(n_exp)), axis=0, keepdims=True)
        hit = io_e == ix
        wts.append(jnp.sum(jnp.where(hit, scores, 0.0), axis=0, keepdims=True))
        eidx.append(ix)
        cur = jnp.where(hit, neg, cur)
    eidx = jnp.concatenate(eidx, axis=0).astype(I32)
    wts = jnp.concatenate(wts, axis=0)
    wts = wts / jnp.sum(wts, axis=0, keepdims=True) * ROUTED_SCALE
    return eidx, wts


def _outproj_kernel(y_ref, wo_ref, x_ref, g1_ref, nw_ref, sh_ref, sc_ref, rw_ref, rb_ref,
                    x1_ref, h2_ref, h2p_ref, eidx_ref, wts_ref, cnt_ref, carry_scr, *, n_exp):
    i = pl.program_id(0)
    tm, d = x_ref.shape

    @pl.when(i == 0)
    def _():
        carry_scr[...] = jnp.zeros_like(carry_scr)

    x1 = x_ref[...] + g1_ref[...] * jnp.dot(y_ref[...], wo_ref[...], preferred_element_type=F32)
    x1_ref[...] = x1
    ms = jnp.mean(x1 * x1, axis=-1, keepdims=True)
    h2 = (x1 * lax.rsqrt(ms + EPS) * nw_ref[...]) * (1.0 + sc_ref[...]) + sh_ref[...]
    h2b = h2.astype(BF16)
    h2_ref[...] = h2b
    n_sub = d // LANE
    for s in range(n_sub):
        h2p_ref[pl.ds(s, tm, stride=n_sub), :] = h2[:, s * LANE:(s + 1) * LANE]
    h_hi = h2b
    h_lo = (h2 - h_hi.astype(F32)).astype(BF16)
    part = (jnp.dot(h_hi, rw_ref[...], preferred_element_type=F32)
            + jnp.dot(h_lo, rw_ref[...], preferred_element_type=F32))
    logits = part[:, :LANE] + part[:, LANE:]
    logits_t = jnp.transpose(logits)[:n_exp, :]
    eidx, wts = _route(logits_t, rb_ref[...], n_exp)
    eidx_ref[...] = eidx
    wts_ref[...] = wts
    io_e = lax.broadcasted_iota(I32, (n_exp, tm), 0)
    onehot = jnp.zeros((n_exp, tm), F32)
    for k in range(TOP_K):
        onehot = onehot + jnp.where(io_e == eidx[k:k + 1, :], 1.0, 0.0)
    carry_scr[...] = carry_scr[...] + jnp.sum(onehot, axis=1, keepdims=True)
    cnt_ref[...] = carry_scr[...].astype(I32)


def _outproj_route(y, wo, x2d, g1, norm_w, shift, scale, rw_pad, rb, seq, n_exp, tm):
    m, d = x2d.shape
    per_b = seq // tm
    n_sub = d // LANE
    bspec = pl.BlockSpec((None, 1, d), lambda i: (i // per_b, 0, 0))
    row = pl.BlockSpec((tm, d), lambda i: (i, 0))
    tok = pl.BlockSpec((TOP_K, tm), lambda i: (0, i))
    return pl.pallas_call(
        functools.partial(_outproj_kernel, n_exp=n_exp),
        grid=(m // tm,),
        in_specs=[row,
                  pl.BlockSpec((d, d), lambda i: (0, 0)),
                  row, bspec,
                  pl.BlockSpec((1, d), lambda i: (0, 0)),
                  bspec, bspec,
                  pl.BlockSpec((d, 2 * LANE), lambda i: (0, 0)),
                  pl.BlockSpec((n_exp, 1), lambda i: (0, 0))],
        out_specs=[row, row,
                   pl.BlockSpec((tm * n_sub, LANE), lambda i: (i, 0)),
                   tok, tok,
                   pl.BlockSpec((n_exp, LANE), lambda i: (0, 0))],
        out_shape=[jax.ShapeDtypeStruct((m, d), F32),
                   jax.ShapeDtypeStruct((m, d), BF16),
                   jax.ShapeDtypeStruct((m * n_sub, LANE), F32),
                   jax.ShapeDtypeStruct((TOP_K, m), I32),
                   jax.ShapeDtypeStruct((TOP_K, m), F32),
                   jax.ShapeDtypeStruct((n_exp, LANE), I32)],
        scratch_shapes=[pltpu.VMEM((n_exp, LANE), F32)],
        compiler_params=_cparams("arbitrary"),
        name="outproj_route",
    )(y, wo, x2d, g1, norm_w.reshape(1, d), shift, scale, rw_pad, rb.reshape(n_exp, 1))


def _expert_kernel(be_ref, bwin_ref, brem_ref, bval_ref, nact_ref, prev_win, cur_win, nxt_win,
                   h2p_hbm, wg_ref, wu_ref, wd_ref, ytk_hbm,
                   wg_s, wu_s, wd_s, xbuf0, xbuf1, ybuf0, ybuf1, sem_g, sem_s, sem_z,
                   *, n_sub, n_tok, n_blocks, trash_base):
    i = pl.program_id(0)
    nact = nact_ref[0]
    bm = EXPERT_BLOCK
    xbufs = (xbuf0, xbuf1)
    ybufs = (ybuf0, ybuf1)
    unroll = 8

    def token_of(f):
        return f & (n_tok - 1) if n_tok & (n_tok - 1) == 0 else lax.rem(f, n_tok)

    def gather_copy(tok, r, par):
        return pltpu.make_async_copy(h2p_hbm.at[pl.ds(pl.multiple_of(tok * n_sub, n_sub), n_sub), :],
                                     xbufs[par].at[pl.ds(pl.multiple_of(r * n_sub, n_sub), n_sub), :],
                                     sem_g.at[par])

    def scatter_copy(r, row, par):
        return pltpu.make_async_copy(ybufs[par].at[pl.ds(pl.multiple_of(r * n_sub, n_sub), n_sub), :],
                                     ytk_hbm.at[pl.ds(pl.multiple_of(row * n_sub, n_sub), n_sub), :],
                                     sem_s.at[par])

    def wait_gathers(par):
        @pl.loop(0, bm // unroll)
        def _(q):
            for u in range(unroll):
                gather_copy(0, 0, par).wait()

    def wait_scatters(par):
        @pl.loop(0, bm // unroll)
        def _(q):
            for u in range(unroll):
                scatter_copy(0, 0, par).wait()

    def scatter_row(win_ref, rem, valid, r, par):
        row = jnp.where(r < valid, win_ref[0, rem + r], trash_base + par * bm + r)
        return scatter_copy(r, row, par)

    @pl.when(i == 0)
    def _():
        ybuf0[...] = jnp.zeros_like(ybuf0)
        ybuf1[...] = jnp.zeros_like(ybuf1)
        for par in range(2):
            start = (trash_base + par * bm) * n_sub
            cp = pltpu.make_async_copy(ybuf0, ytk_hbm.at[pl.ds(start, bm * n_sub), :], sem_z)
            cp.start()
            cp.wait()
        rem0 = brem_ref[0]

        @pl.loop(0, bm // unroll)
        def _(q):
            for u in range(unroll):
                r = q * unroll + u
                gather_copy(token_of(cur_win[0, rem0 + r]), r, 0).start(priority=u % 2)

    def block_step(par):
        wait_gathers(par)

        @pl.when(i >= 1)
        def _():
            wait_scatters(par)

        @pl.when((i == 0) | (be_ref[i] != be_ref[jnp.maximum(i - 1, 0)]))
        def _():
            wg_s[...] = wg_ref[...].astype(BF16)
            wu_s[...] = wu_ref[...].astype(BF16)
            wd_s[...] = wd_ref[...].astype(BF16)

        nxt_rem = brem_ref[jnp.minimum(i + 1, n_blocks - 1)]
        prv = jnp.maximum(i - 1, 0)
        prv_rem = brem_ref[prv]
        prv_valid = jnp.where(i >= 1, bval_ref[prv], 0)
        for r in range(bm):
            gather_copy(token_of(nxt_win[0, nxt_rem + r]), r, 1 - par).start(priority=r % 2)
        for r in range(bm):
            scatter_row(prev_win, prv_rem, prv_valid, r, 1 - par).start(priority=r % 2)

        x = jnp.concatenate([xbufs[par][pl.ds(s, bm, stride=n_sub), :] for s in range(n_sub)],
                            axis=1).astype(BF16)
        g = jnp.dot(x, wg_s[...], preferred_element_type=F32)
        u = jnp.dot(x, wu_s[...], preferred_element_type=F32)
        y = jnp.dot((_silu(g) * u).astype(BF16), wd_s[...], preferred_element_type=F32)
        for s in range(n_sub):
            ybufs[par][pl.ds(s, bm, stride=n_sub), :] = y[:, s * LANE:(s + 1) * LANE]

        @pl.when(i == nact - 1)
        def _():
            wait_gathers(1 - par)
            wait_scatters(1 - par)
            rem = brem_ref[i]
            valid = bval_ref[i]

            @pl.loop(0, bm // unroll)
            def _(q):
                for u_ in range(unroll):
                    scatter_row(cur_win, rem, valid, q * unroll + u_, par).start(priority=u_ % 2)

            wait_scatters(par)

    for par in range(2):
        @pl.when((i < nact) & (lax.rem(i, 2) == par))
        def _(par=par):
            block_step(par)


def _experts(h2p, order, blk_expert, blk_off, blk_valid, n_active, exp_gate, exp_up, exp_down, n_blocks, n_tok):
    n_exp, d, ff = exp_gate.shape
    n_sub = d // LANE
    bm = EXPERT_BLOCK
    n_win = order.shape[0] // bm
    pieces = order.reshape(n_win, 1, bm)
    windows = jnp.concatenate([pieces, jnp.roll(pieces, -1, axis=0)], axis=-1)
    blk_win = blk_off // bm
    blk_rem = blk_off - blk_win * bm
    trash_base = TOP_K * n_tok

    def win(step_shift):
        def index_map(i, be, bwin, *_):
            return (bwin[jnp.clip(i + step_shift, 0, n_blocks - 1)], 0, 0)
        return pl.BlockSpec((None, 1, 2 * bm), index_map, memory_space=pltpu.SMEM)

    row_buf = pltpu.VMEM((bm * n_sub, LANE), F32)
    grid_spec = pltpu.PrefetchScalarGridSpec(
        num_scalar_prefetch=5,
        grid=(n_blocks,),
        in_specs=[win(-1), win(0), win(1),
                  pl.BlockSpec(memory_space=pl.ANY),
                  pl.BlockSpec((None, d, ff), lambda i, be, *_: (be[i], 0, 0)),
                  pl.BlockSpec((None, d, ff), lambda i, be, *_: (be[i], 0, 0)),
                  pl.BlockSpec((None, ff, d), lambda i, be, *_: (be[i], 0, 0))],
        out_specs=pl.BlockSpec(memory_space=pl.ANY),
        scratch_shapes=[pltpu.VMEM((d, ff), BF16), pltpu.VMEM((d, ff), BF16), pltpu.VMEM((ff, d), BF16),
                        row_buf, row_buf, row_buf, row_buf,
                        pltpu.SemaphoreType.DMA((2,)), pltpu.SemaphoreType.DMA((2,)),
                        pltpu.SemaphoreType.DMA(())],
    )
    return pl.pallas_call(
        functools.partial(_expert_kernel, n_sub=n_sub, n_tok=n_tok, n_blocks=n_blocks, trash_base=trash_base),
        grid_spec=grid_spec,
        out_shape=jax.ShapeDtypeStruct(((trash_base + 2 * bm) * n_sub, LANE), F32),
        compiler_params=_cparams("arbitrary"),
        name="experts",
    )(blk_expert, blk_win, blk_rem, blk_valid, n_active, windows, windows, windows, h2p, exp_gate, exp_up,
      exp_down)


def _combine_kernel(*refs, tmc, n_out):
    yk_refs = refs[:TOP_K]
    wt_ref, h2_ref, x1_ref, g2_ref, sg_ref, su_ref, sd_ref, o_ref = refs[TOP_K:]
    h = h2_ref[...]
    g = jnp.dot(h, sg_ref[...], preferred_element_type=F32)
    u = jnp.dot(h, su_ref[...], preferred_element_type=F32)
    routed = jnp.dot((_silu(g) * u).astype(BF16), sd_ref[...], preferred_element_type=F32)
    for k in range(TOP_K):
        rows = jnp.concatenate([yk_refs[k][pl.ds(s, tmc, stride=n_out), :] for s in range(n_out)], axis=1)
        routed = routed + wt_ref[:, k:k + 1] * rows
    o_ref[...] = x1_ref[...] + g2_ref[...] * routed


def _combine(ytk, wts_t, h2, x1, g2, sg, su, sd, seq, tmc):
    m, d = x1.shape
    ff = sg.shape[1]
    n_out = d // LANE
    per_b = seq // tmc
    tiles = m // tmc
    row = pl.BlockSpec((tmc, d), lambda i: (i, 0))
    yk_specs = [pl.BlockSpec((tmc * n_out, LANE), lambda i, k=k: (k * tiles + i, 0)) for k in range(TOP_K)]
    return pl.pallas_call(
        functools.partial(_combine_kernel, tmc=tmc, n_out=n_out),
        grid=(tiles,),
        in_specs=yk_specs + [pl.BlockSpec((tmc, TOP_K), lambda i: (i, 0)),
                             row, row,
                             pl.BlockSpec((None, 1, d), lambda i: (i // per_b, 0, 0)),
                             pl.BlockSpec((d, ff), lambda i: (0, 0)),
                             pl.BlockSpec((d, ff), lambda i: (0, 0)),
                             pl.BlockSpec((ff, d), lambda i: (0, 0))],
        out_specs=row,
        out_shape=jax.ShapeDtypeStruct((m, d), F32),
        compiler_params=_cparams("arbitrary"),
        name="combine",
    )(*([ytk] * TOP_K), wts_t, h2, x1, g2, sg, su, sd)


def kernel(x, c, ctx, c_ctx, mod_w, mod_b, norm1_w, w_in, q_norm_w, k_norm_w, w_fourier_out, w_attn_out, w_out,
           norm2_w, router_w, router_b, exp_gate, exp_up, exp_down, shared_gate, shared_up, shared_down):
    batch, seq, d = x.shape
    n_ctx = ctx.shape[1]
    assert mod_w.shape[0] == 1, "single-layer block only"
    assert batch + 1 <= 8 and seq % GRID_W == 0
    fw = w_fourier_out.shape[1]
    att = w_attn_out.shape[1]
    in_cols = w_in.shape[2]
    kvw = (in_cols - fw - att - 2 * d) // 2
    n_kv = kvw // HEAD_DIM
    n_exp = router_w.shape[2]
    assert att == n_kv * Q_GROUP * HEAD_DIM and n_exp <= LANE
    q_off, k_off = fw, fw + att
    v_off = k_off + kvw
    gf_off = v_off + kvw
    ga_off = gf_off + d
    m = batch * seq
    assert q_off % (Q_GROUP * HEAD_DIM) == 0 and kvw % HEAD_DIM == 0 and fw % F_GROUP_DIM == 0
    assert gf_off % min(512, d) == 0 and d % (2 * LANE) == 0

    cond = jnp.concatenate([c, c_ctx[None], jnp.zeros((8 - batch - 1, d), F32)], axis=0)
    mod = _adaln(cond, mod_w[0], mod_b[0])
    sh1, sc1, g1, sh2, sc2, g2 = [mod[:batch, j * d:(j + 1) * d].reshape(batch, 1, d) for j in range(6)]
    csh1 = jnp.broadcast_to(mod[batch, 0:d], (batch, 1, d))
    csc1 = jnp.broadcast_to(mod[batch, d:2 * d], (batch, 1, d))

    w_in_b = w_in[0].astype(BF16)
    cos_t, sin_t = _rope_tables(seq)
    tn = min(512, kvw)
    tm = min(1024, seq)
    modes = (("plain", 0, q_off // tn), ("q", q_off // tn, k_off // tn), ("k", k_off // tn, v_off // tn),
             ("plain", v_off // tn, gf_off // tn), ("sigmoid", gf_off // tn, in_cols // tn))
    proj = _inproj(x.reshape(m, d), sh1, sc1, norm1_w[0], w_in_b, q_norm_w[0], k_norm_w[0], cos_t, sin_t,
                   modes, seq, tm, tn)
    cmodes = (("k_norope", 0, kvw // tn), ("plain", kvw // tn, 2 * kvw // tn))
    tmc_ctx = min(n_ctx, 256)
    kvc = _inproj(ctx.reshape(batch * n_ctx, d), csh1, csc1, norm1_w[0], w_in_b[:, k_off:gf_off], q_norm_w[0],
                  k_norm_w[0], cos_t[:tmc_ctx], sin_t[:tmc_ctx], cmodes, n_ctx, tmc_ctx, tn)

    ya = _attention(proj, kvc, batch, seq, n_ctx, n_kv, q_off, k_off, v_off, tq=min(256, seq), tk=min(1024, seq))
    yf = _fourier_mix(proj, batch, seq, fw)
    y = _merge(yf, ya, w_fourier_out[0].astype(BF16), w_attn_out[0].astype(BF16), proj, gf_off, ga_off,
               tm=min(1024, seq), tn=min(512, d))

    rw_pad = jnp.pad(router_w[0], ((0, 0), (0, LANE - n_exp)))
    rw_hi = rw_pad.astype(BF16)
    rw_pad = jnp.concatenate([rw_hi, (rw_pad - rw_hi.astype(F32)).astype(BF16)], axis=1)
    x1, h2, h2p, eidx, wts, cnt = _outproj_route(
        y, w_out[0].astype(BF16), x.reshape(m, d), g1, norm2_w[0], sh2, sc2, rw_pad, router_b[0], seq, n_exp,
        tm=min(256, seq))

    n_assign = m * TOP_K
    assert n_assign % EXPERT_BLOCK == 0
    _, order = lax.sort((eidx.reshape(-1), jnp.arange(n_assign, dtype=I32)), num_keys=1)
    counts = cnt[:, 0]
    starts = jnp.cumsum(counts) - counts
    nb_e = (counts + EXPERT_BLOCK - 1) // EXPERT_BLOCK
    cum_nb = jnp.cumsum(nb_e)
    n_blocks = n_assign // EXPERT_BLOCK + n_exp
    n_active = jnp.maximum(cum_nb[-1], 1).astype(I32)
    blk = jnp.minimum(jnp.arange(n_blocks, dtype=I32), n_active - 1)
    blk_expert = jnp.minimum(jnp.sum(blk[:, None] >= cum_nb[None, :], axis=1), n_exp - 1).astype(I32)
    is_e = blk_expert[:, None] == jnp.arange(n_exp, dtype=I32)[None, :]
    lookup = lambda table: jnp.sum(jnp.where(is_e, table[None, :], 0), axis=1)
    blk_j = blk - lookup(cum_nb - nb_e)
    blk_off = (lookup(starts) + blk_j * EXPERT_BLOCK).astype(I32)
    blk_valid = jnp.clip(lookup(counts) - blk_j * EXPERT_BLOCK, 0, EXPERT_BLOCK).astype(I32)

    ytk = _experts(h2p, order, blk_expert, blk_off, blk_valid, n_active.reshape(1), exp_gate[0], exp_up[0],
                   exp_down[0], n_blocks, m)
    out = _combine(ytk, wts.T, h2, x1, g2, shared_gate[0].astype(BF16), shared_up[0].astype(BF16),
                   shared_down[0].astype(BF16), seq, tmc=min(128, seq))
    return out.reshape(batch, seq, d)
```

```python
import functools
import math

import jax
import jax.numpy as jnp
from jax import lax
from jax.experimental import pallas as pl
from jax.experimental.pallas import tpu as pltpu

F32 = jnp.float32
BF16 = jnp.bfloat16
I32 = jnp.int32

GRID_W = 64
HEAD_DIM = 128
Q_GROUP = 4
ROPE_FREQS = HEAD_DIM // 4
ROPE_THETA = 10000.0
F_GROUP_DIM = 128
N_EXPERT_GROUPS = 8
TOPK_GROUPS = 4
TOP_K = 8
ROUTED_SCALE = 2.5
EPS = 1e-6
LANE = 128
EXPERT_BLOCK = 256
VMEM_LIMIT = 56 * 1024 * 1024


def _cparams(*sem):
    return pltpu.CompilerParams(dimension_semantics=sem, vmem_limit_bytes=VMEM_LIMIT)


def _silu(v):
    return v * jax.nn.sigmoid(v)


def _adaln_kernel(c_ref, w_ref, b_ref, o_ref):
    a = _silu(c_ref[...])
    o_ref[...] = jnp.dot(a.astype(BF16), w_ref[...].astype(BF16), preferred_element_type=F32) + b_ref[...]


def _adaln(cond_pad, mod_w, mod_b):
    d, n = mod_w.shape
    tn = min(n, 1024)
    return pl.pallas_call(
        _adaln_kernel,
        grid=(n // tn,),
        in_specs=[pl.BlockSpec((8, d), lambda j: (0, 0)),
                  pl.BlockSpec((d, tn), lambda j: (0, j)),
                  pl.BlockSpec((1, tn), lambda j: (0, j))],
        out_specs=pl.BlockSpec((8, tn), lambda j: (0, j)),
        out_shape=jax.ShapeDtypeStruct((8, n), F32),
        compiler_params=_cparams("arbitrary"),
        name="adaln",
    )(cond_pad, mod_w, mod_b.reshape(1, n))


def _head_norm_rope(a, w, cos, sin, scale):
    ms = jnp.mean(a * a, axis=-1, keepdims=True)
    a = a * lax.rsqrt(ms + EPS) * w
    if cos is not None:
        lane = lax.broadcasted_iota(I32, a.shape, 1)
        first = (lane % (2 * ROPE_FREQS)) < ROPE_FREQS
        partner = jnp.where(first, pltpu.roll(a, HEAD_DIM - ROPE_FREQS, 1), pltpu.roll(a, ROPE_FREQS, 1))
        a = a * cos + partner * sin
    if scale is not None:
        a = a * scale
    return a


def _inproj_kernel(x_ref, sh_ref, sc_ref, nw_ref, w_ref, qw_ref, kw_ref, cos_ref, sin_ref, o_ref, h_scr, wb_scr,
                   acc_scr, *, modes, tn):
    j = pl.program_id(1)

    @pl.when(j == 0)
    def _():
        x = x_ref[...]
        ms = jnp.mean(x * x, axis=-1, keepdims=True)
        y = x * lax.rsqrt(ms + EPS) * nw_ref[...]
        h_scr[...] = (y * (1.0 + sc_ref[...]) + sh_ref[...]).astype(BF16)

    wb_scr[...] = w_ref[...].astype(BF16)

    def tile(c0, c1):
        return jnp.dot(h_scr[...], wb_scr[:, c0:c1], preferred_element_type=F32)

    piece = 2 * HEAD_DIM if tn % (2 * HEAD_DIM) == 0 else HEAD_DIM
    gated = functools.reduce(jnp.logical_or, [(j >= j0) & (j < j1) for mode, j0, j1 in modes if mode == "sigmoid"],
                             jnp.bool_(False))

    @pl.when(gated)
    def _():
        for c0 in range(0, tn, piece):
            o_ref[:, c0:c0 + piece] = jax.nn.sigmoid(tile(c0, c0 + piece)).astype(BF16)

    @pl.when(jnp.logical_not(gated))
    def _():
        acc_scr[...] = tile(0, tn)

    for mode, j0, j1 in modes:
        if mode == "sigmoid":
            continue

        @pl.when((j >= j0) & (j < j1))
        def _(mode=mode):
            if mode == "plain":
                o_ref[...] = acc_scr[...].astype(BF16)
                return
            rope = mode in ("q", "k")
            w = qw_ref[...] if mode == "q" else kw_ref[...]
            scale = HEAD_DIM ** -0.5 * math.log2(math.e) if mode == "q" else None
            for h in range(tn // HEAD_DIM):
                sl = slice(h * HEAD_DIM, (h + 1) * HEAD_DIM)
                a = _head_norm_rope(acc_scr[:, sl], w, cos_ref[...] if rope else None,
                                    sin_ref[...] if rope else None, scale)
                o_ref[:, sl] = a.astype(BF16)


def _inproj(x2d, shift, scale, norm_w, w_in, q_norm_w, k_norm_w, cos_t, sin_t, modes, seq, tm, tn):
    m, d = x2d.shape
    n = w_in.shape[1]
    per_b = seq // tm
    pos_blocks = cos_t.shape[0] // tm
    kern = functools.partial(_inproj_kernel, modes=modes, tn=tn)
    return pl.pallas_call(
        kern,
        grid=(m // tm, n // tn),
        in_specs=[pl.BlockSpec((tm, d), lambda i, j: (i, 0)),
                  pl.BlockSpec((None, 1, d), lambda i, j: (i // per_b, 0, 0)),
                  pl.BlockSpec((None, 1, d), lambda i, j: (i // per_b, 0, 0)),
                  pl.BlockSpec((1, d), lambda i, j: (0, 0)),
                  pl.BlockSpec((d, tn), lambda i, j: (0, j)),
                  pl.BlockSpec((1, HEAD_DIM), lambda i, j: (0, 0)),
                  pl.BlockSpec((1, HEAD_DIM), lambda i, j: (0, 0)),
                  pl.BlockSpec((tm, HEAD_DIM), lambda i, j: (i % pos_blocks, 0)),
                  pl.BlockSpec((tm, HEAD_DIM), lambda i, j: (i % pos_blocks, 0))],
        out_specs=pl.BlockSpec((tm, tn), lambda i, j: (i, j)),
        out_shape=jax.ShapeDtypeStruct((m, n), BF16),
        scratch_shapes=[pltpu.VMEM((tm, d), BF16), pltpu.VMEM((d, tn), BF16), pltpu.VMEM((tm, tn), F32)],
        compiler_params=_cparams("arbitrary", "arbitrary"),
        name="inproj",
    )(x2d, shift, scale, norm_w.reshape(1, d), w_in, q_norm_w.reshape(1, HEAD_DIM),
      k_norm_w.reshape(1, HEAD_DIM), cos_t, sin_t)


def _rope_tables(seq):
    rows = seq // GRID_W
    row = jnp.repeat(jnp.arange(rows), GRID_W)
    col = jnp.tile(jnp.arange(GRID_W), rows)
    pos = jnp.stack([row, col], axis=-1).astype(F32)
    inv_freq = ROPE_THETA ** (-jnp.arange(ROPE_FREQS, dtype=F32) / ROPE_FREQS)
    ang = pos[:, :, None] * inv_freq
    cos, sin = jnp.cos(ang), jnp.sin(ang)
    cos_t = jnp.concatenate([cos[:, 0], cos[:, 0], cos[:, 1], cos[:, 1]], axis=-1)
    sin_t = jnp.concatenate([-sin[:, 0], sin[:, 0], -sin[:, 1], sin[:, 1]], axis=-1)
    return cos_t, sin_t


def _attn_kernel(q_ref, k_ref, v_ref, kc_ref, vc_ref, o_ref, vt_scr, vct_scr, *, tq, tk, n_chunks):
    @pl.when(pl.program_id(2) == 0)
    def _():
        for c in range(n_chunks):
            vt_scr[c] = jnp.transpose(v_ref[c * tk:(c + 1) * tk, :].astype(F32)).astype(BF16)
        vct_scr[...] = jnp.transpose(vc_ref[...].astype(F32)).astype(BF16)

    q = jnp.concatenate([q_ref[:, h * HEAD_DIM:(h + 1) * HEAD_DIM] for h in range(Q_GROUP)], axis=0)
    qt = jnp.transpose(q.astype(F32)).astype(BF16)
    m_cols = Q_GROUP * tq

    def scores(kb):
        return jnp.dot(kb, qt, preferred_element_type=F32)

    def update(st, vtb, carry):
        m, l, acc = carry
        m_new = jnp.maximum(m, jnp.max(st, axis=0, keepdims=True))
        alpha = jnp.exp2(m - m_new)
        pt = jnp.exp2(st - m_new)
        l = alpha * l + jnp.sum(pt, axis=0, keepdims=True)
        acc = alpha * acc + jnp.dot(vtb, pt.astype(BF16), preferred_element_type=F32)
        return m_new, l, acc

    carry = (jnp.full((1, m_cols), -jnp.inf, F32), jnp.zeros((1, m_cols), F32),
             jnp.zeros((HEAD_DIM, m_cols), F32))
    st = scores(k_ref[0:tk, :])
    for c in range(n_chunks):
        st_next = scores(k_ref[(c + 1) * tk:(c + 2) * tk, :]) if c + 1 < n_chunks else scores(kc_ref[...])
        carry = update(st, vt_scr[c], carry)
        st = st_next
    _, l, acc = update(st, vct_scr[...], carry)
    o = jnp.transpose(acc / l)
    for h in range(Q_GROUP):
        o_ref[:, h * HEAD_DIM:(h + 1) * HEAD_DIM] = o[h * tq:(h + 1) * tq].astype(BF16)


def _attention(proj, kvc, batch, seq, n_ctx, n_kv, q_off, k_off, v_off, tq, tk):
    gw = Q_GROUP * HEAD_DIM
    per_b = seq // tq
    n_chunks = seq // tk
    kern = functools.partial(_attn_kernel, tq=tq, tk=tk, n_chunks=n_chunks)
    return pl.pallas_call(
        kern,
        grid=(batch, n_kv, per_b),
        in_specs=[pl.BlockSpec((tq, gw), lambda b, g, i: (b * per_b + i, q_off // gw + g)),
                  pl.BlockSpec((seq, HEAD_DIM), lambda b, g, i: (b, k_off // HEAD_DIM + g)),
                  pl.BlockSpec((seq, HEAD_DIM), lambda b, g, i: (b, v_off // HEAD_DIM + g)),
                  pl.BlockSpec((n_ctx, HEAD_DIM), lambda b, g, i: (b, g)),
                  pl.BlockSpec((n_ctx, HEAD_DIM), lambda b, g, i: (b, n_kv + g))],
        out_specs=pl.BlockSpec((tq, gw), lambda b, g, i: (b * per_b + i, g)),
        out_shape=jax.ShapeDtypeStruct((batch * seq, n_kv * gw), BF16),
        scratch_shapes=[pltpu.VMEM((n_chunks, HEAD_DIM, tk), BF16), pltpu.VMEM((HEAD_DIM, n_ctx), BF16)],
        compiler_params=_cparams("arbitrary", "arbitrary", "arbitrary"),
        name="attention",
    )(proj, proj, proj, kvc, kvc)


def _dft_mats(n_rows, n_cols):
    def cs(k, n):
        ang = (2.0 * math.pi / n) * (k % n).astype(F32)
        return jnp.cos(ang), jnp.sin(ang)

    ch = jnp.arange(F_GROUP_DIM)
    cc, sc = cs(ch[:, None] * ch[None, :], F_GROUP_DIM)
    mat_a = jnp.concatenate([cc, -sc], axis=1).astype(BF16)
    r = jnp.arange(n_rows)
    cr, sr = cs(r[:, None] * r[None, :], n_rows)
    mat_1 = jnp.concatenate([jnp.concatenate([cr, sr], axis=1),
                             jnp.concatenate([-sr, cr], axis=1)], axis=0).astype(BF16)
    c = jnp.arange(n_cols)
    c2, s2 = cs(c[:, None] * c[None, :], n_cols)
    mat_2 = jnp.concatenate([c2, s2], axis=1).astype(BF16)
    tr, ts = cs(r[:, None] * c[None, :], n_rows * n_cols)
    tw_r = jnp.repeat(tr, LANE, axis=1)
    tw_i = jnp.repeat(-ts, LANE, axis=1)
    return mat_a, mat_1, mat_2, tw_r, tw_i


def _fourier_a_kernel(u_ref, m_ref, vr_ref, vi_ref, *, groups):
    for g in range(groups):
        sl = slice(g * F_GROUP_DIM, (g + 1) * F_GROUP_DIM)
        r = jnp.dot(u_ref[:, sl], m_ref[...], preferred_element_type=F32)
        vr_ref[:, sl] = r[:, :F_GROUP_DIM].astype(BF16)
        vi_ref[:, sl] = r[:, F_GROUP_DIM:].astype(BF16)


def _fourier_1_kernel(ar_ref, ai_ref, m_ref, twr_ref, twi_ref, pr_ref, pi_ref, *, nseg, fw, n_rows):
    rhs = jnp.concatenate([ar_ref[...], ai_ref[...]], axis=0)
    z = jnp.dot(m_ref[...], rhs, preferred_element_type=F32)
    reps = fw // LANE
    for s in range(nseg):
        zr = z[:n_rows, s * fw:(s + 1) * fw]
        zi = z[n_rows:, s * fw:(s + 1) * fw]
        tr = jnp.tile(twr_ref[:, s * LANE:(s + 1) * LANE], (1, reps))
        ti = jnp.tile(twi_ref[:, s * LANE:(s + 1) * LANE], (1, reps))
        pr_ref[s] = (zr * tr - zi * ti).astype(BF16)
        pi_ref[s] = (zr * ti + zi * tr).astype(BF16)


def _fourier_2_kernel(ar_ref, ai_ref, m_ref, o_ref, *, norm):
    rhs = jnp.concatenate([ar_ref[...], ai_ref[...]], axis=0)
    o_ref[...] = (jnp.dot(m_ref[...], rhs, preferred_element_type=F32) * norm).astype(BF16)


def _fourier_mix(proj, batch, seq, fw):
    n_cols = GRID_W
    n_rows = seq // GRID_W
    groups = fw // F_GROUP_DIM
    mat_a, mat_1, mat_2, tw_r, tw_i = _dft_mats(n_rows, n_cols)
    m = batch * seq
    tm = min(seq, 1024)
    vr, vi = pl.pallas_call(
        functools.partial(_fourier_a_kernel, groups=groups),
        grid=(m // tm,),
        in_specs=[pl.BlockSpec((tm, fw), lambda i: (i, 0)),
                  pl.BlockSpec((F_GROUP_DIM, 2 * F_GROUP_DIM), lambda i: (0, 0))],
        out_specs=[pl.BlockSpec((tm, fw), lambda i: (i, 0))] * 2,
        out_shape=[jax.ShapeDtypeStruct((m, fw), BF16)] * 2,
        compiler_params=_cparams("arbitrary"),
        name="fourier_channels",
    )(proj, mat_a)
    wide = n_cols * fw
    vr = vr.reshape(batch * n_rows, wide)
    vi = vi.reshape(batch * n_rows, wide)
    nseg = min(n_cols, 4)
    pr, pi = pl.pallas_call(
        functools.partial(_fourier_1_kernel, nseg=nseg, fw=fw, n_rows=n_rows),
        grid=(batch, n_cols // nseg),
        in_specs=[pl.BlockSpec((n_rows, nseg * fw), lambda b, j: (b, j)),
                  pl.BlockSpec((n_rows, nseg * fw), lambda b, j: (b, j)),
                  pl.BlockSpec((2 * n_rows, 2 * n_rows), lambda b, j: (0, 0)),
                  pl.BlockSpec((n_rows, nseg * LANE), lambda b, j: (0, j)),
                  pl.BlockSpec((n_rows, nseg * LANE), lambda b, j: (0, j))],
        out_specs=[pl.BlockSpec((nseg, n_rows, fw), lambda b, j: (b * (n_cols // nseg) + j, 0, 0))] * 2,
        out_shape=[jax.ShapeDtypeStruct((batch * n_cols, n_rows, fw), BF16)] * 2,
        compiler_params=_cparams("arbitrary", "arbitrary"),
        name="fourier_rows",
    )(vr, vi, mat_1, tw_r, tw_i)
    wide2 = n_rows * fw
    pr = pr.reshape(batch * n_cols, wide2)
    pi = pi.reshape(batch * n_cols, wide2)
    tc = min(wide2, 8192)
    norm = 1.0 / math.sqrt(seq * F_GROUP_DIM)
    yf = pl.pallas_call(
        functools.partial(_fourier_2_kernel, norm=norm),
        grid=(batch, wide2 // tc),
        in_specs=[pl.BlockSpec((n_cols, tc), lambda b, j: (b, j)),
                  pl.BlockSpec((n_cols, tc), lambda b, j: (b, j)),
                  pl.BlockSpec((n_cols, 2 * n_cols), lambda b, j: (0, 0))],
        out_specs=pl.BlockSpec((n_cols, tc), lambda b, j: (b, j)),
        out_shape=jax.ShapeDtypeStruct((batch * n_cols, wide2), BF16),
        compiler_params=_cparams("arbitrary", "arbitrary"),
        name="fourier_cols",
    )(pr, pi, mat_2)
    return yf.reshape(m, fw)


def _merge_kernel(yf_ref, ya_ref, wfo_ref, wao_ref, gf_ref, ga_ref, o_ref):
    a = jnp.dot(yf_ref[...], wfo_ref[...], preferred_element_type=F32)
    b = jnp.dot(ya_ref[...], wao_ref[...], preferred_element_type=F32)
    o_ref[...] = (gf_ref[...].astype(F32) * a + ga_ref[...].astype(F32) * b).astype(BF16)


def _merge(yf, ya, wfo, wao, proj, gf_off, ga_off, tm, tn):
    m, fw = yf.shape
    aw = ya.shape[1]
    d = wfo.shape[1]
    return pl.pallas_call(
        _merge_kernel,
        grid=(m // tm, d // tn),
        in_specs=[pl.BlockSpec((tm, fw), lambda i, j: (i, 0)),
                  pl.BlockSpec((tm, aw), lambda i, j: (i, 0)),
                  pl.BlockSpec((fw, tn), lambda i, j: (0, j)),
                  pl.BlockSpec((aw, tn), lambda i, j: (0, j)),
                  pl.BlockSpec((tm, tn), lambda i, j: (i, gf_off // tn + j)),
                  pl.BlockSpec((tm, tn), lambda i, j: (i, ga_off // tn + j))],
        out_specs=pl.BlockSpec((tm, tn), lambda i, j: (i, j)),
        out_shape=jax.ShapeDtypeStruct((m, d), BF16),
        compiler_params=_cparams("arbitrary", "arbitrary"),
        name="merge",
    )(yf, ya, wfo, wao, proj, proj)


def _route(logits_t, bias, n_exp):
    gsz = n_exp // N_EXPERT_GROUPS
    tm = logits_t.shape[1]
    neg = -jnp.inf
    scores = jax.nn.sigmoid(logits_t)
    biased = scores + bias
    io_g = lax.broadcasted_iota(I32, (gsz, tm), 0).astype(F32)
    gs = []
    for g in range(N_EXPERT_GROUPS):
        grp = biased[g * gsz:(g + 1) * gsz, :]
        m1 = jnp.max(grp, axis=0, keepdims=True)
        i1 = jnp.min(jnp.where(grp == m1, io_g, float(gsz)), axis=0, keepdims=True)
        m2 = jnp.max(jnp.where(io_g == i1, neg, grp), axis=0, keepdims=True)
        gs.append(m1 + m2)
    cur = jnp.concatenate(gs, axis=0)
    io_n = lax.broadcasted_iota(I32, (N_EXPERT_GROUPS, tm), 0).astype(F32)
    sel = jnp.zeros((N_EXPERT_GROUPS, tm), F32)
    for _ in range(TOPK_GROUPS):
        mx = jnp.max(cur, axis=0, keepdims=True)
        ix = jnp.min(jnp.where(cur == mx, io_n, float(N_EXPERT_GROUPS)), axis=0, keepdims=True)
        hit = io_n == ix
        sel = jnp.where(hit, 1.0, sel)
        cur = jnp.where(hit, neg, cur)
    cur = jnp.concatenate([jnp.where(sel[g:g + 1, :] > 0.0, biased[g * gsz:(g + 1) * gsz, :], neg)
                           for g in range(N_EXPERT_GROUPS)], axis=0)
    io_e = lax.broadcasted_iota(I32, (n_exp, tm), 0).astype(F32)
    eidx, wts = [], []
    for _ in range(TOP_K):
        mx = jnp.max(cur, axis=0, keepdims=True)
        ix = jnp.min(jnp.where(cur == mx, io_e, float(n_exp)), axis=0, keepdims=True)
        hit = io_e == ix
        wts.append(jnp.sum(jnp.where(hit, scores, 0.0), axis=0, keepdims=True))
        eidx.append(ix)
        cur = jnp.where(hit, neg, cur)
    eidx = jnp.concatenate(eidx, axis=0).astype(I32)
    wts = jnp.concatenate(wts, axis=0)
    wts = wts / jnp.sum(wts, axis=0, keepdims=True) * ROUTED_SCALE
    return eidx, wts


def _outproj_kernel(y_ref, wo_ref, x_ref, g1_ref, nw_ref, sh_ref, sc_ref, rw_ref, rb_ref,
                    x1_ref, h2_ref, h2p_ref, eidx_ref, wts_ref, cnt_ref, carry_scr, *, n_exp):
    i = pl.program_id(0)
    tm, d = x_ref.shape

    @pl.when(i == 0)
    def _():
        carry_scr[...] = jnp.zeros_like(carry_scr)

    x1 = x_ref[...] + g1_ref[...] * jnp.dot(y_ref[...], wo_ref[...], preferred_element_type=F32)
    x1_ref[...] = x1
    ms = jnp.mean(x1 * x1, axis=-1, keepdims=True)
    h2 = (x1 * lax.rsqrt(ms + EPS) * nw_ref[...]) * (1.0 + sc_ref[...]) + sh_ref[...]
    h2b = h2.astype(BF16)
    h2_ref[...] = h2b
    n_sub = d // LANE
    for s in range(n_sub):
        h2p_ref[pl.ds(s, tm, stride=n_sub), :] = h2[:, s * LANE:(s + 1) * LANE]
    h_hi = h2b
    h_lo = (h2 - h_hi.astype(F32)).astype(BF16)
    part = (jnp.dot(h_hi, rw_ref[...], preferred_element_type=F32)
            + jnp.dot(h_lo, rw_ref[...], preferred_element_type=F32))
    logits = part[:, :LANE] + part[:, LANE:]
    logits_t = jnp.transpose(logits)[:n_exp, :]
    eidx, wts = _route(logits_t, rb_ref[...], n_exp)
    eidx_ref[...] = eidx
    wts_ref[...] = wts
    io_e = lax.broadcasted_iota(I32, (n_exp, tm), 0)
    onehot = jnp.zeros((n_exp, tm), F32)
    for k in range(TOP_K):
        onehot = onehot + jnp.where(io_e == eidx[k:k + 1, :], 1.0, 0.0)
    carry_scr[...] = carry_scr[...] + jnp.sum(onehot, axis=1, keepdims=True)
    cnt_ref[...] = carry_scr[...].astype(I32)


def _outproj_route(y, wo, x2d, g1, norm_w, shift, scale, rw_pad, rb, seq, n_exp, tm):
    m, d = x2d.shape
    per_b = seq // tm
    n_sub = d // LANE
    bspec = pl.BlockSpec((None, 1, d), lambda i: (i // per_b, 0, 0))
    row = pl.BlockSpec((tm, d), lambda i: (i, 0))
    tok = pl.BlockSpec((TOP_K, tm), lambda i: (0, i))
    return pl.pallas_call(
        functools.partial(_outproj_kernel, n_exp=n_exp),
        grid=(m // tm,),
        in_specs=[row,
                  pl.BlockSpec((d, d), lambda i: (0, 0), pipeline_mode=pl.Buffered(1)),
                  row, bspec,
                  pl.BlockSpec((1, d), lambda i: (0, 0)),
                  bspec, bspec,
                  pl.BlockSpec((d, 2 * LANE), lambda i: (0, 0)),
                  pl.BlockSpec((n_exp, 1), lambda i: (0, 0))],
        out_specs=[row, row,
                   pl.BlockSpec((tm * n_sub, LANE), lambda i: (i, 0)),
                   tok, tok,
                   pl.BlockSpec((n_exp, LANE), lambda i: (0, 0))],
        out_shape=[jax.ShapeDtypeStruct((m, d), F32),
                   jax.ShapeDtypeStruct((m, d), BF16),
                   jax.ShapeDtypeStruct((m * n_sub, LANE), F32),
                   jax.ShapeDtypeStruct((TOP_K, m), I32),
                   jax.ShapeDtypeStruct((TOP_K, m), F32),
                   jax.ShapeDtypeStruct((n_exp, LANE), I32)],
        scratch_shapes=[pltpu.VMEM((n_exp, LANE), F32)],
        compiler_params=_cparams("arbitrary"),
        name="outproj_route",
    )(y, wo, x2d, g1, norm_w.reshape(1, d), shift, scale, rw_pad, rb.reshape(n_exp, 1))


def _expert_kernel(be_ref, bwin_ref, brem_ref, bval_ref, nact_ref, prev_win, cur_win, nxt_win,
                   h2p_hbm, wg_ref, wu_ref, wd_ref, ytk_hbm,
                   wg_s, wu_s, wd_s, xbuf0, xbuf1, ybuf0, ybuf1, sem_g, sem_s, sem_z,
                   *, n_sub, n_tok, n_blocks, trash_base):
    i = pl.program_id(0)
    nact = nact_ref[0]
    bm = EXPERT_BLOCK
    xbufs = (xbuf0, xbuf1)
    ybufs = (ybuf0, ybuf1)
    unroll = 8

    def token_of(f):
        return f & (n_tok - 1) if n_tok & (n_tok - 1) == 0 else lax.rem(f, n_tok)

    def gather_copy(tok, r, par):
        return pltpu.make_async_copy(h2p_hbm.at[pl.ds(pl.multiple_of(tok * n_sub, n_sub), n_sub), :],
                                     xbufs[par].at[pl.ds(pl.multiple_of(r * n_sub, n_sub), n_sub), :],
                                     sem_g.at[par])

    def scatter_copy(r, row, par):
        return pltpu.make_async_copy(ybufs[par].at[pl.ds(pl.multiple_of(r * n_sub, n_sub), n_sub), :],
                                     ytk_hbm.at[pl.ds(pl.multiple_of(row * n_sub, n_sub), n_sub), :],
                                     sem_s.at[par])

    def wait_gathers(par):
        @pl.loop(0, bm // unroll)
        def _(q):
            for u in range(unroll):
                gather_copy(0, 0, par).wait()

    def wait_scatters(par):
        @pl.loop(0, bm // unroll)
        def _(q):
            for u in range(unroll):
                scatter_copy(0, 0, par).wait()

    def scatter_row(win_ref, rem, valid, r, par):
        row = jnp.where(r < valid, win_ref[0, rem + r], trash_base + par * bm + r)
        return scatter_copy(r, row, par)

    @pl.when(i == 0)
    def _():
        ybuf0[...] = jnp.zeros_like(ybuf0)
        ybuf1[...] = jnp.zeros_like(ybuf1)
        for par in range(2):
            start = (trash_base + par * bm) * n_sub
            cp = pltpu.make_async_copy(ybuf0, ytk_hbm.at[pl.ds(start, bm * n_sub), :], sem_z)
            cp.start()
            cp.wait()
        rem0 = brem_ref[0]

        @pl.loop(0, bm // unroll)
        def _(q):
            for u in range(unroll):
                r = q * unroll + u
                gather_copy(token_of(cur_win[0, rem0 + r]), r, 0).start(priority=u % 2)

    def block_step(par):
        wait_gathers(par)

        @pl.when(i >= 1)
        def _():
            wait_scatters(par)

        @pl.when((i == 0) | (be_ref[i] != be_ref[jnp.maximum(i - 1, 0)]))
        def _():
            wg_s[...] = wg_ref[...].astype(BF16)
            wu_s[...] = wu_ref[...].astype(BF16)
            wd_s[...] = wd_ref[...].astype(BF16)

        nxt_rem = brem_ref[jnp.minimum(i + 1, n_blocks - 1)]
        prv = jnp.maximum(i - 1, 0)
        prv_rem = brem_ref[prv]
        prv_valid = jnp.where(i >= 1, bval_ref[prv], 0)
        for r in range(bm):
            gather_copy(token_of(nxt_win[0, nxt_rem + r]), r, 1 - par).start(priority=r % 2)
        for r in range(bm):
            scatter_row(prev_win, prv_rem, prv_valid, r, 1 - par).start(priority=r % 2)

        x = jnp.concatenate([xbufs[par][pl.ds(s, bm, stride=n_sub), :] for s in range(n_sub)],
                            axis=1).astype(BF16)
        g = jnp.dot(x, wg_s[...], preferred_element_type=F32)
        u = jnp.dot(x, wu_s[...], preferred_element_type=F32)
        y = jnp.dot((_silu(g) * u).astype(BF16), wd_s[...], preferred_element_type=F32)
        for s in range(n_sub):
            ybufs[par][pl.ds(s, bm, stride=n_sub), :] = y[:, s * LANE:(s + 1) * LANE]

        @pl.when(i == nact - 1)
        def _():
            wait_gathers(1 - par)
            wait_scatters(1 - par)
            rem = brem_ref[i]
            valid = bval_ref[i]

            @pl.loop(0, bm // unroll)
            def _(q):
                for u_ in range(unroll):
                    scatter_row(cur_win, rem, valid, q * unroll + u_, par).start(priority=u_ % 2)

            wait_scatters(par)

    for par in range(2):
        @pl.when((i < nact) & (lax.rem(i, 2) == par))
        def _(par=par):
            block_step(par)


def _experts(h2p, order, blk_expert, blk_off, blk_valid, n_active, exp_gate, exp_up, exp_down, n_blocks, n_tok):
    n_exp, d, ff = exp_gate.shape
    n_sub = d // LANE
    bm = EXPERT_BLOCK
    n_win = order.shape[0] // bm
    pieces = order.reshape(n_win, 1, bm)
    windows = jnp.concatenate([pieces, jnp.roll(pieces, -1, axis=0)], axis=-1)
    blk_win = blk_off // bm
    blk_rem = blk_off - blk_win * bm
    trash_base = TOP_K * n_tok

    def win(step_shift):
        def index_map(i, be, bwin, *_):
            return (bwin[jnp.clip(i + step_shift, 0, n_blocks - 1)], 0, 0)
        return pl.BlockSpec((None, 1, 2 * bm), index_map, memory_space=pltpu.SMEM)

    row_buf = pltpu.VMEM((bm * n_sub, LANE), F32)
    grid_spec = pltpu.PrefetchScalarGridSpec(
        num_scalar_prefetch=5,
        grid=(n_blocks,),
        in_specs=[win(-1), win(0), win(1),
                  pl.BlockSpec(memory_space=pl.ANY),
                  pl.BlockSpec((None, d, ff), lambda i, be, *_: (be[i], 0, 0)),
                  pl.BlockSpec((None, d, ff), lambda i, be, *_: (be[i], 0, 0)),
                  pl.BlockSpec((None, ff, d), lambda i, be, *_: (be[i], 0, 0))],
        out_specs=pl.BlockSpec(memory_space=pl.ANY),
        scratch_shapes=[pltpu.VMEM((d, ff), BF16), pltpu.VMEM((d, ff), BF16), pltpu.VMEM((ff, d), BF16),
                        row_buf, row_buf, row_buf, row_buf,
                        pltpu.SemaphoreType.DMA((2,)), pltpu.SemaphoreType.DMA((2,)),
                        pltpu.SemaphoreType.DMA(())],
    )
    return pl.pallas_call(
        functools.partial(_expert_kernel, n_sub=n_sub, n_tok=n_tok, n_blocks=n_blocks, trash_base=trash_base),
        grid_spec=grid_spec,
        out_shape=jax.ShapeDtypeStruct(((trash_base + 2 * bm) * n_sub, LANE), F32),
        compiler_params=_cparams("arbitrary"),
        name="experts",
    )(blk_expert, blk_win, blk_rem, blk_valid, n_active, windows, windows, windows, h2p, exp_gate, exp_up,
      exp_down)


def _combine_kernel(*refs, tmc, n_out):
    yk_refs = refs[:TOP_K]
    wt_ref, h2_ref, x1_ref, g2_ref, sg_ref, su_ref, sd_ref, o_ref = refs[TOP_K:]
    h = h2_ref[...]
    g = jnp.dot(h, sg_ref[...], preferred_element_type=F32)
    u = jnp.dot(h, su_ref[...], preferred_element_type=F32)
    routed = jnp.dot((_silu(g) * u).astype(BF16), sd_ref[...], preferred_element_type=F32)
    for k in range(TOP_K):
        rows = jnp.concatenate([yk_refs[k][pl.ds(s, tmc, stride=n_out), :] for s in range(n_out)], axis=1)
        routed = routed + wt_ref[:, k:k + 1] * rows
    o_ref[...] = x1_ref[...] + g2_ref[...] * routed


def _combine(ytk, wts_t, h2, x1, g2, sg, su, sd, seq, tmc):
    m, d = x1.shape
    ff = sg.shape[1]
    n_out = d // LANE
    per_b = seq // tmc
    tiles = m // tmc
    row = pl.BlockSpec((tmc, d), lambda i: (i, 0))
    yk_specs = [pl.BlockSpec((tmc * n_out, LANE), lambda i, k=k: (k * tiles + i, 0)) for k in range(TOP_K)]
    return pl.pallas_call(
        functools.partial(_combine_kernel, tmc=tmc, n_out=n_out),
        grid=(tiles,),
        in_specs=yk_specs + [pl.BlockSpec((tmc, TOP_K), lambda i: (i, 0)),
                             row, row,
                             pl.BlockSpec((None, 1, d), lambda i: (i // per_b, 0, 0)),
                             pl.BlockSpec((d, ff), lambda i: (0, 0)),
                             pl.BlockSpec((d, ff), lambda i: (0, 0)),
                             pl.BlockSpec((ff, d), lambda i: (0, 0))],
        out_specs=row,
        out_shape=jax.ShapeDtypeStruct((m, d), F32),
        compiler_params=_cparams("arbitrary"),
        name="combine",
    )(*([ytk] * TOP_K), wts_t, h2, x1, g2, sg, su, sd)


def kernel(x, c, ctx, c_ctx, mod_w, mod_b, norm1_w, w_in, q_norm_w, k_norm_w, w_fourier_out, w_attn_out, w_out,
           norm2_w, router_w, router_b, exp_gate, exp_up, exp_down, shared_gate, shared_up, shared_down):
    batch, seq, d = x.shape
    n_ctx = ctx.shape[1]
    assert mod_w.shape[0] == 1, "single-layer block only"
    assert batch + 1 <= 8 and seq % GRID_W == 0
    fw = w_fourier_out.shape[1]
    att = w_attn_out.shape[1]
    in_cols = w_in.shape[2]
    kvw = (in_cols - fw - att - 2 * d) // 2
    n_kv = kvw // HEAD_DIM
    n_exp = router_w.shape[2]
    assert att == n_kv * Q_GROUP * HEAD_DIM and n_exp <= LANE
    q_off, k_off = fw, fw + att
    v_off = k_off + kvw
    gf_off = v_off + kvw
    ga_off = gf_off + d
    m = batch * seq
    assert q_off % (Q_GROUP * HEAD_DIM) == 0 and kvw % HEAD_DIM == 0 and fw % F_GROUP_DIM == 0
    assert gf_off % min(512, d) == 0 and d % (2 * LANE) == 0

    cond = jnp.concatenate([c, c_ctx[None], jnp.zeros((8 - batch - 1, d), F32)], axis=0)
    mod = _adaln(cond, mod_w[0], mod_b[0])
    sh1, sc1, g1, sh2, sc2, g2 = [mod[:batch, j * d:(j + 1) * d].reshape(batch, 1, d) for j in range(6)]
    csh1 = jnp.broadcast_to(mod[batch, 0:d], (batch, 1, d))
    csc1 = jnp.broadcast_to(mod[batch, d:2 * d], (batch, 1, d))

    cos_t, sin_t = _rope_tables(seq)
    tn = min(512, kvw)
    tm = min(1024, seq)
    modes = (("plain", 0, q_off // tn), ("q", q_off // tn, k_off // tn), ("k", k_off // tn, v_off // tn),
             ("plain", v_off // tn, gf_off // tn), ("sigmoid", gf_off // tn, in_cols // tn))
    proj = _inproj(x.reshape(m, d), sh1, sc1, norm1_w[0], w_in[0], q_norm_w[0], k_norm_w[0], cos_t, sin_t,
                   modes, seq, tm, tn)
    cmodes = (("k_norope", 0, kvw // tn), ("plain", kvw // tn, 2 * kvw // tn))
    tmc_ctx = min(n_ctx, 256)
    kvc = _inproj(ctx.reshape(batch * n_ctx, d), csh1, csc1, norm1_w[0], w_in[0][:, k_off:gf_off], q_norm_w[0],
                  k_norm_w[0], cos_t[:tmc_ctx], sin_t[:tmc_ctx], cmodes, n_ctx, tmc_ctx, tn)

    ya = _attention(proj, kvc, batch, seq, n_ctx, n_kv, q_off, k_off, v_off, tq=min(256, seq), tk=min(1024, seq))
    yf = _fourier_mix(proj, batch, seq, fw)
    y = _merge(yf, ya, w_fourier_out[0].astype(BF16), w_attn_out[0].astype(BF16), proj, gf_off, ga_off,
               tm=min(1024, seq), tn=min(512, d))

    rw_pad = jnp.pad(router_w[0], ((0, 0), (0, LANE - n_exp)))
    rw_hi = rw_pad.astype(BF16)
    rw_pad = jnp.concatenate([rw_hi, (rw_pad - rw_hi.astype(F32)).astype(BF16)], axis=1)
    x1, h2, h2p, eidx, wts, cnt = _outproj_route(
        y, w_out[0].astype(BF16), x.reshape(m, d), g1, norm2_w[0], sh2, sc2, rw_pad, router_b[0], seq, n_exp,
        tm=min(512, seq))

    n_assign = m * TOP_K
    assert n_assign % EXPERT_BLOCK == 0
    _, order = lax.sort((eidx.reshape(-1), jnp.arange(n_assign, dtype=I32)), num_keys=1)
    counts = cnt[:, 0]
    starts = jnp.cumsum(counts) - counts
    nb_e = (counts + EXPERT_BLOCK - 1) // EXPERT_BLOCK
    cum_nb = jnp.cumsum(nb_e)
    n_blocks = n_assign // EXPERT_BLOCK + n_exp
    n_active = jnp.maximum(cum_nb[-1], 1).astype(I32)
    blk = jnp.minimum(jnp.arange(n_blocks, dtype=I32), n_active - 1)
    blk_expert = jnp.minimum(jnp.sum(blk[:, None] >= cum_nb[None, :], axis=1), n_exp - 1).astype(I32)
    is_e = blk_expert[:, None] == jnp.arange(n_exp, dtype=I32)[None, :]
    lookup = lambda table: jnp.sum(jnp.where(is_e, table[None, :], 0), axis=1)
    blk_j = blk - lookup(cum_nb - nb_e)
    blk_off = (lookup(starts) + blk_j * EXPERT_BLOCK).astype(I32)
    blk_valid = jnp.clip(lookup(counts) - blk_j * EXPERT_BLOCK, 0, EXPERT_BLOCK).astype(I32)

    ytk = _experts(h2p, order, blk_expert, blk_off, blk_valid, n_active.reshape(1), exp_gate[0], exp_up[0],
                   exp_down[0], n_blocks, m)
    out = _combine(ytk, wts.T, h2, x1, g2, shared_gate[0].astype(BF16), shared_up[0].astype(BF16),
                   shared_down[0].astype(BF16), seq, tmc=min(128, seq))
    return out.reshape(batch, seq, d)
```

```python
import functools
import math

import jax
import jax.numpy as jnp
from jax import lax
from jax.experimental import pallas as pl
from jax.experimental.pallas import tpu as pltpu

F32 = jnp.float32
BF16 = jnp.bfloat16
I32 = jnp.int32

GRID_W = 64
HEAD_DIM = 128
Q_GROUP = 4
ROPE_FREQS = HEAD_DIM // 4
ROPE_THETA = 10000.0
F_GROUP_DIM = 128
N_EXPERT_GROUPS = 8
TOPK_GROUPS = 4
TOP_K = 8
ROUTED_SCALE = 2.5
EPS = 1e-6
LANE = 128
EXPERT_BLOCK = 256
VMEM_LIMIT = 56 * 1024 * 1024


def _cparams(*sem):
    return pltpu.CompilerParams(dimension_semantics=sem, vmem_limit_bytes=VMEM_LIMIT)


def _silu(v):
    return v * jax.nn.sigmoid(v)


def _adaln_kernel(c_ref, w_ref, b_ref, o_ref):
    a = _silu(c_ref[...])
    o_ref[...] = jnp.dot(a.astype(BF16), w_ref[...].astype(BF16), preferred_element_type=F32) + b_ref[...]


def _adaln(cond_pad, mod_w, mod_b):
    d, n = mod_w.shape
    tn = min(n, 1024)
    return pl.pallas_call(
        _adaln_kernel,
        grid=(n // tn,),
        in_specs=[pl.BlockSpec((8, d), lambda j: (0, 0)),
                  pl.BlockSpec((d, tn), lambda j: (0, j)),
                  pl.BlockSpec((1, tn), lambda j: (0, j))],
        out_specs=pl.BlockSpec((8, tn), lambda j: (0, j)),
        out_shape=jax.ShapeDtypeStruct((8, n), F32),
        compiler_params=_cparams("arbitrary"),
        name="adaln",
    )(cond_pad, mod_w, mod_b.reshape(1, n))


def _head_norm_rope(a, w, cos, sin, scale):
    ms = jnp.mean(a * a, axis=-1, keepdims=True)
    a = a * lax.rsqrt(ms + EPS) * w
    if cos is not None:
        lane = lax.broadcasted_iota(I32, a.shape, 1)
        first = (lane % (2 * ROPE_FREQS)) < ROPE_FREQS
        partner = jnp.where(first, pltpu.roll(a, HEAD_DIM - ROPE_FREQS, 1), pltpu.roll(a, ROPE_FREQS, 1))
        a = a * cos + partner * sin
    if scale is not None:
        a = a * scale
    return a


def _inproj_kernel(x_ref, sh_ref, sc_ref, nw_ref, w_ref, qw_ref, kw_ref, cos_ref, sin_ref, o_ref, h_scr, wb_scr,
                   acc_scr, *, modes, tn):
    j = pl.program_id(1)

    @pl.when(j == 0)
    def _():
        x = x_ref[...]
        ms = jnp.mean(x * x, axis=-1, keepdims=True)
        y = x * lax.rsqrt(ms + EPS) * nw_ref[...]
        h_scr[...] = (y * (1.0 + sc_ref[...]) + sh_ref[...]).astype(BF16)

    wb_scr[...] = w_ref[...].astype(BF16)

    def tile(c0, c1):
        return jnp.dot(h_scr[...], wb_scr[:, c0:c1], preferred_element_type=F32)

    piece = 2 * HEAD_DIM if tn % (2 * HEAD_DIM) == 0 else HEAD_DIM
    gated = functools.reduce(jnp.logical_or, [(j >= j0) & (j < j1) for mode, j0, j1 in modes if mode == "sigmoid"],
                             jnp.bool_(False))

    @pl.when(gated)
    def _():
        for c0 in range(0, tn, piece):
            o_ref[:, c0:c0 + piece] = jax.nn.sigmoid(tile(c0, c0 + piece)).astype(BF16)

    @pl.when(jnp.logical_not(gated))
    def _():
        acc_scr[...] = tile(0, tn)

    for mode, j0, j1 in modes:
        if mode == "sigmoid":
            continue

        @pl.when((j >= j0) & (j < j1))
        def _(mode=mode):
            if mode == "plain":
                o_ref[...] = acc_scr[...].astype(BF16)
                return
            rope = mode in ("q", "k")
            w = qw_ref[...] if mode == "q" else kw_ref[...]
            scale = HEAD_DIM ** -0.5 * math.log2(math.e) if mode == "q" else None
            for h in range(tn // HEAD_DIM):
                sl = slice(h * HEAD_DIM, (h + 1) * HEAD_DIM)
                a = _head_norm_rope(acc_scr[:, sl], w, cos_ref[...] if rope else None,
                                    sin_ref[...] if rope else None, scale)
                o_ref[:, sl] = a.astype(BF16)


def _inproj(x2d, shift, scale, norm_w, w_in, q_norm_w, k_norm_w, cos_t, sin_t, modes, seq, tm, tn):
    m, d = x2d.shape
    n = w_in.shape[1]
    per_b = seq // tm
    pos_blocks = cos_t.shape[0] // tm
    kern = functools.partial(_inproj_kernel, modes=modes, tn=tn)
    return pl.pallas_call(
        kern,
        grid=(m // tm, n // tn),
        in_specs=[pl.BlockSpec((tm, d), lambda i, j: (i, 0)),
                  pl.BlockSpec((None, 1, d), lambda i, j: (i // per_b, 0, 0)),
                  pl.BlockSpec((None, 1, d), lambda i, j: (i // per_b, 0, 0)),
                  pl.BlockSpec((1, d), lambda i, j: (0, 0)),
                  pl.BlockSpec((d, tn), lambda i, j: (0, j)),
                  pl.BlockSpec((1, HEAD_DIM), lambda i, j: (0, 0)),
                  pl.BlockSpec((1, HEAD_DIM), lambda i, j: (0, 0)),
                  pl.BlockSpec((tm, HEAD_DIM), lambda i, j: (i % pos_blocks, 0)),
                  pl.BlockSpec((tm, HEAD_DIM), lambda i, j: (i % pos_blocks, 0))],
        out_specs=pl.BlockSpec((tm, tn), lambda i, j: (i, j)),
        out_shape=jax.ShapeDtypeStruct((m, n), BF16),
        scratch_shapes=[pltpu.VMEM((tm, d), BF16), pltpu.VMEM((d, tn), BF16), pltpu.VMEM((tm, tn), F32)],
        compiler_params=_cparams("arbitrary", "arbitrary"),
        name="inproj",
    )(x2d, shift, scale, norm_w.reshape(1, d), w_in, q_norm_w.reshape(1, HEAD_DIM),
      k_norm_w.reshape(1, HEAD_DIM), cos_t, sin_t)


def _rope_tables(seq):
    rows = seq // GRID_W
    row = jnp.repeat(jnp.arange(rows), GRID_W)
    col = jnp.tile(jnp.arange(GRID_W), rows)
    pos = jnp.stack([row, col], axis=-1).astype(F32)
    inv_freq = ROPE_THETA ** (-jnp.arange(ROPE_FREQS, dtype=F32) / ROPE_FREQS)
    ang = pos[:, :, None] * inv_freq
    cos, sin = jnp.cos(ang), jnp.sin(ang)
    cos_t = jnp.concatenate([cos[:, 0], cos[:, 0], cos[:, 1], cos[:, 1]], axis=-1)
    sin_t = jnp.concatenate([-sin[:, 0], sin[:, 0], -sin[:, 1], sin[:, 1]], axis=-1)
    return cos_t, sin_t


def _attn_kernel(q_ref, k_ref, v_ref, kc_ref, vc_ref, o_ref, vt_scr, vct_scr, *, tq, tk, n_chunks):
    @pl.when(pl.program_id(2) == 0)
    def _():
        for c in range(n_chunks):
            vt_scr[c] = jnp.transpose(v_ref[c * tk:(c + 1) * tk, :].astype(F32)).astype(BF16)
        vct_scr[...] = jnp.transpose(vc_ref[...].astype(F32)).astype(BF16)

    q = jnp.concatenate([q_ref[:, h * HEAD_DIM:(h + 1) * HEAD_DIM] for h in range(Q_GROUP)], axis=0)
    qt = jnp.transpose(q.astype(F32)).astype(BF16)
    m_cols = Q_GROUP * tq

    def scores(kb):
        return jnp.dot(kb, qt, preferred_element_type=F32)

    def update(st, vtb, carry):
        m, l, acc = carry
        m_new = jnp.maximum(m, jnp.max(st, axis=0, keepdims=True))
        alpha = jnp.exp2(m - m_new)
        pt = jnp.exp2(st - m_new)
        l = alpha * l + jnp.sum(pt, axis=0, keepdims=True)
        acc = alpha * acc + jnp.dot(vtb, pt.astype(BF16), preferred_element_type=F32)
        return m_new, l, acc

    carry = (jnp.full((1, m_cols), -jnp.inf, F32), jnp.zeros((1, m_cols), F32),
             jnp.zeros((HEAD_DIM, m_cols), F32))
    st = scores(k_ref[0:tk, :])
    for c in range(n_chunks):
        st_next = scores(k_ref[(c + 1) * tk:(c + 2) * tk, :]) if c + 1 < n_chunks else scores(kc_ref[...])
        carry = update(st, vt_scr[c], carry)
        st = st_next
    _, l, acc = update(st, vct_scr[...], carry)
    o = jnp.transpose(acc / l)
    for h in range(Q_GROUP):
        o_ref[:, h * HEAD_DIM:(h + 1) * HEAD_DIM] = o[h * tq:(h + 1) * tq].astype(BF16)


def _attention(proj, kvc, batch, seq, n_ctx, n_kv, q_off, k_off, v_off, tq, tk):
    gw = Q_GROUP * HEAD_DIM
    per_b = seq // tq
    n_chunks = seq // tk
    kern = functools.partial(_attn_kernel, tq=tq, tk=tk, n_chunks=n_chunks)
    return pl.pallas_call(
        kern,
        grid=(batch, n_kv, per_b),
        in_specs=[pl.BlockSpec((tq, gw), lambda b, g, i: (b * per_b + i, q_off // gw + g)),
                  pl.BlockSpec((seq, HEAD_DIM), lambda b, g, i: (b, k_off // HEAD_DIM + g)),
                  pl.BlockSpec((seq, HEAD_DIM), lambda b, g, i: (b, v_off // HEAD_DIM + g)),
                  pl.BlockSpec((n_ctx, HEAD_DIM), lambda b, g, i: (b, g)),
                  pl.BlockSpec((n_ctx, HEAD_DIM), lambda b, g, i: (b, n_kv + g))],
        out_specs=pl.BlockSpec((tq, gw), lambda b, g, i: (b * per_b + i, g)),
        out_shape=jax.ShapeDtypeStruct((batch * seq, n_kv * gw), BF16),
        scratch_shapes=[pltpu.VMEM((n_chunks, HEAD_DIM, tk), BF16), pltpu.VMEM((HEAD_DIM, n_ctx), BF16)],
        compiler_params=_cparams("arbitrary", "arbitrary", "arbitrary"),
        name="attention",
    )(proj, proj, proj, kvc, kvc)


def _dft_mats(n_rows, n_cols):
    def cs(k, n):
        ang = (2.0 * math.pi / n) * (k % n).astype(F32)
        return jnp.cos(ang), jnp.sin(ang)

    ch = jnp.arange(F_GROUP_DIM)
    cc, sc = cs(ch[:, None] * ch[None, :], F_GROUP_DIM)
    mat_a = jnp.concatenate([cc, -sc], axis=1).astype(BF16)
    r = jnp.arange(n_rows)
    cr, sr = cs(r[:, None] * r[None, :], n_rows)
    mat_1 = jnp.concatenate([jnp.concatenate([cr, sr], axis=1),
                             jnp.concatenate([-sr, cr], axis=1)], axis=0).astype(BF16)
    c = jnp.arange(n_cols)
    c2, s2 = cs(c[:, None] * c[None, :], n_cols)
    mat_2 = jnp.concatenate([c2, s2], axis=1).astype(BF16)
    tr, ts = cs(r[:, None] * c[None, :], n_rows * n_cols)
    tw_r = jnp.repeat(tr, LANE, axis=1)
    tw_i = jnp.repeat(-ts, LANE, axis=1)
    return mat_a, mat_1, mat_2, tw_r, tw_i


def _fourier_a_kernel(u_ref, m_ref, vr_ref, vi_ref, *, groups):
    for g in range(groups):
        sl = slice(g * F_GROUP_DIM, (g + 1) * F_GROUP_DIM)
        r = jnp.dot(u_ref[:, sl], m_ref[...], preferred_element_type=F32)
        vr_ref[:, sl] = r[:, :F_GROUP_DIM].astype(BF16)
        vi_ref[:, sl] = r[:, F_GROUP_DIM:].astype(BF16)


def _fourier_1_kernel(ar_ref, ai_ref, m_ref, twr_ref, twi_ref, pr_ref, pi_ref, *, nseg, fw, n_rows):
    rhs = jnp.concatenate([ar_ref[...], ai_ref[...]], axis=0)
    z = jnp.dot(m_ref[...], rhs, preferred_element_type=F32)
    reps = fw // LANE
    for s in range(nseg):
        zr = z[:n_rows, s * fw:(s + 1) * fw]
        zi = z[n_rows:, s * fw:(s + 1) * fw]
        tr = jnp.tile(twr_ref[:, s * LANE:(s + 1) * LANE], (1, reps))
        ti = jnp.tile(twi_ref[:, s * LANE:(s + 1) * LANE], (1, reps))
        pr_ref[s] = (zr * tr - zi * ti).astype(BF16)
        pi_ref[s] = (zr * ti + zi * tr).astype(BF16)


def _fourier_2_kernel(ar_ref, ai_ref, m_ref, o_ref, *, norm):
    rhs = jnp.concatenate([ar_ref[...], ai_ref[...]], axis=0)
    o_ref[...] = (jnp.dot(m_ref[...], rhs, preferred_element_type=F32) * norm).astype(BF16)


def _fourier_mix(proj, batch, seq, fw):
    n_cols = GRID_W
    n_rows = seq // GRID_W
    groups = fw // F_GROUP_DIM
    mat_a, mat_1, mat_2, tw_r, tw_i = _dft_mats(n_rows, n_cols)
    m = batch * seq
    tm = min(seq, 1024)
    vr, vi = pl.pallas_call(
        functools.partial(_fourier_a_kernel, groups=groups),
        grid=(m // tm,),
        in_specs=[pl.BlockSpec((tm, fw), lambda i: (i, 0)),
                  pl.BlockSpec((F_GROUP_DIM, 2 * F_GROUP_DIM), lambda i: (0, 0))],
        out_specs=[pl.BlockSpec((tm, fw), lambda i: (i, 0))] * 2,
        out_shape=[jax.ShapeDtypeStruct((m, fw), BF16)] * 2,
        compiler_params=_cparams("arbitrary"),
        name="fourier_channels",
    )(proj, mat_a)
    wide = n_cols * fw
    vr = vr.reshape(batch * n_rows, wide)
    vi = vi.reshape(batch * n_rows, wide)
    nseg = min(n_cols, 4)
    pr, pi = pl.pallas_call(
        functools.partial(_fourier_1_kernel, nseg=nseg, fw=fw, n_rows=n_rows),
        grid=(batch, n_cols // nseg),
        in_specs=[pl.BlockSpec((n_rows, nseg * fw), lambda b, j: (b, j)),
                  pl.BlockSpec((n_rows, nseg * fw), lambda b, j: (b, j)),
                  pl.BlockSpec((2 * n_rows, 2 * n_rows), lambda b, j: (0, 0)),
                  pl.BlockSpec((n_rows, nseg * LANE), lambda b, j: (0, j)),
                  pl.BlockSpec((n_rows, nseg * LANE), lambda b, j: (0, j))],
        out_specs=[pl.BlockSpec((nseg, n_rows, fw), lambda b, j: (b * (n_cols // nseg) + j, 0, 0))] * 2,
        out_shape=[jax.ShapeDtypeStruct((batch * n_cols, n_rows, fw), BF16)] * 2,
        compiler_params=_cparams("arbitrary", "arbitrary"),
        name="fourier_rows",
    )(vr, vi, mat_1, tw_r, tw_i)
    wide2 = n_rows * fw
    pr = pr.reshape(batch * n_cols, wide2)
    pi = pi.reshape(batch * n_cols, wide2)
    tc = min(wide2, 8192)
    norm = 1.0 / math.sqrt(seq * F_GROUP_DIM)
    yf = pl.pallas_call(
        functools.partial(_fourier_2_kernel, norm=norm),
        grid=(batch, wide2 // tc),
        in_specs=[pl.BlockSpec((n_cols, tc), lambda b, j: (b, j)),
                  pl.BlockSpec((n_cols, tc), lambda b, j: (b, j)),
                  pl.BlockSpec((n_cols, 2 * n_cols), lambda b, j: (0, 0))],
        out_specs=pl.BlockSpec((n_cols, tc), lambda b, j: (b, j)),
        out_shape=jax.ShapeDtypeStruct((batch * n_cols, wide2), BF16),
        compiler_params=_cparams("arbitrary", "arbitrary"),
        name="fourier_cols",
    )(pr, pi, mat_2)
    return yf.reshape(m, fw)


def _merge_kernel(yf_ref, ya_ref, wfo_ref, wao_ref, gf_ref, ga_ref, o_ref):
    a = jnp.dot(yf_ref[...], wfo_ref[...], preferred_element_type=F32)
    b = jnp.dot(ya_ref[...], wao_ref[...], preferred_element_type=F32)
    o_ref[...] = (gf_ref[...].astype(F32) * a + ga_ref[...].astype(F32) * b).astype(BF16)


def _merge(yf, ya, wfo, wao, proj, gf_off, ga_off, tm, tn):
    m, fw = yf.shape
    aw = ya.shape[1]
    d = wfo.shape[1]
    return pl.pallas_call(
        _merge_kernel,
        grid=(m // tm, d // tn),
        in_specs=[pl.BlockSpec((tm, fw), lambda i, j: (i, 0)),
                  pl.BlockSpec((tm, aw), lambda i, j: (i, 0)),
                  pl.BlockSpec((fw, tn), lambda i, j: (0, j)),
                  pl.BlockSpec((aw, tn), lambda i, j: (0, j)),
                  pl.BlockSpec((tm, tn), lambda i, j: (i, gf_off // tn + j)),
                  pl.BlockSpec((tm, tn), lambda i, j: (i, ga_off // tn + j))],
        out_specs=pl.BlockSpec((tm, tn), lambda i, j: (i, j)),
        out_shape=jax.ShapeDtypeStruct((m, d), BF16),
        compiler_params=_cparams("arbitrary", "arbitrary"),
        name="merge",
    )(yf, ya, wfo, wao, proj, proj)


def _route(logits_t, bias, n_exp):
    gsz = n_exp // N_EXPERT_GROUPS
    tm = logits_t.shape[1]
    neg = -jnp.inf
    scores = jax.nn.sigmoid(logits_t)
    biased = scores + bias
    io_g = lax.broadcasted_iota(I32, (gsz, tm), 0).astype(F32)
    gs = []
    for g in range(N_EXPERT_GROUPS):
        grp = biased[g * gsz:(g + 1) * gsz, :]
        m1 = jnp.max(grp, axis=0, keepdims=True)
        i1 = jnp.min(jnp.where(grp == m1, io_g, float(gsz)), axis=0, keepdims=True)
        m2 = jnp.max(jnp.where(io_g == i1, neg, grp), axis=0, keepdims=True)
        gs.append(m1 + m2)
    cur = jnp.concatenate(gs, axis=0)
    io_n = lax.broadcasted_iota(I32, (N_EXPERT_GROUPS, tm), 0).astype(F32)
    sel = jnp.zeros((N_EXPERT_GROUPS, tm), F32)
    for _ in range(TOPK_GROUPS):
        mx = jnp.max(cur, axis=0, keepdims=True)
        ix = jnp.min(jnp.where(cur == mx, io_n, float(N_EXPERT_GROUPS)), axis=0, keepdims=True)
        hit = io_n == ix
        sel = jnp.where(hit, 1.0, sel)
        cur = jnp.where(hit, neg, cur)
    cur = jnp.concatenate([jnp.where(sel[g:g + 1, :] > 0.0, biased[g * gsz:(g + 1) * gsz, :], neg)
                           for g in range(N_EXPERT_GROUPS)], axis=0)
    io_e = lax.broadcasted_iota(I32, (n_exp, tm), 0).astype(F32)
    eidx, wts = [], []
    for _ in range(TOP_K):
        mx = jnp.max(cur, axis=0, keepdims=True)
        ix = jnp.min(jnp.where(cur == mx, io_e, float(n_exp)), axis=0, keepdims=True)
        hit = io_e == ix
        wts.append(jnp.sum(jnp.where(hit, scores, 0.0), axis=0, keepdims=True))
        eidx.append(ix)
        cur = jnp.where(hit, neg, cur)
    eidx = jnp.concatenate(eidx, axis=0).astype(I32)
    wts = jnp.concatenate(wts, axis=0)
    wts = wts / jnp.sum(wts, axis=0, keepdims=True) * ROUTED_SCALE
    return eidx, wts


def _outproj_kernel(y_ref, wo_ref, x_ref, g1_ref, nw_ref, sh_ref, sc_ref, rw_ref, rb_ref,
                    x1_ref, h2_ref, h2p_ref, eidx_ref, wts_ref, cnt_ref, carry_scr, *, n_exp):
    i = pl.program_id(0)
    tm, d = x_ref.shape

    @pl.when(i == 0)
    def _():
        carry_scr[...] = jnp.zeros_like(carry_scr)

    x1 = x_ref[...] + g1_ref[...] * jnp.dot(y_ref[...], wo_ref[...], preferred_element_type=F32)
    x1_ref[...] = x1
    ms = jnp.mean(x1 * x1, axis=-1, keepdims=True)
    h2 = (x1 * lax.rsqrt(ms + EPS) * nw_ref[...]) * (1.0 + sc_ref[...]) + sh_ref[...]
    h2b = h2.astype(BF16)
    h2_ref[...] = h2b
    n_sub = d // LANE
    for s in range(n_sub):
        h2p_ref[pl.ds(s, tm, stride=n_sub), :] = h2[:, s * LANE:(s + 1) * LANE]
    h_hi = h2b
    h_lo = (h2 - h_hi.astype(F32)).astype(BF16)
    part = (jnp.dot(h_hi, rw_ref[...], preferred_element_type=F32)
            + jnp.dot(h_lo, rw_ref[...], preferred_element_type=F32))
    logits = part[:, :LANE] + part[:, LANE:]
    logits_t = jnp.transpose(logits)[:n_exp, :]
    eidx, wts = _route(logits_t, rb_ref[...], n_exp)
    eidx_ref[...] = eidx
    wts_ref[...] = wts
    io_e = lax.broadcasted_iota(I32, (n_exp, tm), 0)
    onehot = jnp.zeros((n_exp, tm), F32)
    for k in range(TOP_K):
        onehot = onehot + jnp.where(io_e == eidx[k:k + 1, :], 1.0, 0.0)
    carry_scr[...] = carry_scr[...] + jnp.sum(onehot, axis=1, keepdims=True)
    cnt_ref[...] = carry_scr[...].astype(I32)


def _outproj_route(y, wo, x2d, g1, norm_w, shift, scale, rw_pad, rb, seq, n_exp, tm):
    m, d = x2d.shape
    per_b = seq // tm
    n_sub = d // LANE
    bspec = pl.BlockSpec((None, 1, d), lambda i: (i // per_b, 0, 0))
    row = pl.BlockSpec((tm, d), lambda i: (i, 0))
    tok = pl.BlockSpec((TOP_K, tm), lambda i: (0, i))
    return pl.pallas_call(
        functools.partial(_outproj_kernel, n_exp=n_exp),
        grid=(m // tm,),
        in_specs=[row,
                  pl.BlockSpec((d, d), lambda i: (0, 0), pipeline_mode=pl.Buffered(1)),
                  row, bspec,
                  pl.BlockSpec((1, d), lambda i: (0, 0)),
                  bspec, bspec,
                  pl.BlockSpec((d, 2 * LANE), lambda i: (0, 0)),
                  pl.BlockSpec((n_exp, 1), lambda i: (0, 0))],
        out_specs=[row, row,
                   pl.BlockSpec((tm * n_sub, LANE), lambda i: (i, 0)),
                   tok, tok,
                   pl.BlockSpec((n_exp, LANE), lambda i: (0, 0))],
        out_shape=[jax.ShapeDtypeStruct((m, d), F32),
                   jax.ShapeDtypeStruct((m, d), BF16),
                   jax.ShapeDtypeStruct((m * n_sub, LANE), F32),
                   jax.ShapeDtypeStruct((TOP_K, m), I32),
                   jax.ShapeDtypeStruct((TOP_K, m), F32),
                   jax.ShapeDtypeStruct((n_exp, LANE), I32)],
        scratch_shapes=[pltpu.VMEM((n_exp, LANE), F32)],
        compiler_params=_cparams("arbitrary"),
        name="outproj_route",
    )(y, wo, x2d, g1, norm_w.reshape(1, d), shift, scale, rw_pad, rb.reshape(n_exp, 1))


def _expert_kernel(be_ref, bwin_ref, brem_ref, bval_ref, nact_ref, prev_win, cur_win, nxt_win,
                   h2p_hbm, wg_ref, wu_ref, wd_ref, ytk_hbm,
                   wg_s, wu_s, wd_s, xbuf0, xbuf1, ybuf0, ybuf1, sem_g, sem_s, sem_z,
                   *, n_sub, n_tok, n_blocks, trash_base):
    i = pl.program_id(0)
    nact = nact_ref[0]
    bm = EXPERT_BLOCK
    xbufs = (xbuf0, xbuf1)
    ybufs = (ybuf0, ybuf1)
    unroll = 8

    def token_of(f):
        return f & (n_tok - 1) if n_tok & (n_tok - 1) == 0 else lax.rem(f, n_tok)

    def gather_copy(tok, r, par):
        return pltpu.make_async_copy(h2p_hbm.at[pl.ds(pl.multiple_of(tok * n_sub, n_sub), n_sub), :],
                                     xbufs[par].at[pl.ds(pl.multiple_of(r * n_sub, n_sub), n_sub), :],
                                     sem_g.at[par])

    def scatter_copy(r, row, par):
        return pltpu.make_async_copy(ybufs[par].at[pl.ds(pl.multiple_of(r * n_sub, n_sub), n_sub), :],
                                     ytk_hbm.at[pl.ds(pl.multiple_of(row * n_sub, n_sub), n_sub), :],
                                     sem_s.at[par])

    def wait_gathers(par):
        @pl.loop(0, bm // unroll)
        def _(q):
            for u in range(unroll):
                gather_copy(0, 0, par).wait()

    def wait_scatters(par):
        @pl.loop(0, bm // unroll)
        def _(q):
            for u in range(unroll):
                scatter_copy(0, 0, par).wait()

    def scatter_row(win_ref, rem, valid, r, par):
        row = jnp.where(r < valid, win_ref[0, rem + r], trash_base + par * bm + r)
        return scatter_copy(r, row, par)

    @pl.when(i == 0)
    def _():
        ybuf0[...] = jnp.zeros_like(ybuf0)
        ybuf1[...] = jnp.zeros_like(ybuf1)
        for par in range(2):
            start = (trash_base + par * bm) * n_sub
            cp = pltpu.make_async_copy(ybuf0, ytk_hbm.at[pl.ds(start, bm * n_sub), :], sem_z)
            cp.start()
            cp.wait()
        rem0 = brem_ref[0]

        @pl.loop(0, bm // unroll)
        def _(q):
            for u in range(unroll):
                r = q * unroll + u
                gather_copy(token_of(cur_win[0, rem0 + r]), r, 0).start(priority=u % 2)

    def block_step(par):
        wait_gathers(par)

        @pl.when(i >= 1)
        def _():
            wait_scatters(par)

        @pl.when((i == 0) | (be_ref[i] != be_ref[jnp.maximum(i - 1, 0)]))
        def _():
            wg_s[...] = wg_ref[...].astype(BF16)
            wu_s[...] = wu_ref[...].astype(BF16)
            wd_s[...] = wd_ref[...].astype(BF16)

        nxt_rem = brem_ref[jnp.minimum(i + 1, n_blocks - 1)]
        prv = jnp.maximum(i - 1, 0)
        prv_rem = brem_ref[prv]
        prv_valid = jnp.where(i >= 1, bval_ref[prv], 0)
        for r in range(bm):
            gather_copy(token_of(nxt_win[0, nxt_rem + r]), r, 1 - par).start(priority=r % 2)
        for r in range(bm):
            scatter_row(prev_win, prv_rem, prv_valid, r, 1 - par).start(priority=r % 2)

        x = jnp.concatenate([xbufs[par][pl.ds(s, bm, stride=n_sub), :] for s in range(n_sub)],
                            axis=1).astype(BF16)
        g = jnp.dot(x, wg_s[...], preferred_element_type=F32)
        u = jnp.dot(x, wu_s[...], preferred_element_type=F32)
        y = jnp.dot((_silu(g) * u).astype(BF16), wd_s[...], preferred_element_type=F32)
        for s in range(n_sub):
            ybufs[par][pl.ds(s, bm, stride=n_sub), :] = y[:, s * LANE:(s + 1) * LANE]

        @pl.when(i == nact - 1)
        def _():
            wait_gathers(1 - par)
            wait_scatters(1 - par)
            rem = brem_ref[i]
            valid = bval_ref[i]

            @pl.loop(0, bm // unroll)
            def _(q):
                for u_ in range(unroll):
                    scatter_row(cur_win, rem, valid, q * unroll + u_, par).start(priority=u_ % 2)

            wait_scatters(par)

    for par in range(2):
        @pl.when((i < nact) & (lax.rem(i, 2) == par))
        def _(par=par):
            block_step(par)


def _experts(h2p, order, blk_expert, blk_off, blk_valid, n_active, exp_gate, exp_up, exp_down, n_blocks, n_tok):
    n_exp, d, ff = exp_gate.shape
    n_sub = d // LANE
    bm = EXPERT_BLOCK
    n_win = order.shape[0] // bm
    pieces = order.reshape(n_win, 1, bm)
    windows = jnp.concatenate([pieces, jnp.roll(pieces, -1, axis=0)], axis=-1)
    blk_win = blk_off // bm
    blk_rem = blk_off - blk_win * bm
    trash_base = TOP_K * n_tok

    def win(step_shift):
        def index_map(i, be, bwin, *_):
            return (bwin[jnp.clip(i + step_shift, 0, n_blocks - 1)], 0, 0)
        return pl.BlockSpec((None, 1, 2 * bm), index_map, memory_space=pltpu.SMEM)

    row_buf = pltpu.VMEM((bm * n_sub, LANE), F32)
    grid_spec = pltpu.PrefetchScalarGridSpec(
        num_scalar_prefetch=5,
        grid=(n_blocks,),
        in_specs=[win(-1), win(0), win(1),
                  pl.BlockSpec(memory_space=pl.ANY),
                  pl.BlockSpec((None, d, ff), lambda i, be, *_: (be[i], 0, 0)),
                  pl.BlockSpec((None, d, ff), lambda i, be, *_: (be[i], 0, 0)),
                  pl.BlockSpec((None, ff, d), lambda i, be, *_: (be[i], 0, 0))],
        out_specs=pl.BlockSpec(memory_space=pl.ANY),
        scratch_shapes=[pltpu.VMEM((d, ff), BF16), pltpu.VMEM((d, ff), BF16), pltpu.VMEM((ff, d), BF16),
                        row_buf, row_buf, row_buf, row_buf,
                        pltpu.SemaphoreType.DMA((2,)), pltpu.SemaphoreType.DMA((2,)),
                        pltpu.SemaphoreType.DMA(())],
    )
    return pl.pallas_call(
        functools.partial(_expert_kernel, n_sub=n_sub, n_tok=n_tok, n_blocks=n_blocks, trash_base=trash_base),
        grid_spec=grid_spec,
        out_shape=jax.ShapeDtypeStruct(((trash_base + 2 * bm) * n_sub, LANE), F32),
        compiler_params=_cparams("arbitrary"),
        name="experts",
    )(blk_expert, blk_win, blk_rem, blk_valid, n_active, windows, windows, windows, h2p, exp_gate, exp_up,
      exp_down)


def _combine_kernel(*refs, tmc, n_out):
    yk_refs = refs[:TOP_K]
    wt_ref, h2_ref, x1_ref, g2_ref, sg_ref, su_ref, sd_ref, o_ref, acc_scr = refs[TOP_K:]
    h = h2_ref[...]
    g = jnp.dot(h, sg_ref[...], preferred_element_type=F32)
    u = jnp.dot(h, su_ref[...], preferred_element_type=F32)
    shared = jnp.dot((_silu(g) * u).astype(BF16), sd_ref[...], preferred_element_type=F32)
    unroll = 4

    @pl.loop(0, tmc // unroll)
    def _(q):
        for j in range(unroll):
            t = q * unroll + j
            rows = pl.ds(pl.multiple_of(t * n_out, n_out), n_out)
            acc = wt_ref[0, t] * yk_refs[0][rows, :]
            for k in range(1, TOP_K):
                acc = acc + wt_ref[k, t] * yk_refs[k][rows, :]
            acc_scr[rows, :] = acc

    routed = jnp.concatenate([acc_scr[pl.ds(s, tmc, stride=n_out), :] for s in range(n_out)], axis=1)
    o_ref[...] = x1_ref[...] + g2_ref[...] * (routed + shared)


def _combine(ytk, wts, h2, x1, g2, sg, su, sd, seq, tmc):
    m, d = x1.shape
    ff = sg.shape[1]
    n_out = d // LANE
    per_b = seq // tmc
    tiles = m // tmc
    row = pl.BlockSpec((tmc, d), lambda i: (i, 0))
    yk_specs = [pl.BlockSpec((tmc * n_out, LANE), lambda i, k=k: (k * tiles + i, 0)) for k in range(TOP_K)]
    return pl.pallas_call(
        functools.partial(_combine_kernel, tmc=tmc, n_out=n_out),
        grid=(tiles,),
        in_specs=yk_specs + [pl.BlockSpec((TOP_K, tmc), lambda i: (0, i), memory_space=pltpu.SMEM),
                             row, row,
                             pl.BlockSpec((None, 1, d), lambda i: (i // per_b, 0, 0)),
                             pl.BlockSpec((d, ff), lambda i: (0, 0)),
                             pl.BlockSpec((d, ff), lambda i: (0, 0)),
                             pl.BlockSpec((ff, d), lambda i: (0, 0))],
        out_specs=row,
        out_shape=jax.ShapeDtypeStruct((m, d), F32),
        scratch_shapes=[pltpu.VMEM((tmc * n_out, LANE), F32)],
        compiler_params=_cparams("arbitrary"),
        name="combine",
    )(*([ytk] * TOP_K), wts, h2, x1, g2, sg, su, sd)


def kernel(x, c, ctx, c_ctx, mod_w, mod_b, norm1_w, w_in, q_norm_w, k_norm_w, w_fourier_out, w_attn_out, w_out,
           norm2_w, router_w, router_b, exp_gate, exp_up, exp_down, shared_gate, shared_up, shared_down):
    batch, seq, d = x.shape
    n_ctx = ctx.shape[1]
    assert mod_w.shape[0] == 1, "single-layer block only"
    assert batch + 1 <= 8 and seq % GRID_W == 0
    fw = w_fourier_out.shape[1]
    att = w_attn_out.shape[1]
    in_cols = w_in.shape[2]
    kvw = (in_cols - fw - att - 2 * d) // 2
    n_kv = kvw // HEAD_DIM
    n_exp = router_w.shape[2]
    assert att == n_kv * Q_GROUP * HEAD_DIM and n_exp <= LANE
    q_off, k_off = fw, fw + att
    v_off = k_off + kvw
    gf_off = v_off + kvw
    ga_off = gf_off + d
    m = batch * seq
    assert q_off % (Q_GROUP * HEAD_DIM) == 0 and kvw % HEAD_DIM == 0 and fw % F_GROUP_DIM == 0
    assert gf_off % min(512, d) == 0 and d % (2 * LANE) == 0

    cond = jnp.concatenate([c, c_ctx[None], jnp.zeros((8 - batch - 1, d), F32)], axis=0)
    mod = _adaln(cond, mod_w[0], mod_b[0])
    sh1, sc1, g1, sh2, sc2, g2 = [mod[:batch, j * d:(j + 1) * d].reshape(batch, 1, d) for j in range(6)]
    csh1 = jnp.broadcast_to(mod[batch, 0:d], (batch, 1, d))
    csc1 = jnp.broadcast_to(mod[batch, d:2 * d], (batch, 1, d))

    cos_t, sin_t = _rope_tables(seq)
    tn = min(512, kvw)
    tm = min(1024, seq)
    modes = (("plain", 0, q_off // tn), ("q", q_off // tn, k_off // tn), ("k", k_off // tn, v_off // tn),
             ("plain", v_off // tn, gf_off // tn), ("sigmoid", gf_off // tn, in_cols // tn))
    proj = _inproj(x.reshape(m, d), sh1, sc1, norm1_w[0], w_in[0], q_norm_w[0], k_norm_w[0], cos_t, sin_t,
                   modes, seq, tm, tn)
    cmodes = (("k_norope", 0, kvw // tn), ("plain", kvw // tn, 2 * kvw // tn))
    tmc_ctx = min(n_ctx, 256)
    kvc = _inproj(ctx.reshape(batch * n_ctx, d), csh1, csc1, norm1_w[0], w_in[0][:, k_off:gf_off], q_norm_w[0],
                  k_norm_w[0], cos_t[:tmc_ctx], sin_t[:tmc_ctx], cmodes, n_ctx, tmc_ctx, tn)

    ya = _attention(proj, kvc, batch, seq, n_ctx, n_kv, q_off, k_off, v_off, tq=min(512, seq), tk=min(512, seq))
    yf = _fourier_mix(proj, batch, seq, fw)
    y = _merge(yf, ya, w_fourier_out[0].astype(BF16), w_attn_out[0].astype(BF16), proj, gf_off, ga_off,
               tm=min(1024, seq), tn=min(512, d))

    rw_pad = jnp.pad(router_w[0], ((0, 0), (0, LANE - n_exp)))
    rw_hi = rw_pad.astype(BF16)
    rw_pad = jnp.concatenate([rw_hi, (rw_pad - rw_hi.astype(F32)).astype(BF16)], axis=1)
    x1, h2, h2p, eidx, wts, cnt = _outproj_route(
        y, w_out[0].astype(BF16), x.reshape(m, d), g1, norm2_w[0], sh2, sc2, rw_pad, router_b[0], seq, n_exp,
        tm=min(512, seq))

    n_assign = m * TOP_K
    assert n_assign % EXPERT_BLOCK == 0
    _, order = lax.sort((eidx.reshape(-1), jnp.arange(n_assign, dtype=I32)), num_keys=1)
    counts = cnt[:, 0]
    starts = jnp.cumsum(counts) - counts
    nb_e = (counts + EXPERT_BLOCK - 1) // EXPERT_BLOCK
    cum_nb = jnp.cumsum(nb_e)
    n_blocks = n_assign // EXPERT_BLOCK + n_exp
    n_active = jnp.maximum(cum_nb[-1], 1).astype(I32)
    blk = jnp.minimum(jnp.arange(n_blocks, dtype=I32), n_active - 1)
    blk_expert = jnp.minimum(jnp.sum(blk[:, None] >= cum_nb[None, :], axis=1), n_exp - 1).astype(I32)
    is_e = blk_expert[:, None] == jnp.arange(n_exp, dtype=I32)[None, :]
    lookup = lambda table: jnp.sum(jnp.where(is_e, table[None, :], 0), axis=1)
    blk_j = blk - lookup(cum_nb - nb_e)
    blk_off = (lookup(starts) + blk_j * EXPERT_BLOCK).astype(I32)
    blk_valid = jnp.clip(lookup(counts) - blk_j * EXPERT_BLOCK, 0, EXPERT_BLOCK).astype(I32)

    ytk = _experts(h2p, order, blk_expert, blk_off, blk_valid, n_active.reshape(1), exp_gate[0], exp_up[0],
                   exp_down[0], n_blocks, m)
    out = _combine(ytk, wts, h2, x1, g2, shared_gate[0].astype(BF16), shared_up[0].astype(BF16),
                   shared_down[0].astype(BF16), seq, tmc=min(128, seq))
    return out.reshape(batch, seq, d)
```

```python
import functools
import math

import jax
import jax.numpy as jnp
from jax import lax
from jax.experimental import pallas as pl
from jax.experimental.pallas import tpu as pltpu

F32 = jnp.float32
BF16 = jnp.bfloat16
I32 = jnp.int32

GRID_W = 64
HEAD_DIM = 128
Q_GROUP = 4
ROPE_FREQS = HEAD_DIM // 4
ROPE_THETA = 10000.0
F_GROUP_DIM = 128
N_EXPERT_GROUPS = 8
TOPK_GROUPS = 4
TOP_K = 8
ROUTED_SCALE = 2.5
EPS = 1e-6
LANE = 128
EXPERT_BLOCK = 256
VMEM_LIMIT = 56 * 1024 * 1024


def _cparams(*sem):
    return pltpu.CompilerParams(dimension_semantics=sem, vmem_limit_bytes=VMEM_LIMIT)


def _silu(v):
    return v * jax.nn.sigmoid(v)


def _adaln_kernel(c_ref, w_ref, b_ref, o_ref):
    a = _silu(c_ref[...])
    o_ref[...] = jnp.dot(a.astype(BF16), w_ref[...].astype(BF16), preferred_element_type=F32) + b_ref[...]


def _adaln(cond_pad, mod_w, mod_b):
    d, n = mod_w.shape
    tn = min(n, 1024)
    return pl.pallas_call(
        _adaln_kernel,
        grid=(n // tn,),
        in_specs=[pl.BlockSpec((8, d), lambda j: (0, 0)),
                  pl.BlockSpec((d, tn), lambda j: (0, j)),
                  pl.BlockSpec((1, tn), lambda j: (0, j))],
        out_specs=pl.BlockSpec((8, tn), lambda j: (0, j)),
        out_shape=jax.ShapeDtypeStruct((8, n), F32),
        compiler_params=_cparams("arbitrary"),
        name="adaln",
    )(cond_pad, mod_w, mod_b.reshape(1, n))


def _head_norm_rope(a, w, cos, sin, scale):
    ms = jnp.mean(a * a, axis=-1, keepdims=True)
    a = a * lax.rsqrt(ms + EPS) * w
    if cos is not None:
        lane = lax.broadcasted_iota(I32, a.shape, 1)
        first = (lane % (2 * ROPE_FREQS)) < ROPE_FREQS
        partner = jnp.where(first, pltpu.roll(a, HEAD_DIM - ROPE_FREQS, 1), pltpu.roll(a, ROPE_FREQS, 1))
        a = a * cos + partner * sin
    if scale is not None:
        a = a * scale
    return a


def _inproj_kernel(x_ref, sh_ref, sc_ref, nw_ref, w_ref, qw_ref, kw_ref, cos_ref, sin_ref, o_ref, h_scr, wb_scr,
                   acc_scr, *, modes, tn):
    j = pl.program_id(1)

    @pl.when(j == 0)
    def _():
        x = x_ref[...]
        ms = jnp.mean(x * x, axis=-1, keepdims=True)
        y = x * lax.rsqrt(ms + EPS) * nw_ref[...]
        h_scr[...] = (y * (1.0 + sc_ref[...]) + sh_ref[...]).astype(BF16)

    wb_scr[...] = w_ref[...].astype(BF16)

    def tile(c0, c1):
        return jnp.dot(h_scr[...], wb_scr[:, c0:c1], preferred_element_type=F32)

    piece = 2 * HEAD_DIM if tn % (2 * HEAD_DIM) == 0 else HEAD_DIM
    gated = functools.reduce(jnp.logical_or, [(j >= j0) & (j < j1) for mode, j0, j1 in modes if mode == "sigmoid"],
                             jnp.bool_(False))

    @pl.when(gated)
    def _():
        for c0 in range(0, tn, piece):
            o_ref[:, c0:c0 + piece] = jax.nn.sigmoid(tile(c0, c0 + piece)).astype(BF16)

    @pl.when(jnp.logical_not(gated))
    def _():
        acc_scr[...] = tile(0, tn)

    for mode, j0, j1 in modes:
        if mode == "sigmoid":
            continue

        @pl.when((j >= j0) & (j < j1))
        def _(mode=mode):
            if mode == "plain":
                o_ref[...] = acc_scr[...].astype(BF16)
                return
            rope = mode in ("q", "k")
            w = qw_ref[...] if mode == "q" else kw_ref[...]
            scale = HEAD_DIM ** -0.5 * math.log2(math.e) if mode == "q" else None
            for h in range(tn // HEAD_DIM):
                sl = slice(h * HEAD_DIM, (h + 1) * HEAD_DIM)
                a = _head_norm_rope(acc_scr[:, sl], w, cos_ref[...] if rope else None,
                                    sin_ref[...] if rope else None, scale)
                o_ref[:, sl] = a.astype(BF16)


def _inproj(x2d, shift, scale, norm_w, w_in, q_norm_w, k_norm_w, cos_t, sin_t, modes, seq, tm, tn, col0=0,
            n_cols=None):
    m, d = x2d.shape
    n = w_in.shape[1] if n_cols is None else n_cols
    cb0 = col0 // tn
    per_b = seq // tm
    pos_blocks = cos_t.shape[0] // tm
    kern = functools.partial(_inproj_kernel, modes=modes, tn=tn)
    return pl.pallas_call(
        kern,
        grid=(m // tm, n // tn),
        in_specs=[pl.BlockSpec((tm, d), lambda i, j: (i, 0)),
                  pl.BlockSpec((None, 1, d), lambda i, j: (i // per_b, 0, 0)),
                  pl.BlockSpec((None, 1, d), lambda i, j: (i // per_b, 0, 0)),
                  pl.BlockSpec((1, d), lambda i, j: (0, 0)),
                  pl.BlockSpec((d, tn), lambda i, j: (0, cb0 + j)),
                  pl.BlockSpec((1, HEAD_DIM), lambda i, j: (0, 0)),
                  pl.BlockSpec((1, HEAD_DIM), lambda i, j: (0, 0)),
                  pl.BlockSpec((tm, HEAD_DIM), lambda i, j: (i % pos_blocks, 0)),
                  pl.BlockSpec((tm, HEAD_DIM), lambda i, j: (i % pos_blocks, 0))],
        out_specs=pl.BlockSpec((tm, tn), lambda i, j: (i, j)),
        out_shape=jax.ShapeDtypeStruct((m, n), BF16),
        scratch_shapes=[pltpu.VMEM((tm, d), BF16), pltpu.VMEM((d, tn), BF16), pltpu.VMEM((tm, tn), F32)],
        compiler_params=_cparams("arbitrary", "arbitrary"),
        name="inproj",
    )(x2d, shift, scale, norm_w.reshape(1, d), w_in, q_norm_w.reshape(1, HEAD_DIM),
      k_norm_w.reshape(1, HEAD_DIM), cos_t, sin_t)


def _rope_tables(seq):
    rows = seq // GRID_W
    row = jnp.repeat(jnp.arange(rows), GRID_W)
    col = jnp.tile(jnp.arange(GRID_W), rows)
    pos = jnp.stack([row, col], axis=-1).astype(F32)
    inv_freq = ROPE_THETA ** (-jnp.arange(ROPE_FREQS, dtype=F32) / ROPE_FREQS)
    ang = pos[:, :, None] * inv_freq
    cos, sin = jnp.cos(ang), jnp.sin(ang)
    cos_t = jnp.concatenate([cos[:, 0], cos[:, 0], cos[:, 1], cos[:, 1]], axis=-1)
    sin_t = jnp.concatenate([-sin[:, 0], sin[:, 0], -sin[:, 1], sin[:, 1]], axis=-1)
    return cos_t, sin_t


def _attn_kernel(q_ref, k_ref, v_ref, kc_ref, vc_ref, o_ref, vt_scr, vct_scr, *, tq, tk, n_chunks):
    @pl.when(pl.program_id(2) == 0)
    def _():
        for c in range(n_chunks):
            vt_scr[c] = jnp.transpose(v_ref[c * tk:(c + 1) * tk, :].astype(F32)).astype(BF16)
        vct_scr[...] = jnp.transpose(vc_ref[...].astype(F32)).astype(BF16)

    q = jnp.concatenate([q_ref[:, h * HEAD_DIM:(h + 1) * HEAD_DIM] for h in range(Q_GROUP)], axis=0)
    qt = jnp.transpose(q.astype(F32)).astype(BF16)
    m_cols = Q_GROUP * tq

    def scores(kb):
        return jnp.dot(kb, qt, preferred_element_type=F32)

    def update(st, vtb, carry):
        m, l, acc = carry
        m_new = jnp.maximum(m, jnp.max(st, axis=0, keepdims=True))
        alpha = jnp.exp2(m - m_new)
        pt = jnp.exp2(st - m_new)
        l = alpha * l + jnp.sum(pt, axis=0, keepdims=True)
        acc = alpha * acc + jnp.dot(vtb, pt.astype(BF16), preferred_element_type=F32)
        return m_new, l, acc

    carry = (jnp.full((1, m_cols), -jnp.inf, F32), jnp.zeros((1, m_cols), F32),
             jnp.zeros((HEAD_DIM, m_cols), F32))
    st = scores(k_ref[0:tk, :])
    for c in range(n_chunks):
        st_next = scores(k_ref[(c + 1) * tk:(c + 2) * tk, :]) if c + 1 < n_chunks else scores(kc_ref[...])
        carry = update(st, vt_scr[c], carry)
        st = st_next
    _, l, acc = update(st, vct_scr[...], carry)
    o = jnp.transpose(acc / l)
    for h in range(Q_GROUP):
        o_ref[:, h * HEAD_DIM:(h + 1) * HEAD_DIM] = o[h * tq:(h + 1) * tq].astype(BF16)


def _attention(proj, kvc, batch, seq, n_ctx, n_kv, q_off, k_off, v_off, tq, tk):
    gw = Q_GROUP * HEAD_DIM
    per_b = seq // tq
    n_chunks = seq // tk
    kern = functools.partial(_attn_kernel, tq=tq, tk=tk, n_chunks=n_chunks)
    return pl.pallas_call(
        kern,
        grid=(batch, n_kv, per_b),
        in_specs=[pl.BlockSpec((tq, gw), lambda b, g, i: (b * per_b + i, q_off // gw + g)),
                  pl.BlockSpec((seq, HEAD_DIM), lambda b, g, i: (b, k_off // HEAD_DIM + g)),
                  pl.BlockSpec((seq, HEAD_DIM), lambda b, g, i: (b, v_off // HEAD_DIM + g)),
                  pl.BlockSpec((n_ctx, HEAD_DIM), lambda b, g, i: (b, g)),
                  pl.BlockSpec((n_ctx, HEAD_DIM), lambda b, g, i: (b, n_kv + g))],
        out_specs=pl.BlockSpec((tq, gw), lambda b, g, i: (b * per_b + i, g)),
        out_shape=jax.ShapeDtypeStruct((batch * seq, n_kv * gw), BF16),
        scratch_shapes=[pltpu.VMEM((n_chunks, HEAD_DIM, tk), BF16), pltpu.VMEM((HEAD_DIM, n_ctx), BF16)],
        compiler_params=_cparams("arbitrary", "arbitrary", "arbitrary"),
        name="attention",
    )(proj, proj, proj, kvc, kvc)


def _dft_mats(n_rows, n_cols):
    def cs(k, n):
        ang = (2.0 * math.pi / n) * (k % n).astype(F32)
        return jnp.cos(ang), jnp.sin(ang)

    ch = jnp.arange(F_GROUP_DIM)
    cc, sc = cs(ch[:, None] * ch[None, :], F_GROUP_DIM)
    mat_a = jnp.concatenate([cc, -sc], axis=1).astype(BF16)
    r = jnp.arange(n_rows)
    cr, sr = cs(r[:, None] * r[None, :], n_rows)
    mat_1 = jnp.concatenate([jnp.concatenate([cr, sr], axis=1),
                             jnp.concatenate([-sr, cr], axis=1)], axis=0).astype(BF16)
    c = jnp.arange(n_cols)
    c2, s2 = cs(c[:, None] * c[None, :], n_cols)
    mat_2 = jnp.concatenate([c2, s2], axis=1).astype(BF16)
    tr, ts = cs(r[:, None] * c[None, :], n_rows * n_cols)
    tw_r = jnp.repeat(tr, LANE, axis=1)
    tw_i = jnp.repeat(-ts, LANE, axis=1)
    return mat_a, mat_1, mat_2, tw_r, tw_i


def _fourier_a_kernel(u_ref, m_ref, vr_ref, vi_ref, *, groups):
    for g in range(groups):
        sl = slice(g * F_GROUP_DIM, (g + 1) * F_GROUP_DIM)
        r = jnp.dot(u_ref[:, sl], m_ref[...], preferred_element_type=F32)
        vr_ref[:, sl] = r[:, :F_GROUP_DIM].astype(BF16)
        vi_ref[:, sl] = r[:, F_GROUP_DIM:].astype(BF16)


def _fourier_1_kernel(ar_ref, ai_ref, m_ref, twr_ref, twi_ref, pr_ref, pi_ref, *, nseg, fw, n_rows):
    rhs = jnp.concatenate([ar_ref[...], ai_ref[...]], axis=0)
    z = jnp.dot(m_ref[...], rhs, preferred_element_type=F32)
    reps = fw // LANE
    for s in range(nseg):
        zr = z[:n_rows, s * fw:(s + 1) * fw]
        zi = z[n_rows:, s * fw:(s + 1) * fw]
        tr = jnp.tile(twr_ref[:, s * LANE:(s + 1) * LANE], (1, reps))
        ti = jnp.tile(twi_ref[:, s * LANE:(s + 1) * LANE], (1, reps))
        pr_ref[s] = (zr * tr - zi * ti).astype(BF16)
        pi_ref[s] = (zr * ti + zi * tr).astype(BF16)


def _fourier_2_kernel(ar_ref, ai_ref, m_ref, o_ref, *, norm):
    rhs = jnp.concatenate([ar_ref[...], ai_ref[...]], axis=0)
    o_ref[...] = (jnp.dot(m_ref[...], rhs, preferred_element_type=F32) * norm).astype(BF16)


def _fourier_mix(proj, batch, seq, fw):
    n_cols = GRID_W
    n_rows = seq // GRID_W
    groups = fw // F_GROUP_DIM
    mat_a, mat_1, mat_2, tw_r, tw_i = _dft_mats(n_rows, n_cols)
    m = batch * seq
    tm = min(seq, 1024)
    vr, vi = pl.pallas_call(
        functools.partial(_fourier_a_kernel, groups=groups),
        grid=(m // tm,),
        in_specs=[pl.BlockSpec((tm, fw), lambda i: (i, 0)),
                  pl.BlockSpec((F_GROUP_DIM, 2 * F_GROUP_DIM), lambda i: (0, 0))],
        out_specs=[pl.BlockSpec((tm, fw), lambda i: (i, 0))] * 2,
        out_shape=[jax.ShapeDtypeStruct((m, fw), BF16)] * 2,
        compiler_params=_cparams("arbitrary"),
        name="fourier_channels",
    )(proj, mat_a)
    wide = n_cols * fw
    vr = vr.reshape(batch * n_rows, wide)
    vi = vi.reshape(batch * n_rows, wide)
    nseg = min(n_cols, 4)
    pr, pi = pl.pallas_call(
        functools.partial(_fourier_1_kernel, nseg=nseg, fw=fw, n_rows=n_rows),
        grid=(batch, n_cols // nseg),
        in_specs=[pl.BlockSpec((n_rows, nseg * fw), lambda b, j: (b, j)),
                  pl.BlockSpec((n_rows, nseg * fw), lambda b, j: (b, j)),
                  pl.BlockSpec((2 * n_rows, 2 * n_rows), lambda b, j: (0, 0)),
                  pl.BlockSpec((n_rows, nseg * LANE), lambda b, j: (0, j)),
                  pl.BlockSpec((n_rows, nseg * LANE), lambda b, j: (0, j))],
        out_specs=[pl.BlockSpec((nseg, n_rows, fw), lambda b, j: (b * (n_cols // nseg) + j, 0, 0))] * 2,
        out_shape=[jax.ShapeDtypeStruct((batch * n_cols, n_rows, fw), BF16)] * 2,
        compiler_params=_cparams("arbitrary", "arbitrary"),
        name="fourier_rows",
    )(vr, vi, mat_1, tw_r, tw_i)
    wide2 = n_rows * fw
    pr = pr.reshape(batch * n_cols, wide2)
    pi = pi.reshape(batch * n_cols, wide2)
    tc = min(wide2, 8192)
    norm = 1.0 / math.sqrt(seq * F_GROUP_DIM)
    yf = pl.pallas_call(
        functools.partial(_fourier_2_kernel, norm=norm),
        grid=(batch, wide2 // tc),
        in_specs=[pl.BlockSpec((n_cols, tc), lambda b, j: (b, j)),
                  pl.BlockSpec((n_cols, tc), lambda b, j: (b, j)),
                  pl.BlockSpec((n_cols, 2 * n_cols), lambda b, j: (0, 0))],
        out_specs=pl.BlockSpec((n_cols, tc), lambda b, j: (b, j)),
        out_shape=jax.ShapeDtypeStruct((batch * n_cols, wide2), BF16),
        compiler_params=_cparams("arbitrary", "arbitrary"),
        name="fourier_cols",
    )(pr, pi, mat_2)
    return yf.reshape(m, fw)


def _merge_kernel(yf_ref, ya_ref, wfo_ref, wao_ref, gf_ref, ga_ref, o_ref):
    a = jnp.dot(yf_ref[...], wfo_ref[...], preferred_element_type=F32)
    b = jnp.dot(ya_ref[...], wao_ref[...], preferred_element_type=F32)
    o_ref[...] = (gf_ref[...].astype(F32) * a + ga_ref[...].astype(F32) * b).astype(BF16)


def _merge(yf, ya, wfo, wao, proj, gf_off, ga_off, tm, tn):
    m, fw = yf.shape
    aw = ya.shape[1]
    d = wfo.shape[1]
    return pl.pallas_call(
        _merge_kernel,
        grid=(m // tm, d // tn),
        in_specs=[pl.BlockSpec((tm, fw), lambda i, j: (i, 0)),
                  pl.BlockSpec((tm, aw), lambda i, j: (i, 0)),
                  pl.BlockSpec((fw, tn), lambda i, j: (0, j)),
                  pl.BlockSpec((aw, tn), lambda i, j: (0, j)),
                  pl.BlockSpec((tm, tn), lambda i, j: (i, gf_off // tn + j)),
                  pl.BlockSpec((tm, tn), lambda i, j: (i, ga_off // tn + j))],
        out_specs=pl.BlockSpec((tm, tn), lambda i, j: (i, j)),
        out_shape=jax.ShapeDtypeStruct((m, d), BF16),
        compiler_params=_cparams("arbitrary", "arbitrary"),
        name="merge",
    )(yf, ya, wfo, wao, proj, proj)


def _route(logits_t, bias, n_exp):
    gsz = n_exp // N_EXPERT_GROUPS
    tm = logits_t.shape[1]
    neg = -jnp.inf
    scores = jax.nn.sigmoid(logits_t)
    biased = scores + bias
    io_g = lax.broadcasted_iota(I32, (gsz, tm), 0).astype(F32)
    gs = []
    for g in range(N_EXPERT_GROUPS):
        grp = biased[g * gsz:(g + 1) * gsz, :]
        m1 = jnp.max(grp, axis=0, keepdims=True)
        i1 = jnp.min(jnp.where(grp == m1, io_g, float(gsz)), axis=0, keepdims=True)
        m2 = jnp.max(jnp.where(io_g == i1, neg, grp), axis=0, keepdims=True)
        gs.append(m1 + m2)
    cur = jnp.concatenate(gs, axis=0)
    io_n = lax.broadcasted_iota(I32, (N_EXPERT_GROUPS, tm), 0).astype(F32)
    sel = jnp.zeros((N_EXPERT_GROUPS, tm), F32)
    for _ in range(TOPK_GROUPS):
        mx = jnp.max(cur, axis=0, keepdims=True)
        ix = jnp.min(jnp.where(cur == mx, io_n, float(N_EXPERT_GROUPS)), axis=0, keepdims=True)
        hit = io_n == ix
        sel = jnp.where(hit, 1.0, sel)
        cur = jnp.where(hit, neg, cur)
    cur = jnp.concatenate([jnp.where(sel[g:g + 1, :] > 0.0, biased[g * gsz:(g + 1) * gsz, :], neg)
                           for g in range(N_EXPERT_GROUPS)], axis=0)
    io_e = lax.broadcasted_iota(I32, (n_exp, tm), 0).astype(F32)
    eidx, wts = [], []
    for _ in range(TOP_K):
        mx = jnp.max(cur, axis=0, keepdims=True)
        ix = jnp.min(jnp.where(cur == mx, io_e, float(n_exp)), axis=0, keepdims=True)
        hit = io_e == ix
        wts.append(jnp.sum(jnp.where(hit, scores, 0.0), axis=0, keepdims=True))
        eidx.append(ix)
        cur = jnp.where(hit, neg, cur)
    eidx = jnp.concatenate(eidx, axis=0).astype(I32)
    wts = jnp.concatenate(wts, axis=0)
    wts = wts / jnp.sum(wts, axis=0, keepdims=True) * ROUTED_SCALE
    return eidx, wts


def _outproj_kernel(y_ref, wo_ref, x_ref, g1_ref, nw_ref, sh_ref, sc_ref, rw_ref, rb_ref,
                    x1_ref, h2_ref, h2p_ref, eidx_ref, wts_ref, cnt_ref, carry_scr, *, n_exp):
    i = pl.program_id(0)
    tm, d = x_ref.shape
    n_sub = d // LANE

    @pl.when(i == 0)
    def _():
        carry_scr[...] = jnp.zeros_like(carry_scr)

    half = tm // 2 if tm % (2 * LANE) == 0 else tm
    added = jnp.zeros((n_exp, 1), F32)
    for r0 in range(0, tm, half):
        rs = slice(r0, r0 + half)
        x1 = x_ref[rs, :] + g1_ref[...] * jnp.dot(y_ref[rs, :], wo_ref[...], preferred_element_type=F32)
        x1_ref[rs, :] = x1
        ms = jnp.mean(x1 * x1, axis=-1, keepdims=True)
        h2 = (x1 * lax.rsqrt(ms + EPS) * nw_ref[...]) * (1.0 + sc_ref[...]) + sh_ref[...]
        h2b = h2.astype(BF16)
        h2_ref[rs, :] = h2b
        for s in range(n_sub):
            h2p_ref[pl.ds(r0 * n_sub + s, half, stride=n_sub), :] = h2[:, s * LANE:(s + 1) * LANE]
        h_lo = (h2 - h2b.astype(F32)).astype(BF16)
        part = (jnp.dot(h2b, rw_ref[...], preferred_element_type=F32)
                + jnp.dot(h_lo, rw_ref[...], preferred_element_type=F32))
        logits = part[:, :LANE] + part[:, LANE:]
        logits_t = jnp.transpose(logits)[:n_exp, :]
        eidx, wts = _route(logits_t, rb_ref[...], n_exp)
        eidx_ref[:, rs] = eidx
        wts_ref[:, rs] = wts
        io_e = lax.broadcasted_iota(I32, (n_exp, half), 0)
        onehot = jnp.zeros((n_exp, half), F32)
        for k in range(TOP_K):
            onehot = onehot + jnp.where(io_e == eidx[k:k + 1, :], 1.0, 0.0)
        added = added + jnp.sum(onehot, axis=1, keepdims=True)
    carry_scr[...] = carry_scr[...] + added
    cnt_ref[...] = carry_scr[...].astype(I32)


def _outproj_route(y, wo, x2d, g1, norm_w, shift, scale, rw_pad, rb, seq, n_exp, tm):
    m, d = x2d.shape
    per_b = seq // tm
    n_sub = d // LANE
    bspec = pl.BlockSpec((None, 1, d), lambda i: (i // per_b, 0, 0))
    row = pl.BlockSpec((tm, d), lambda i: (i, 0))
    tok = pl.BlockSpec((TOP_K, tm), lambda i: (0, i))
    return pl.pallas_call(
        functools.partial(_outproj_kernel, n_exp=n_exp),
        grid=(m // tm,),
        in_specs=[row,
                  pl.BlockSpec((d, d), lambda i: (0, 0), pipeline_mode=pl.Buffered(1)),
                  row, bspec,
                  pl.BlockSpec((1, d), lambda i: (0, 0)),
                  bspec, bspec,
                  pl.BlockSpec((d, 2 * LANE), lambda i: (0, 0)),
                  pl.BlockSpec((n_exp, 1), lambda i: (0, 0))],
        out_specs=[row, row,
                   pl.BlockSpec((tm * n_sub, LANE), lambda i: (i, 0)),
                   tok, tok,
                   pl.BlockSpec((n_exp, LANE), lambda i: (0, 0))],
        out_shape=[jax.ShapeDtypeStruct((m, d), F32),
                   jax.ShapeDtypeStruct((m, d), BF16),
                   jax.ShapeDtypeStruct((m * n_sub, LANE), F32),
                   jax.ShapeDtypeStruct((TOP_K, m), I32),
                   jax.ShapeDtypeStruct((TOP_K, m), F32),
                   jax.ShapeDtypeStruct((n_exp, LANE), I32)],
        scratch_shapes=[pltpu.VMEM((n_exp, LANE), F32)],
        compiler_params=_cparams("arbitrary"),
        name="outproj_route",
    )(y, wo, x2d, g1, norm_w.reshape(1, d), shift, scale, rw_pad, rb.reshape(n_exp, 1))


def _expert_kernel(be_ref, bwin_ref, brem_ref, bval_ref, nact_ref, prev_win, cur_win, nxt_win,
                   h2p_hbm, wg_ref, wu_ref, wd_ref, ytk_hbm,
                   wg_s, wu_s, wd_s, xbuf0, xbuf1, ybuf0, ybuf1, sem_g, sem_s, sem_z,
                   *, n_sub, n_tok, n_blocks, trash_base):
    i = pl.program_id(0)
    nact = nact_ref[0]
    bm = EXPERT_BLOCK
    xbufs = (xbuf0, xbuf1)
    ybufs = (ybuf0, ybuf1)
    unroll = 8

    def token_of(f):
        return f & (n_tok - 1) if n_tok & (n_tok - 1) == 0 else lax.rem(f, n_tok)

    def gather_copy(tok, r, par):
        return pltpu.make_async_copy(h2p_hbm.at[pl.ds(pl.multiple_of(tok * n_sub, n_sub), n_sub), :],
                                     xbufs[par].at[pl.ds(pl.multiple_of(r * n_sub, n_sub), n_sub), :],
                                     sem_g.at[par])

    def scatter_copy(r, row, par):
        return pltpu.make_async_copy(ybufs[par].at[pl.ds(pl.multiple_of(r * n_sub, n_sub), n_sub), :],
                                     ytk_hbm.at[pl.ds(pl.multiple_of(row * n_sub, n_sub), n_sub), :],
                                     sem_s.at[par])

    def wait_gathers(par):
        @pl.loop(0, bm // unroll)
        def _(q):
            for u in range(unroll):
                gather_copy(0, 0, par).wait()

    def wait_scatters(par):
        @pl.loop(0, bm // unroll)
        def _(q):
            for u in range(unroll):
                scatter_copy(0, 0, par).wait()

    def scatter_row(win_ref, rem, valid, r, par):
        row = jnp.where(r < valid, win_ref[0, rem + r], trash_base + par * bm + r)
        return scatter_copy(r, row, par)

    @pl.when(i == 0)
    def _():
        ybuf0[...] = jnp.zeros_like(ybuf0)
        ybuf1[...] = jnp.zeros_like(ybuf1)
        for par in range(2):
            start = (trash_base + par * bm) * n_sub
            cp = pltpu.make_async_copy(ybuf0, ytk_hbm.at[pl.ds(start, bm * n_sub), :], sem_z)
            cp.start()
            cp.wait()
        rem0 = brem_ref[0]

        @pl.loop(0, bm // unroll)
        def _(q):
            for u in range(unroll):
                r = q * unroll + u
                gather_copy(token_of(cur_win[0, rem0 + r]), r, 0).start(priority=u % 2)

    def block_step(par):
        wait_gathers(par)

        @pl.when(i >= 1)
        def _():
            wait_scatters(par)

        @pl.when((i == 0) | (be_ref[i] != be_ref[jnp.maximum(i - 1, 0)]))
        def _():
            wg_s[...] = wg_ref[...].astype(BF16)
            wu_s[...] = wu_ref[...].astype(BF16)
            wd_s[...] = wd_ref[...].astype(BF16)

        nxt_rem = brem_ref[jnp.minimum(i + 1, n_blocks - 1)]
        prv = jnp.maximum(i - 1, 0)
        prv_rem = brem_ref[prv]
        prv_valid = jnp.where(i >= 1, bval_ref[prv], 0)
        for r in range(bm):
            gather_copy(token_of(nxt_win[0, nxt_rem + r]), r, 1 - par).start(priority=1)
        for r in range(bm):
            scatter_row(prev_win, prv_rem, prv_valid, r, 1 - par).start(priority=r % 2)

        x = jnp.concatenate([xbufs[par][pl.ds(s, bm, stride=n_sub), :] for s in range(n_sub)],
                            axis=1).astype(BF16)
        g = jnp.dot(x, wg_s[...], preferred_element_type=F32)
        u = jnp.dot(x, wu_s[...], preferred_element_type=F32)
        y = jnp.dot((_silu(g) * u).astype(BF16), wd_s[...], preferred_element_type=F32)
        for s in range(n_sub):
            ybufs[par][pl.ds(s, bm, stride=n_sub), :] = y[:, s * LANE:(s + 1) * LANE]

        @pl.when(i == nact - 1)
        def _():
            wait_gathers(1 - par)
            wait_scatters(1 - par)
            rem = brem_ref[i]
            valid = bval_ref[i]

            @pl.loop(0, bm // unroll)
            def _(q):
                for u_ in range(unroll):
                    scatter_row(cur_win, rem, valid, q * unroll + u_, par).start(priority=u_ % 2)

            wait_scatters(par)

    for par in range(2):
        @pl.when((i < nact) & (lax.rem(i, 2) == par))
        def _(par=par):
            block_step(par)


def _experts(h2p, order, blk_expert, blk_off, blk_valid, n_active, exp_gate, exp_up, exp_down, n_blocks, n_tok):
    n_exp, d, ff = exp_gate.shape
    n_sub = d // LANE
    bm = EXPERT_BLOCK
    n_win = order.shape[0] // bm
    pieces = order.reshape(n_win, 1, bm)
    windows = jnp.concatenate([pieces, jnp.roll(pieces, -1, axis=0)], axis=-1)
    blk_win = blk_off // bm
    blk_rem = blk_off - blk_win * bm
    trash_base = TOP_K * n_tok

    def win(step_shift):
        def index_map(i, be, bwin, *_):
            return (bwin[jnp.clip(i + step_shift, 0, n_blocks - 1)], 0, 0)
        return pl.BlockSpec((None, 1, 2 * bm), index_map, memory_space=pltpu.SMEM)

    row_buf = pltpu.VMEM((bm * n_sub, LANE), F32)
    grid_spec = pltpu.PrefetchScalarGridSpec(
        num_scalar_prefetch=5,
        grid=(n_blocks,),
        in_specs=[win(-1), win(0), win(1),
                  pl.BlockSpec(memory_space=pl.ANY),
                  pl.BlockSpec((None, d, ff), lambda i, be, *_: (be[i], 0, 0)),
                  pl.BlockSpec((None, d, ff), lambda i, be, *_: (be[i], 0, 0)),
                  pl.BlockSpec((None, ff, d), lambda i, be, *_: (be[i], 0, 0))],
        out_specs=pl.BlockSpec(memory_space=pl.ANY),
        scratch_shapes=[pltpu.VMEM((d, ff), BF16), pltpu.VMEM((d, ff), BF16), pltpu.VMEM((ff, d), BF16),
                        row_buf, row_buf, row_buf, row_buf,
                        pltpu.SemaphoreType.DMA((2,)), pltpu.SemaphoreType.DMA((2,)),
                        pltpu.SemaphoreType.DMA(())],
    )
    return pl.pallas_call(
        functools.partial(_expert_kernel, n_sub=n_sub, n_tok=n_tok, n_blocks=n_blocks, trash_base=trash_base),
        grid_spec=grid_spec,
        out_shape=jax.ShapeDtypeStruct(((trash_base + 2 * bm) * n_sub, LANE), F32),
        compiler_params=_cparams("arbitrary"),
        name="experts",
    )(blk_expert, blk_win, blk_rem, blk_valid, n_active, windows, windows, windows, h2p, exp_gate, exp_up,
      exp_down)


def _combine_kernel(*refs, tmc, n_out):
    yk_refs = refs[:TOP_K]
    wt_ref, h2_ref, x1_ref, g2_ref, sg_ref, su_ref, sd_ref, o_ref, acc_scr = refs[TOP_K:]
    h = h2_ref[...]
    g = jnp.dot(h, sg_ref[...], preferred_element_type=F32)
    u = jnp.dot(h, su_ref[...], preferred_element_type=F32)
    shared = jnp.dot((_silu(g) * u).astype(BF16), sd_ref[...], preferred_element_type=F32)
    unroll = 4

    @pl.loop(0, tmc // unroll)
    def _(q):
        for j in range(unroll):
            t = q * unroll + j
            rows = pl.ds(pl.multiple_of(t * n_out, n_out), n_out)
            acc = wt_ref[0, t] * yk_refs[0][rows, :]
            for k in range(1, TOP_K):
                acc = acc + wt_ref[k, t] * yk_refs[k][rows, :]
            acc_scr[rows, :] = acc

    routed = jnp.concatenate([acc_scr[pl.ds(s, tmc, stride=n_out), :] for s in range(n_out)], axis=1)
    o_ref[...] = x1_ref[...] + g2_ref[...] * (routed + shared)


def _combine(ytk, wts, h2, x1, g2, sg, su, sd, seq, tmc):
    m, d = x1.shape
    ff = sg.shape[1]
    n_out = d // LANE
    per_b = seq // tmc
    tiles = m // tmc
    row = pl.BlockSpec((tmc, d), lambda i: (i, 0))
    yk_specs = [pl.BlockSpec((tmc * n_out, LANE), lambda i, k=k: (k * tiles + i, 0)) for k in range(TOP_K)]
    return pl.pallas_call(
        functools.partial(_combine_kernel, tmc=tmc, n_out=n_out),
        grid=(tiles,),
        in_specs=yk_specs + [pl.BlockSpec((TOP_K, tmc), lambda i: (0, i), memory_space=pltpu.SMEM),
                             row, row,
                             pl.BlockSpec((None, 1, d), lambda i: (i // per_b, 0, 0)),
                             pl.BlockSpec((d, ff), lambda i: (0, 0)),
                             pl.BlockSpec((d, ff), lambda i: (0, 0)),
                             pl.BlockSpec((ff, d), lambda i: (0, 0))],
        out_specs=row,
        out_shape=jax.ShapeDtypeStruct((m, d), F32),
        scratch_shapes=[pltpu.VMEM((tmc * n_out, LANE), F32)],
        compiler_params=_cparams("arbitrary"),
        name="combine",
    )(*([ytk] * TOP_K), wts, h2, x1, g2, sg, su, sd)


def kernel(x, c, ctx, c_ctx, mod_w, mod_b, norm1_w, w_in, q_norm_w, k_norm_w, w_fourier_out, w_attn_out, w_out,
           norm2_w, router_w, router_b, exp_gate, exp_up, exp_down, shared_gate, shared_up, shared_down):
    batch, seq, d = x.shape
    n_ctx = ctx.shape[1]
    assert mod_w.shape[0] == 1, "single-layer block only"
    assert batch + 1 <= 8 and seq % GRID_W == 0
    fw = w_fourier_out.shape[1]
    att = w_attn_out.shape[1]
    in_cols = w_in.shape[2]
    kvw = (in_cols - fw - att - 2 * d) // 2
    n_kv = kvw // HEAD_DIM
    n_exp = router_w.shape[2]
    assert att == n_kv * Q_GROUP * HEAD_DIM and n_exp <= LANE
    q_off, k_off = fw, fw + att
    v_off = k_off + kvw
    gf_off = v_off + kvw
    ga_off = gf_off + d
    m = batch * seq
    assert q_off % (Q_GROUP * HEAD_DIM) == 0 and kvw % HEAD_DIM == 0 and fw % F_GROUP_DIM == 0
    assert gf_off % min(512, d) == 0 and d % (2 * LANE) == 0

    cond = jnp.concatenate([c, c_ctx[None], jnp.zeros((8 - batch - 1, d), F32)], axis=0)
    mod = _adaln(cond, mod_w[0], mod_b[0])
    sh1, sc1, g1, sh2, sc2, g2 = [mod[:batch, j * d:(j + 1) * d].reshape(batch, 1, d) for j in range(6)]
    csh1 = jnp.broadcast_to(mod[batch, 0:d], (batch, 1, d))
    csc1 = jnp.broadcast_to(mod[batch, d:2 * d], (batch, 1, d))

    cos_t, sin_t = _rope_tables(seq)
    tn = min(512, kvw)
    tm = min(1024, seq)
    modes = (("plain", 0, q_off // tn), ("q", q_off // tn, k_off // tn), ("k", k_off // tn, v_off // tn),
             ("plain", v_off // tn, gf_off // tn), ("sigmoid", gf_off // tn, in_cols // tn))
    proj = _inproj(x.reshape(m, d), sh1, sc1, norm1_w[0], w_in[0], q_norm_w[0], k_norm_w[0], cos_t, sin_t,
                   modes, seq, tm, tn)
    cmodes = (("k_norope", 0, kvw // tn), ("plain", kvw // tn, 2 * kvw // tn))
    tmc_ctx = min(n_ctx, 256)
    kvc = _inproj(ctx.reshape(batch * n_ctx, d), csh1, csc1, norm1_w[0], w_in[0], q_norm_w[0],
                  k_norm_w[0], cos_t[:tmc_ctx], sin_t[:tmc_ctx], cmodes, n_ctx, tmc_ctx, tn, col0=k_off,
                  n_cols=2 * kvw)

    ya = _attention(proj, kvc, batch, seq, n_ctx, n_kv, q_off, k_off, v_off, tq=min(512, seq), tk=min(512, seq))
    yf = _fourier_mix(proj, batch, seq, fw)
    y = _merge(yf, ya, w_fourier_out[0].astype(BF16), w_attn_out[0].astype(BF16), proj, gf_off, ga_off,
               tm=min(1024, seq), tn=min(512, d))

    rw_pad = jnp.pad(router_w[0], ((0, 0), (0, LANE - n_exp)))
    rw_hi = rw_pad.astype(BF16)
    rw_pad = jnp.concatenate([rw_hi, (rw_pad - rw_hi.astype(F32)).astype(BF16)], axis=1)
    x1, h2, h2p, eidx, wts, cnt = _outproj_route(
        y, w_out[0].astype(BF16), x.reshape(m, d), g1, norm2_w[0], sh2, sc2, rw_pad, router_b[0], seq, n_exp,
        tm=min(512, seq))

    n_assign = m * TOP_K
    assert n_assign % EXPERT_BLOCK == 0
    _, order = lax.sort((eidx.reshape(-1), jnp.arange(n_assign, dtype=I32)), num_keys=1)
    counts = cnt[:, 0]
    starts = jnp.cumsum(counts) - counts
    nb_e = (counts + EXPERT_BLOCK - 1) // EXPERT_BLOCK
    cum_nb = jnp.cumsum(nb_e)
    n_blocks = n_assign // EXPERT_BLOCK + n_exp
    n_active = jnp.maximum(cum_nb[-1], 1).astype(I32)
    blk = jnp.minimum(jnp.arange(n_blocks, dtype=I32), n_active - 1)
    blk_expert = jnp.minimum(jnp.sum(blk[:, None] >= cum_nb[None, :], axis=1), n_exp - 1).astype(I32)
    is_e = blk_expert[:, None] == jnp.arange(n_exp, dtype=I32)[None, :]
    lookup = lambda table: jnp.sum(jnp.where(is_e, table[None, :], 0), axis=1)
    blk_j = blk - lookup(cum_nb - nb_e)
    blk_off = (lookup(starts) + blk_j * EXPERT_BLOCK).astype(I32)
    blk_valid = jnp.clip(lookup(counts) - blk_j * EXPERT_BLOCK, 0, EXPERT_BLOCK).astype(I32)

    ytk = _experts(h2p, order, blk_expert, blk_off, blk_valid, n_active.reshape(1), exp_gate[0], exp_up[0],
                   exp_down[0], n_blocks, m)
    out = _combine(ytk, wts, h2, x1, g2, shared_gate[0].astype(BF16), shared_up[0].astype(BF16),
                   shared_down[0].astype(BF16), seq, tmc=min(128, seq))
    return out.reshape(batch, seq, d)
```

```python
import functools
import math

import jax
import jax.numpy as jnp
from jax import lax
from jax.experimental import pallas as pl
from jax.experimental.pallas import tpu as pltpu

F32 = jnp.float32
BF16 = jnp.bfloat16
I32 = jnp.int32

GRID_W = 64
HEAD_DIM = 128
Q_GROUP = 4
ROPE_FREQS = HEAD_DIM // 4
ROPE_THETA = 10000.0
F_GROUP_DIM = 128
N_EXPERT_GROUPS = 8
TOPK_GROUPS = 4
TOP_K = 8
ROUTED_SCALE = 2.5
EPS = 1e-6
LANE = 128
EXPERT_BLOCK = 128
VMEM_LIMIT = 56 * 1024 * 1024


def _cparams(*sem):
    return pltpu.CompilerParams(dimension_semantics=sem, vmem_limit_bytes=VMEM_LIMIT)


def _silu(v):
    return v * jax.nn.sigmoid(v)


def _adaln_kernel(c_ref, w_ref, b_ref, o_ref):
    a = _silu(c_ref[...])
    o_ref[...] = jnp.dot(a.astype(BF16), w_ref[...].astype(BF16), preferred_element_type=F32) + b_ref[...]


def _adaln(cond_pad, mod_w, mod_b):
    d, n = mod_w.shape
    tn = min(n, 1024)
    return pl.pallas_call(
        _adaln_kernel,
        grid=(n // tn,),
        in_specs=[pl.BlockSpec((8, d), lambda j: (0, 0)),
                  pl.BlockSpec((d, tn), lambda j: (0, j)),
                  pl.BlockSpec((1, tn), lambda j: (0, j))],
        out_specs=pl.BlockSpec((8, tn), lambda j: (0, j)),
        out_shape=jax.ShapeDtypeStruct((8, n), F32),
        compiler_params=_cparams("arbitrary"),
        name="adaln",
    )(cond_pad, mod_w, mod_b.reshape(1, n))


def _head_norm_rope(a, w, cos, sin, scale):
    ms = jnp.mean(a * a, axis=-1, keepdims=True)
    a = a * lax.rsqrt(ms + EPS) * w
    if cos is not None:
        lane = lax.broadcasted_iota(I32, a.shape, 1)
        first = (lane % (2 * ROPE_FREQS)) < ROPE_FREQS
        partner = jnp.where(first, pltpu.roll(a, HEAD_DIM - ROPE_FREQS, 1), pltpu.roll(a, ROPE_FREQS, 1))
        a = a * cos + partner * sin
    if scale is not None:
        a = a * scale
    return a


def _inproj_kernel(x_ref, sh_ref, sc_ref, nw_ref, w_ref, qw_ref, kw_ref, cos_ref, sin_ref, o_ref, h_scr, wb_scr,
                   acc_scr, *, modes, tn):
    j = pl.program_id(1)

    @pl.when(j == 0)
    def _():
        x = x_ref[...]
        ms = jnp.mean(x * x, axis=-1, keepdims=True)
        y = x * lax.rsqrt(ms + EPS) * nw_ref[...]
        h_scr[...] = (y * (1.0 + sc_ref[...]) + sh_ref[...]).astype(BF16)

    wb_scr[...] = w_ref[...].astype(BF16)

    def tile(c0, c1):
        return jnp.dot(h_scr[...], wb_scr[:, c0:c1], preferred_element_type=F32)

    piece = 2 * HEAD_DIM if tn % (2 * HEAD_DIM) == 0 else HEAD_DIM
    gated = functools.reduce(jnp.logical_or, [(j >= j0) & (j < j1) for mode, j0, j1 in modes if mode == "sigmoid"],
                             jnp.bool_(False))

    @pl.when(gated)
    def _():
        for c0 in range(0, tn, piece):
            o_ref[:, c0:c0 + piece] = jax.nn.sigmoid(tile(c0, c0 + piece)).astype(BF16)

    @pl.when(jnp.logical_not(gated))
    def _():
        acc_scr[...] = tile(0, tn)

    for mode, j0, j1 in modes:
        if mode == "sigmoid":
            continue

        @pl.when((j >= j0) & (j < j1))
        def _(mode=mode):
            if mode == "plain":
                o_ref[...] = acc_scr[...].astype(BF16)
                return
            rope = mode in ("q", "k")
            w = qw_ref[...] if mode == "q" else kw_ref[...]
            scale = HEAD_DIM ** -0.5 * math.log2(math.e) if mode == "q" else None
            for h in range(tn // HEAD_DIM):
                sl = slice(h * HEAD_DIM, (h + 1) * HEAD_DIM)
                a = _head_norm_rope(acc_scr[:, sl], w, cos_ref[...] if rope else None,
                                    sin_ref[...] if rope else None, scale)
                o_ref[:, sl] = a.astype(BF16)


def _inproj(x2d, shift, scale, norm_w, w_in, q_norm_w, k_norm_w, cos_t, sin_t, modes, seq, tm, tn, col0=0,
            n_cols=None):
    m, d = x2d.shape
    n = w_in.shape[1] if n_cols is None else n_cols
    cb0 = col0 // tn
    per_b = seq // tm
    pos_blocks = cos_t.shape[0] // tm
    kern = functools.partial(_inproj_kernel, modes=modes, tn=tn)
    return pl.pallas_call(
        kern,
        grid=(m // tm, n // tn),
        in_specs=[pl.BlockSpec((tm, d), lambda i, j: (i, 0)),
                  pl.BlockSpec((None, 1, d), lambda i, j: (i // per_b, 0, 0)),
                  pl.BlockSpec((None, 1, d), lambda i, j: (i // per_b, 0, 0)),
                  pl.BlockSpec((1, d), lambda i, j: (0, 0)),
                  pl.BlockSpec((d, tn), lambda i, j: (0, cb0 + j)),
                  pl.BlockSpec((1, HEAD_DIM), lambda i, j: (0, 0)),
                  pl.BlockSpec((1, HEAD_DIM), lambda i, j: (0, 0)),
                  pl.BlockSpec((tm, HEAD_DIM), lambda i, j: (i % pos_blocks, 0)),
                  pl.BlockSpec((tm, HEAD_DIM), lambda i, j: (i % pos_blocks, 0))],
        out_specs=pl.BlockSpec((tm, tn), lambda i, j: (i, j)),
        out_shape=jax.ShapeDtypeStruct((m, n), BF16),
        scratch_shapes=[pltpu.VMEM((tm, d), BF16), pltpu.VMEM((d, tn), BF16), pltpu.VMEM((tm, tn), F32)],
        compiler_params=_cparams("arbitrary", "arbitrary"),
        name="inproj",
    )(x2d, shift, scale, norm_w.reshape(1, d), w_in, q_norm_w.reshape(1, HEAD_DIM),
      k_norm_w.reshape(1, HEAD_DIM), cos_t, sin_t)


def _rope_tables(seq):
    rows = seq // GRID_W
    row = jnp.repeat(jnp.arange(rows), GRID_W)
    col = jnp.tile(jnp.arange(GRID_W), rows)
    pos = jnp.stack([row, col], axis=-1).astype(F32)
    inv_freq = ROPE_THETA ** (-jnp.arange(ROPE_FREQS, dtype=F32) / ROPE_FREQS)
    ang = pos[:, :, None] * inv_freq
    cos, sin = jnp.cos(ang), jnp.sin(ang)
    cos_t = jnp.concatenate([cos[:, 0], cos[:, 0], cos[:, 1], cos[:, 1]], axis=-1)
    sin_t = jnp.concatenate([-sin[:, 0], sin[:, 0], -sin[:, 1], sin[:, 1]], axis=-1)
    return cos_t, sin_t


def _attn_kernel(q_ref, k_ref, v_ref, kc_ref, vc_ref, o_ref, vt_scr, vct_scr, *, tq, tk, n_chunks):
    @pl.when(pl.program_id(2) == 0)
    def _():
        for c in range(n_chunks):
            vt_scr[c] = jnp.transpose(v_ref[c * tk:(c + 1) * tk, :].astype(F32)).astype(BF16)
        vct_scr[...] = jnp.transpose(vc_ref[...].astype(F32)).astype(BF16)

    q = jnp.concatenate([q_ref[:, h * HEAD_DIM:(h + 1) * HEAD_DIM] for h in range(Q_GROUP)], axis=0)
    qt = jnp.transpose(q.astype(F32)).astype(BF16)
    m_cols = Q_GROUP * tq

    def scores(kb):
        return jnp.dot(kb, qt, preferred_element_type=F32)

    def update(st, vtb, carry):
        m, l, acc = carry
        m_new = jnp.maximum(m, jnp.max(st, axis=0, keepdims=True))
        alpha = jnp.exp2(m - m_new)
        pt = jnp.exp2(st - m_new)
        l = alpha * l + jnp.sum(pt, axis=0, keepdims=True)
        acc = alpha * acc + jnp.dot(vtb, pt.astype(BF16), preferred_element_type=F32)
        return m_new, l, acc

    carry = (jnp.full((1, m_cols), -jnp.inf, F32), jnp.zeros((1, m_cols), F32),
             jnp.zeros((HEAD_DIM, m_cols), F32))
    st = scores(k_ref[0:tk, :])
    for c in range(n_chunks):
        st_next = scores(k_ref[(c + 1) * tk:(c + 2) * tk, :]) if c + 1 < n_chunks else scores(kc_ref[...])
        carry = update(st, vt_scr[c], carry)
        st = st_next
    _, l, acc = update(st, vct_scr[...], carry)
    o = jnp.transpose(acc / l)
    for h in range(Q_GROUP):
        o_ref[:, h * HEAD_DIM:(h + 1) * HEAD_DIM] = o[h * tq:(h + 1) * tq].astype(BF16)


def _attention(proj, kvc, batch, seq, n_ctx, n_kv, q_off, k_off, v_off, tq, tk):
    gw = Q_GROUP * HEAD_DIM
    per_b = seq // tq
    n_chunks = seq // tk
    kern = functools.partial(_attn_kernel, tq=tq, tk=tk, n_chunks=n_chunks)
    return pl.pallas_call(
        kern,
        grid=(batch, n_kv, per_b),
        in_specs=[pl.BlockSpec((tq, gw), lambda b, g, i: (b * per_b + i, q_off // gw + g)),
                  pl.BlockSpec((seq, HEAD_DIM), lambda b, g, i: (b, k_off // HEAD_DIM + g)),
                  pl.BlockSpec((seq, HEAD_DIM), lambda b, g, i: (b, v_off // HEAD_DIM + g)),
                  pl.BlockSpec((n_ctx, HEAD_DIM), lambda b, g, i: (b, g)),
                  pl.BlockSpec((n_ctx, HEAD_DIM), lambda b, g, i: (b, n_kv + g))],
        out_specs=pl.BlockSpec((tq, gw), lambda b, g, i: (b * per_b + i, g)),
        out_shape=jax.ShapeDtypeStruct((batch * seq, n_kv * gw), BF16),
        scratch_shapes=[pltpu.VMEM((n_chunks, HEAD_DIM, tk), BF16), pltpu.VMEM((HEAD_DIM, n_ctx), BF16)],
        compiler_params=_cparams("arbitrary", "arbitrary", "arbitrary"),
        name="attention",
    )(proj, proj, proj, kvc, kvc)


def _dft_mats(n_rows, n_cols):
    def cs(k, n):
        ang = (2.0 * math.pi / n) * (k % n).astype(F32)
        return jnp.cos(ang), jnp.sin(ang)

    ch = jnp.arange(F_GROUP_DIM)
    cc, sc = cs(ch[:, None] * ch[None, :], F_GROUP_DIM)
    mat_a = jnp.concatenate([cc, -sc], axis=1).astype(BF16)
    r = jnp.arange(n_rows)
    cr, sr = cs(r[:, None] * r[None, :], n_rows)
    mat_1 = jnp.concatenate([jnp.concatenate([cr, sr], axis=1),
                             jnp.concatenate([-sr, cr], axis=1)], axis=0).astype(BF16)
    c = jnp.arange(n_cols)
    c2, s2 = cs(c[:, None] * c[None, :], n_cols)
    mat_2 = jnp.concatenate([c2, s2], axis=1).astype(BF16)
    tr, ts = cs(r[:, None] * c[None, :], n_rows * n_cols)
    tw_r = jnp.repeat(tr, LANE, axis=1)
    tw_i = jnp.repeat(-ts, LANE, axis=1)
    return mat_a, mat_1, mat_2, tw_r, tw_i


def _fourier_a_kernel(u_ref, m_ref, vr_ref, vi_ref, *, groups):
    for g in range(groups):
        sl = slice(g * F_GROUP_DIM, (g + 1) * F_GROUP_DIM)
        r = jnp.dot(u_ref[:, sl], m_ref[...], preferred_element_type=F32)
        vr_ref[:, sl] = r[:, :F_GROUP_DIM].astype(BF16)
        vi_ref[:, sl] = r[:, F_GROUP_DIM:].astype(BF16)


def _fourier_1_kernel(ar_ref, ai_ref, m_ref, twr_ref, twi_ref, pr_ref, pi_ref, *, nseg, fw, n_rows):
    rhs = jnp.concatenate([ar_ref[...], ai_ref[...]], axis=0)
    z = jnp.dot(m_ref[...], rhs, preferred_element_type=F32)
    reps = fw // LANE
    for s in range(nseg):
        zr = z[:n_rows, s * fw:(s + 1) * fw]
        zi = z[n_rows:, s * fw:(s + 1) * fw]
        tr = jnp.tile(twr_ref[:, s * LANE:(s + 1) * LANE], (1, reps))
        ti = jnp.tile(twi_ref[:, s * LANE:(s + 1) * LANE], (1, reps))
        pr_ref[s] = (zr * tr - zi * ti).astype(BF16)
        pi_ref[s] = (zr * ti + zi * tr).astype(BF16)


def _fourier_2_kernel(ar_ref, ai_ref, m_ref, o_ref, *, norm):
    rhs = jnp.concatenate([ar_ref[...], ai_ref[...]], axis=0)
    o_ref[...] = (jnp.dot(m_ref[...], rhs, preferred_element_type=F32) * norm).astype(BF16)


def _fourier_mix(proj, batch, seq, fw):
    n_cols = GRID_W
    n_rows = seq // GRID_W
    groups = fw // F_GROUP_DIM
    mat_a, mat_1, mat_2, tw_r, tw_i = _dft_mats(n_rows, n_cols)
    m = batch * seq
    tm = min(seq, 1024)
    vr, vi = pl.pallas_call(
        functools.partial(_fourier_a_kernel, groups=groups),
        grid=(m // tm,),
        in_specs=[pl.BlockSpec((tm, fw), lambda i: (i, 0)),
                  pl.BlockSpec((F_GROUP_DIM, 2 * F_GROUP_DIM), lambda i: (0, 0))],
        out_specs=[pl.BlockSpec((tm, fw), lambda i: (i, 0))] * 2,
        out_shape=[jax.ShapeDtypeStruct((m, fw), BF16)] * 2,
        compiler_params=_cparams("arbitrary"),
        name="fourier_channels",
    )(proj, mat_a)
    wide = n_cols * fw
    vr = vr.reshape(batch * n_rows, wide)
    vi = vi.reshape(batch * n_rows, wide)
    nseg = min(n_cols, 4)
    pr, pi = pl.pallas_call(
        functools.partial(_fourier_1_kernel, nseg=nseg, fw=fw, n_rows=n_rows),
        grid=(batch, n_cols // nseg),
        in_specs=[pl.BlockSpec((n_rows, nseg * fw), lambda b, j: (b, j)),
                  pl.BlockSpec((n_rows, nseg * fw), lambda b, j: (b, j)),
                  pl.BlockSpec((2 * n_rows, 2 * n_rows), lambda b, j: (0, 0)),
                  pl.BlockSpec((n_rows, nseg * LANE), lambda b, j: (0, j)),
                  pl.BlockSpec((n_rows, nseg * LANE), lambda b, j: (0, j))],
        out_specs=[pl.BlockSpec((nseg, n_rows, fw), lambda b, j: (b * (n_cols // nseg) + j, 0, 0))] * 2,
        out_shape=[jax.ShapeDtypeStruct((batch * n_cols, n_rows, fw), BF16)] * 2,
        compiler_params=_cparams("arbitrary", "arbitrary"),
        name="fourier_rows",
    )(vr, vi, mat_1, tw_r, tw_i)
    wide2 = n_rows * fw
    pr = pr.reshape(batch * n_cols, wide2)
    pi = pi.reshape(batch * n_cols, wide2)
    tc = min(wide2, 8192)
    norm = 1.0 / math.sqrt(seq * F_GROUP_DIM)
    yf = pl.pallas_call(
        functools.partial(_fourier_2_kernel, norm=norm),
        grid=(batch, wide2 // tc),
        in_specs=[pl.BlockSpec((n_cols, tc), lambda b, j: (b, j)),
                  pl.BlockSpec((n_cols, tc), lambda b, j: (b, j)),
                  pl.BlockSpec((n_cols, 2 * n_cols), lambda b, j: (0, 0))],
        out_specs=pl.BlockSpec((n_cols, tc), lambda b, j: (b, j)),
        out_shape=jax.ShapeDtypeStruct((batch * n_cols, wide2), BF16),
        compiler_params=_cparams("arbitrary", "arbitrary"),
        name="fourier_cols",
    )(pr, pi, mat_2)
    return yf.reshape(m, fw)


def _merge_kernel(yf_ref, ya_ref, wfo_ref, wao_ref, gf_ref, ga_ref, o_ref):
    a = jnp.dot(yf_ref[...], wfo_ref[...], preferred_element_type=F32)
    b = jnp.dot(ya_ref[...], wao_ref[...], preferred_element_type=F32)
    o_ref[...] = (gf_ref[...].astype(F32) * a + ga_ref[...].astype(F32) * b).astype(BF16)


def _merge(yf, ya, wfo, wao, proj, gf_off, ga_off, tm, tn):
    m, fw = yf.shape
    aw = ya.shape[1]
    d = wfo.shape[1]
    return pl.pallas_call(
        _merge_kernel,
        grid=(m // tm, d // tn),
        in_specs=[pl.BlockSpec((tm, fw), lambda i, j: (i, 0)),
                  pl.BlockSpec((tm, aw), lambda i, j: (i, 0)),
                  pl.BlockSpec((fw, tn), lambda i, j: (0, j)),
                  pl.BlockSpec((aw, tn), lambda i, j: (0, j)),
                  pl.BlockSpec((tm, tn), lambda i, j: (i, gf_off // tn + j)),
                  pl.BlockSpec((tm, tn), lambda i, j: (i, ga_off // tn + j))],
        out_specs=pl.BlockSpec((tm, tn), lambda i, j: (i, j)),
        out_shape=jax.ShapeDtypeStruct((m, d), BF16),
        compiler_params=_cparams("arbitrary", "arbitrary"),
        name="merge",
    )(yf, ya, wfo, wao, proj, proj)


def _route(logits_t, bias, n_exp):
    gsz = n_exp // N_EXPERT_GROUPS
    tm = logits_t.shape[1]
    neg = -jnp.inf
    scores = jax.nn.sigmoid(logits_t)
    biased = scores + bias
    io_g = lax.broadcasted_iota(I32, (gsz, tm), 0).astype(F32)
    gs = []
    for g in range(N_EXPERT_GROUPS):
        grp = biased[g * gsz:(g + 1) * gsz, :]
        m1 = jnp.max(grp, axis=0, keepdims=True)
        i1 = jnp.min(jnp.where(grp == m1, io_g, float(gsz)), axis=0, keepdims=True)
        m2 = jnp.max(jnp.where(io_g == i1, neg, grp), axis=0, keepdims=True)
        gs.append(m1 + m2)
    cur = jnp.concatenate(gs, axis=0)
    io_n = lax.broadcasted_iota(I32, (N_EXPERT_GROUPS, tm), 0).astype(F32)
    sel = jnp.zeros((N_EXPERT_GROUPS, tm), F32)
    for _ in range(TOPK_GROUPS):
        mx = jnp.max(cur, axis=0, keepdims=True)
        ix = jnp.min(jnp.where(cur == mx, io_n, float(N_EXPERT_GROUPS)), axis=0, keepdims=True)
        hit = io_n == ix
        sel = jnp.where(hit, 1.0, sel)
        cur = jnp.where(hit, neg, cur)
    cur = jnp.concatenate([jnp.where(sel[g:g + 1, :] > 0.0, biased[g * gsz:(g + 1) * gsz, :], neg)
                           for g in range(N_EXPERT_GROUPS)], axis=0)
    io_e = lax.broadcasted_iota(I32, (n_exp, tm), 0).astype(F32)
    eidx, wts = [], []
    for _ in range(TOP_K):
        mx = jnp.max(cur, axis=0, keepdims=True)
        ix = jnp.min(jnp.where(cur == mx, io_e, float(n_exp)), axis=0, keepdims=True)
        hit = io_e == ix
        wts.append(jnp.sum(jnp.where(hit, scores, 0.0), axis=0, keepdims=True))
        eidx.append(ix)
        cur = jnp.where(hit, neg, cur)
    eidx = jnp.concatenate(eidx, axis=0).astype(I32)
    wts = jnp.concatenate(wts, axis=0)
    wts = wts / jnp.sum(wts, axis=0, keepdims=True) * ROUTED_SCALE
    return eidx, wts


def _outproj_kernel(y_ref, wo_ref, x_ref, g1_ref, nw_ref, sh_ref, sc_ref, rw_ref, rb_ref,
                    x1_ref, h2_ref, h2p_ref, eidx_ref, wts_ref, cnt_ref, carry_scr, *, n_exp):
    i = pl.program_id(0)
    tm, d = x_ref.shape
    n_sub = d // LANE

    @pl.when(i == 0)
    def _():
        carry_scr[...] = jnp.zeros_like(carry_scr)

    half = tm // 2 if tm % (2 * LANE) == 0 else tm
    added = jnp.zeros((n_exp, 1), F32)
    for r0 in range(0, tm, half):
        rs = slice(r0, r0 + half)
        x1 = x_ref[rs, :] + g1_ref[...] * jnp.dot(y_ref[rs, :], wo_ref[...], preferred_element_type=F32)
        x1_ref[rs, :] = x1
        ms = jnp.mean(x1 * x1, axis=-1, keepdims=True)
        h2 = (x1 * lax.rsqrt(ms + EPS) * nw_ref[...]) * (1.0 + sc_ref[...]) + sh_ref[...]
        h2b = h2.astype(BF16)
        h2_ref[rs, :] = h2b
        for s in range(n_sub):
            h2p_ref[pl.ds(r0 * n_sub + s, half, stride=n_sub), :] = h2[:, s * LANE:(s + 1) * LANE]
        h_lo = (h2 - h2b.astype(F32)).astype(BF16)
        part = (jnp.dot(h2b, rw_ref[...], preferred_element_type=F32)
                + jnp.dot(h_lo, rw_ref[...], preferred_element_type=F32))
        logits = part[:, :LANE] + part[:, LANE:]
        logits_t = jnp.transpose(logits)[:n_exp, :]
        eidx, wts = _route(logits_t, rb_ref[...], n_exp)
        eidx_ref[:, rs] = eidx
        wts_ref[:, rs] = wts
        io_e = lax.broadcasted_iota(I32, (n_exp, half), 0)
        onehot = jnp.zeros((n_exp, half), F32)
        for k in range(TOP_K):
            onehot = onehot + jnp.where(io_e == eidx[k:k + 1, :], 1.0, 0.0)
        added = added + jnp.sum(onehot, axis=1, keepdims=True)
    carry_scr[...] = carry_scr[...] + added
    cnt_ref[...] = carry_scr[...].astype(I32)


def _outproj_route(y, wo, x2d, g1, norm_w, shift, scale, rw_pad, rb, seq, n_exp, tm):
    m, d = x2d.shape
    per_b = seq // tm
    n_sub = d // LANE
    bspec = pl.BlockSpec((None, 1, d), lambda i: (i // per_b, 0, 0))
    row = pl.BlockSpec((tm, d), lambda i: (i, 0))
    tok = pl.BlockSpec((TOP_K, tm), lambda i: (0, i))
    return pl.pallas_call(
        functools.partial(_outproj_kernel, n_exp=n_exp),
        grid=(m // tm,),
        in_specs=[row,
                  pl.BlockSpec((d, d), lambda i: (0, 0), pipeline_mode=pl.Buffered(1)),
                  row, bspec,
                  pl.BlockSpec((1, d), lambda i: (0, 0)),
                  bspec, bspec,
                  pl.BlockSpec((d, 2 * LANE), lambda i: (0, 0)),
                  pl.BlockSpec((n_exp, 1), lambda i: (0, 0))],
        out_specs=[row, row,
                   pl.BlockSpec((tm * n_sub, LANE), lambda i: (i, 0)),
                   tok, tok,
                   pl.BlockSpec((n_exp, LANE), lambda i: (0, 0))],
        out_shape=[jax.ShapeDtypeStruct((m, d), F32),
                   jax.ShapeDtypeStruct((m, d), BF16),
                   jax.ShapeDtypeStruct((m * n_sub, LANE), F32),
                   jax.ShapeDtypeStruct((TOP_K, m), I32),
                   jax.ShapeDtypeStruct((TOP_K, m), F32),
                   jax.ShapeDtypeStruct((n_exp, LANE), I32)],
        scratch_shapes=[pltpu.VMEM((n_exp, LANE), F32)],
        compiler_params=_cparams("arbitrary"),
        name="outproj_route",
    )(y, wo, x2d, g1, norm_w.reshape(1, d), shift, scale, rw_pad, rb.reshape(n_exp, 1))


def _expert_kernel(be_ref, bwin_ref, brem_ref, bval_ref, nact_ref, prev_win, cur_win, nxt_win,
                   h2p_hbm, wg_ref, wu_ref, wd_ref, ytk_hbm,
                   wg_s, wu_s, wd_s, xbuf0, xbuf1, ybuf0, ybuf1, sem_g, sem_s, sem_z,
                   *, n_sub, n_tok, n_blocks, trash_base):
    i = pl.program_id(0)
    nact = nact_ref[0]
    bm = EXPERT_BLOCK
    xbufs = (xbuf0, xbuf1)
    ybufs = (ybuf0, ybuf1)
    unroll = 8

    def token_of(f):
        return f & (n_tok - 1) if n_tok & (n_tok - 1) == 0 else lax.rem(f, n_tok)

    def gather_copy(tok, r, par):
        return pltpu.make_async_copy(h2p_hbm.at[pl.ds(pl.multiple_of(tok * n_sub, n_sub), n_sub), :],
                                     xbufs[par].at[pl.ds(pl.multiple_of(r * n_sub, n_sub), n_sub), :],
                                     sem_g.at[par])

    def scatter_copy(r, row, par):
        return pltpu.make_async_copy(ybufs[par].at[pl.ds(pl.multiple_of(r * n_sub, n_sub), n_sub), :],
                                     ytk_hbm.at[pl.ds(pl.multiple_of(row * n_sub, n_sub), n_sub), :],
                                     sem_s.at[par])

    def wait_gathers(par):
        @pl.loop(0, bm // unroll)
        def _(q):
            for u in range(unroll):
                gather_copy(0, 0, par).wait()

    def wait_scatters(par):
        @pl.loop(0, bm // unroll)
        def _(q):
            for u in range(unroll):
                scatter_copy(0, 0, par).wait()

    def scatter_row(win_ref, rem, valid, r, par):
        row = jnp.where(r < valid, win_ref[0, rem + r], trash_base + par * bm + r)
        return scatter_copy(r, row, par)

    @pl.when(i == 0)
    def _():
        ybuf0[...] = jnp.zeros_like(ybuf0)
        ybuf1[...] = jnp.zeros_like(ybuf1)
        for par in range(2):
            start = (trash_base + par * bm) * n_sub
            cp = pltpu.make_async_copy(ybuf0, ytk_hbm.at[pl.ds(start, bm * n_sub), :], sem_z)
            cp.start()
            cp.wait()
        rem0 = brem_ref[0]

        @pl.loop(0, bm // unroll)
        def _(q):
            for u in range(unroll):
                r = q * unroll + u
                gather_copy(token_of(cur_win[0, rem0 + r]), r, 0).start(priority=u % 2)

    def block_step(par):
        wait_gathers(par)

        @pl.when(i >= 1)
        def _():
            wait_scatters(par)

        @pl.when((i == 0) | (be_ref[i] != be_ref[jnp.maximum(i - 1, 0)]))
        def _():
            wg_s[...] = wg_ref[...].astype(BF16)
            wu_s[...] = wu_ref[...].astype(BF16)
            wd_s[...] = wd_ref[...].astype(BF16)

        nxt_rem = brem_ref[jnp.minimum(i + 1, n_blocks - 1)]
        prv = jnp.maximum(i - 1, 0)
        prv_rem = brem_ref[prv]
        prv_valid = jnp.where(i >= 1, bval_ref[prv], 0)
        for r in range(bm):
            gather_copy(token_of(nxt_win[0, nxt_rem + r]), r, 1 - par).start(priority=r % 2)
        for r in range(bm):
            scatter_row(prev_win, prv_rem, prv_valid, r, 1 - par).start(priority=r % 2)

        x = jnp.concatenate([xbufs[par][pl.ds(s, bm, stride=n_sub), :] for s in range(n_sub)],
                            axis=1).astype(BF16)
        g = jnp.dot(x, wg_s[...], preferred_element_type=F32)
        u = jnp.dot(x, wu_s[...], preferred_element_type=F32)
        y = jnp.dot((_silu(g) * u).astype(BF16), wd_s[...], preferred_element_type=F32)
        for s in range(n_sub):
            ybufs[par][pl.ds(s, bm, stride=n_sub), :] = y[:, s * LANE:(s + 1) * LANE]

        @pl.when(i == nact - 1)
        def _():
            wait_gathers(1 - par)
            wait_scatters(1 - par)
            rem = brem_ref[i]
            valid = bval_ref[i]

            @pl.loop(0, bm // unroll)
            def _(q):
                for u_ in range(unroll):
                    scatter_row(cur_win, rem, valid, q * unroll + u_, par).start(priority=u_ % 2)

            wait_scatters(par)

    for par in range(2):
        @pl.when((i < nact) & (lax.rem(i, 2) == par))
        def _(par=par):
            block_step(par)


def _experts(h2p, order, blk_expert, blk_off, blk_valid, n_active, exp_gate, exp_up, exp_down, n_blocks, n_tok):
    n_exp, d, ff = exp_gate.shape
    n_sub = d // LANE
    bm = EXPERT_BLOCK
    n_win = order.shape[0] // bm
    pieces = order.reshape(n_win, 1, bm)
    windows = jnp.concatenate([pieces, jnp.roll(pieces, -1, axis=0)], axis=-1)
    blk_win = blk_off // bm
    blk_rem = blk_off - blk_win * bm
    trash_base = TOP_K * n_tok

    def win(step_shift):
        def index_map(i, be, bwin, *_):
            return (bwin[jnp.clip(i + step_shift, 0, n_blocks - 1)], 0, 0)
        return pl.BlockSpec((None, 1, 2 * bm), index_map, memory_space=pltpu.SMEM)

    row_buf = pltpu.VMEM((bm * n_sub, LANE), F32)
    grid_spec = pltpu.PrefetchScalarGridSpec(
        num_scalar_prefetch=5,
        grid=(n_blocks,),
        in_specs=[win(-1), win(0), win(1),
                  pl.BlockSpec(memory_space=pl.ANY),
                  pl.BlockSpec((None, d, ff), lambda i, be, *_: (be[i], 0, 0)),
                  pl.BlockSpec((None, d, ff), lambda i, be, *_: (be[i], 0, 0)),
                  pl.BlockSpec((None, ff, d), lambda i, be, *_: (be[i], 0, 0))],
        out_specs=pl.BlockSpec(memory_space=pl.ANY),
        scratch_shapes=[pltpu.VMEM((d, ff), BF16), pltpu.VMEM((d, ff), BF16), pltpu.VMEM((ff, d), BF16),
                        row_buf, row_buf, row_buf, row_buf,
                        pltpu.SemaphoreType.DMA((2,)), pltpu.SemaphoreType.DMA((2,)),
                        pltpu.SemaphoreType.DMA(())],
    )
    return pl.pallas_call(
        functools.partial(_expert_kernel, n_sub=n_sub, n_tok=n_tok, n_blocks=n_blocks, trash_base=trash_base),
        grid_spec=grid_spec,
        out_shape=jax.ShapeDtypeStruct(((trash_base + 2 * bm) * n_sub, LANE), F32),
        compiler_params=_cparams("arbitrary"),
        name="experts",
    )(blk_expert, blk_win, blk_rem, blk_valid, n_active, windows, windows, windows, h2p, exp_gate, exp_up,
      exp_down)


def _combine_kernel(*refs, tmc, n_out):
    yk_refs = refs[:TOP_K]
    wt_ref, h2_ref, x1_ref, g2_ref, sg_ref, su_ref, sd_ref, o_ref, acc_scr = refs[TOP_K:]
    h = h2_ref[...]
    g = jnp.dot(h, sg_ref[...], preferred_element_type=F32)
    u = jnp.dot(h, su_ref[...], preferred_element_type=F32)
    shared = jnp.dot((_silu(g) * u).astype(BF16), sd_ref[...], preferred_element_type=F32)
    unroll = 4

    @pl.loop(0, tmc // unroll)
    def _(q):
        for j in range(unroll):
            t = q * unroll + j
            rows = pl.ds(pl.multiple_of(t * n_out, n_out), n_out)
            acc = wt_ref[0, t] * yk_refs[0][rows, :]
            for k in range(1, TOP_K):
                acc = acc + wt_ref[k, t] * yk_refs[k][rows, :]
            acc_scr[rows, :] = acc

    routed = jnp.concatenate([acc_scr[pl.ds(s, tmc, stride=n_out), :] for s in range(n_out)], axis=1)
    o_ref[...] = x1_ref[...] + g2_ref[...] * (routed + shared)


def _combine(ytk, wts, h2, x1, g2, sg, su, sd, seq, tmc):
    m, d = x1.shape
    ff = sg.shape[1]
    n_out = d // LANE
    per_b = seq // tmc
    tiles = m // tmc
    row = pl.BlockSpec((tmc, d), lambda i: (i, 0))
    yk_specs = [pl.BlockSpec((tmc * n_out, LANE), lambda i, k=k: (k * tiles + i, 0)) for k in range(TOP_K)]
    return pl.pallas_call(
        functools.partial(_combine_kernel, tmc=tmc, n_out=n_out),
        grid=(tiles,),
        in_specs=yk_specs + [pl.BlockSpec((TOP_K, tmc), lambda i: (0, i), memory_space=pltpu.SMEM),
                             row, row,
                             pl.BlockSpec((None, 1, d), lambda i: (i // per_b, 0, 0)),
                             pl.BlockSpec((d, ff), lambda i: (0, 0)),
                             pl.BlockSpec((d, ff), lambda i: (0, 0)),
                             pl.BlockSpec((ff, d), lambda i: (0, 0))],
        out_specs=row,
        out_shape=jax.ShapeDtypeStruct((m, d), F32),
        scratch_shapes=[pltpu.VMEM((tmc * n_out, LANE), F32)],
        compiler_params=_cparams("arbitrary"),
        name="combine",
    )(*([ytk] * TOP_K), wts, h2, x1, g2, sg, su, sd)


def kernel(x, c, ctx, c_ctx, mod_w, mod_b, norm1_w, w_in, q_norm_w, k_norm_w, w_fourier_out, w_attn_out, w_out,
           norm2_w, router_w, router_b, exp_gate, exp_up, exp_down, shared_gate, shared_up, shared_down):
    batch, seq, d = x.shape
    n_ctx = ctx.shape[1]
    assert mod_w.shape[0] == 1, "single-layer block only"
    assert batch + 1 <= 8 and seq % GRID_W == 0
    fw = w_fourier_out.shape[1]
    att = w_attn_out.shape[1]
    in_cols = w_in.shape[2]
    kvw = (in_cols - fw - att - 2 * d) // 2
    n_kv = kvw // HEAD_DIM
    n_exp = router_w.shape[2]
    assert att == n_kv * Q_GROUP * HEAD_DIM and n_exp <= LANE
    q_off, k_off = fw, fw + att
    v_off = k_off + kvw
    gf_off = v_off + kvw
    ga_off = gf_off + d
    m = batch * seq
    assert q_off % (Q_GROUP * HEAD_DIM) == 0 and kvw % HEAD_DIM == 0 and fw % F_GROUP_DIM == 0
    assert gf_off % min(512, d) == 0 and d % (2 * LANE) == 0

    cond = jnp.concatenate([c, c_ctx[None], jnp.zeros((8 - batch - 1, d), F32)], axis=0)
    mod = _adaln(cond, mod_w[0], mod_b[0])
    sh1, sc1, g1, sh2, sc2, g2 = [mod[:batch, j * d:(j + 1) * d].reshape(batch, 1, d) for j in range(6)]
    csh1 = jnp.broadcast_to(mod[batch, 0:d], (batch, 1, d))
    csc1 = jnp.broadcast_to(mod[batch, d:2 * d], (batch, 1, d))

    cos_t, sin_t = _rope_tables(seq)
    tn = min(512, kvw)
    tm = min(1024, seq)
    modes = (("plain", 0, q_off // tn), ("q", q_off // tn, k_off // tn), ("k", k_off // tn, v_off // tn),
             ("plain", v_off // tn, gf_off // tn), ("sigmoid", gf_off // tn, in_cols // tn))
    proj = _inproj(x.reshape(m, d), sh1, sc1, norm1_w[0], w_in[0], q_norm_w[0], k_norm_w[0], cos_t, sin_t,
                   modes, seq, tm, tn)
    cmodes = (("k_norope", 0, kvw // tn), ("plain", kvw // tn, 2 * kvw // tn))
    tmc_ctx = min(n_ctx, 256)
    kvc = _inproj(ctx.reshape(batch * n_ctx, d), csh1, csc1, norm1_w[0], w_in[0], q_norm_w[0],
                  k_norm_w[0], cos_t[:tmc_ctx], sin_t[:tmc_ctx], cmodes, n_ctx, tmc_ctx, tn, col0=k_off,
                  n_cols=2 * kvw)

    ya = _attention(proj, kvc, batch, seq, n_ctx, n_kv, q_off, k_off, v_off, tq=min(512, seq), tk=min(512, seq))
    yf = _fourier_mix(proj, batch, seq, fw)
    y = _merge(yf, ya, w_fourier_out[0].astype(BF16), w_attn_out[0].astype(BF16), proj, gf_off, ga_off,
               tm=min(1024, seq), tn=min(512, d))

    rw_pad = jnp.pad(router_w[0], ((0, 0), (0, LANE - n_exp)))
    rw_hi = rw_pad.astype(BF16)
    rw_pad = jnp.concatenate([rw_hi, (rw_pad - rw_hi.astype(F32)).astype(BF16)], axis=1)
    x1, h2, h2p, eidx, wts, cnt = _outproj_route(
        y, w_out[0].astype(BF16), x.reshape(m, d), g1, norm2_w[0], sh2, sc2, rw_pad, router_b[0], seq, n_exp,
        tm=min(512, seq))

    n_assign = m * TOP_K
    assert n_assign % EXPERT_BLOCK == 0
    _, order = lax.sort((eidx.reshape(-1), jnp.arange(n_assign, dtype=I32)), num_keys=1)
    counts = cnt[:, 0]
    starts = jnp.cumsum(counts) - counts
    nb_e = (counts + EXPERT_BLOCK - 1) // EXPERT_BLOCK
    cum_nb = jnp.cumsum(nb_e)
    n_blocks = n_assign // EXPERT_BLOCK + n_exp
    n_active = jnp.maximum(cum_nb[-1], 1).astype(I32)
    blk = jnp.minimum(jnp.arange(n_blocks, dtype=I32), n_active - 1)
    blk_expert = jnp.minimum(jnp.sum(blk[:, None] >= cum_nb[None, :], axis=1), n_exp - 1).astype(I32)
    is_e = blk_expert[:, None] == jnp.arange(n_exp, dtype=I32)[None, :]
    lookup = lambda table: jnp.sum(jnp.where(is_e, table[None, :], 0), axis=1)
    blk_j = blk - lookup(cum_nb - nb_e)
    blk_off = (lookup(starts) + blk_j * EXPERT_BLOCK).astype(I32)
    blk_valid = jnp.clip(lookup(counts) - blk_j * EXPERT_BLOCK, 0, EXPERT_BLOCK).astype(I32)

    ytk = _experts(h2p, order, blk_expert, blk_off, blk_valid, n_active.reshape(1), exp_gate[0], exp_up[0],
                   exp_down[0], n_blocks, m)
    out = _combine(ytk, wts, h2, x1, g2, shared_gate[0].astype(BF16), shared_up[0].astype(BF16),
                   shared_down[0].astype(BF16), seq, tmc=min(128, seq))
    return out.reshape(batch, seq, d)
```

```python
import functools
import math

import jax
import jax.numpy as jnp
from jax import lax
from jax.experimental import pallas as pl
from jax.experimental.pallas import tpu as pltpu

F32 = jnp.float32
BF16 = jnp.bfloat16
I32 = jnp.int32

GRID_W = 64
HEAD_DIM = 128
Q_GROUP = 4
ROPE_FREQS = HEAD_DIM // 4
ROPE_THETA = 10000.0
F_GROUP_DIM = 128
N_EXPERT_GROUPS = 8
TOPK_GROUPS = 4
TOP_K = 8
ROUTED_SCALE = 2.5
EPS = 1e-6
LANE = 128
EXPERT_BLOCK = 256
VMEM_LIMIT = 56 * 1024 * 1024


def _cparams(*sem):
    return pltpu.CompilerParams(dimension_semantics=sem, vmem_limit_bytes=VMEM_LIMIT)


def _silu(v):
    return v * jax.nn.sigmoid(v)


def _adaln_kernel(c_ref, w_ref, b_ref, o_ref):
    a = _silu(c_ref[...])
    o_ref[...] = jnp.dot(a.astype(BF16), w_ref[...].astype(BF16), preferred_element_type=F32) + b_ref[...]


def _adaln(cond_pad, mod_w, mod_b):
    d, n = mod_w.shape
    tn = min(n, 1024)
    return pl.pallas_call(
        _adaln_kernel,
        grid=(n // tn,),
        in_specs=[pl.BlockSpec((8, d), lambda j: (0, 0)),
                  pl.BlockSpec((d, tn), lambda j: (0, j)),
                  pl.BlockSpec((1, tn), lambda j: (0, j))],
        out_specs=pl.BlockSpec((8, tn), lambda j: (0, j)),
        out_shape=jax.ShapeDtypeStruct((8, n), F32),
        compiler_params=_cparams("arbitrary"),
        name="adaln",
    )(cond_pad, mod_w, mod_b.reshape(1, n))


def _head_norm_rope(a, w, cos, sin, scale):
    ms = jnp.mean(a * a, axis=-1, keepdims=True)
    a = a * lax.rsqrt(ms + EPS) * w
    if cos is not None:
        lane = lax.broadcasted_iota(I32, a.shape, 1)
        first = (lane % (2 * ROPE_FREQS)) < ROPE_FREQS
        partner = jnp.where(first, pltpu.roll(a, HEAD_DIM - ROPE_FREQS, 1), pltpu.roll(a, ROPE_FREQS, 1))
        a = a * cos + partner * sin
    if scale is not None:
        a = a * scale
    return a


def _inproj_kernel(x_ref, sh_ref, sc_ref, nw_ref, w_ref, qw_ref, kw_ref, cos_ref, sin_ref, o_ref, h_scr, wb_scr,
                   acc_scr, *, modes, tn):
    j = pl.program_id(1)

    @pl.when(j == 0)
    def _():
        x = x_ref[...]
        ms = jnp.mean(x * x, axis=-1, keepdims=True)
        y = x * lax.rsqrt(ms + EPS) * nw_ref[...]
        h_scr[...] = (y * (1.0 + sc_ref[...]) + sh_ref[...]).astype(BF16)

    wb_scr[...] = w_ref[...].astype(BF16)

    def tile(c0, c1):
        return jnp.dot(h_scr[...], wb_scr[:, c0:c1], preferred_element_type=F32)

    piece = 2 * HEAD_DIM if tn % (2 * HEAD_DIM) == 0 else HEAD_DIM
    gated = functools.reduce(jnp.logical_or, [(j >= j0) & (j < j1) for mode, j0, j1 in modes if mode == "sigmoid"],
                             jnp.bool_(False))

    @pl.when(gated)
    def _():
        for c0 in range(0, tn, piece):
            o_ref[:, c0:c0 + piece] = jax.nn.sigmoid(tile(c0, c0 + piece)).astype(BF16)

    @pl.when(jnp.logical_not(gated))
    def _():
        acc_scr[...] = tile(0, tn)

    for mode, j0, j1 in modes:
        if mode == "sigmoid":
            continue

        @pl.when((j >= j0) & (j < j1))
        def _(mode=mode):
            if mode == "plain":
                o_ref[...] = acc_scr[...].astype(BF16)
                return
            rope = mode in ("q", "k")
            w = qw_ref[...] if mode == "q" else kw_ref[...]
            scale = HEAD_DIM ** -0.5 * math.log2(math.e) if mode == "q" else None
            for h in range(tn // HEAD_DIM):
                sl = slice(h * HEAD_DIM, (h + 1) * HEAD_DIM)
                a = _head_norm_rope(acc_scr[:, sl], w, cos_ref[...] if rope else None,
                                    sin_ref[...] if rope else None, scale)
                o_ref[:, sl] = a.astype(BF16)


def _inproj(x2d, shift, scale, norm_w, w_in, q_norm_w, k_norm_w, cos_t, sin_t, modes, seq, tm, tn, col0=0,
            n_cols=None):
    m, d = x2d.shape
    n = w_in.shape[1] if n_cols is None else n_cols
    cb0 = col0 // tn
    per_b = seq // tm
    pos_blocks = cos_t.shape[0] // tm
    kern = functools.partial(_inproj_kernel, modes=modes, tn=tn)
    return pl.pallas_call(
        kern,
        grid=(m // tm, n // tn),
        in_specs=[pl.BlockSpec((tm, d), lambda i, j: (i, 0)),
                  pl.BlockSpec((None, 1, d), lambda i, j: (i // per_b, 0, 0)),
                  pl.BlockSpec((None, 1, d), lambda i, j: (i // per_b, 0, 0)),
                  pl.BlockSpec((1, d), lambda i, j: (0, 0)),
                  pl.BlockSpec((d, tn), lambda i, j: (0, cb0 + j)),
                  pl.BlockSpec((1, HEAD_DIM), lambda i, j: (0, 0)),
                  pl.BlockSpec((1, HEAD_DIM), lambda i, j: (0, 0)),
                  pl.BlockSpec((tm, HEAD_DIM), lambda i, j: (i % pos_blocks, 0)),
                  pl.BlockSpec((tm, HEAD_DIM), lambda i, j: (i % pos_blocks, 0))],
        out_specs=pl.BlockSpec((tm, tn), lambda i, j: (i, j)),
        out_shape=jax.ShapeDtypeStruct((m, n), BF16),
        scratch_shapes=[pltpu.VMEM((tm, d), BF16), pltpu.VMEM((d, tn), BF16), pltpu.VMEM((tm, tn), F32)],
        compiler_params=_cparams("arbitrary", "arbitrary"),
        name="inproj",
    )(x2d, shift, scale, norm_w.reshape(1, d), w_in, q_norm_w.reshape(1, HEAD_DIM),
      k_norm_w.reshape(1, HEAD_DIM), cos_t, sin_t)


def _rope_tables(seq):
    rows = seq // GRID_W
    row = jnp.repeat(jnp.arange(rows), GRID_W)
    col = jnp.tile(jnp.arange(GRID_W), rows)
    pos = jnp.stack([row, col], axis=-1).astype(F32)
    inv_freq = ROPE_THETA ** (-jnp.arange(ROPE_FREQS, dtype=F32) / ROPE_FREQS)
    ang = pos[:, :, None] * inv_freq
    cos, sin = jnp.cos(ang), jnp.sin(ang)
    cos_t = jnp.concatenate([cos[:, 0], cos[:, 0], cos[:, 1], cos[:, 1]], axis=-1)
    sin_t = jnp.concatenate([-sin[:, 0], sin[:, 0], -sin[:, 1], sin[:, 1]], axis=-1)
    return cos_t, sin_t


def _attn_kernel(q_ref, k_ref, v_ref, kc_ref, vc_ref, o_ref, vt_scr, vct_scr, *, tq, tk, n_chunks):
    @pl.when(pl.program_id(2) == 0)
    def _():
        for c in range(n_chunks):
            vt_scr[c] = jnp.transpose(v_ref[c * tk:(c + 1) * tk, :].astype(F32)).astype(BF16)
        vct_scr[...] = jnp.transpose(vc_ref[...].astype(F32)).astype(BF16)

    q = jnp.concatenate([q_ref[:, h * HEAD_DIM:(h + 1) * HEAD_DIM] for h in range(Q_GROUP)], axis=0)
    qt = jnp.transpose(q.astype(F32)).astype(BF16)
    m_cols = Q_GROUP * tq

    def scores(kb):
        return jnp.dot(kb, qt, preferred_element_type=F32)

    def update(st, vtb, carry):
        m, l, acc = carry
        m_new = jnp.maximum(m, jnp.max(st, axis=0, keepdims=True))
        alpha = jnp.exp2(m - m_new)
        pt = jnp.exp2(st - m_new)
        l = alpha * l + jnp.sum(pt, axis=0, keepdims=True)
        acc = alpha * acc + jnp.dot(vtb, pt.astype(BF16), preferred_element_type=F32)
        return m_new, l, acc

    carry = (jnp.full((1, m_cols), -jnp.inf, F32), jnp.zeros((1, m_cols), F32),
             jnp.zeros((HEAD_DIM, m_cols), F32))
    st = scores(k_ref[0:tk, :])
    for c in range(n_chunks):
        st_next = scores(k_ref[(c + 1) * tk:(c + 2) * tk, :]) if c + 1 < n_chunks else scores(kc_ref[...])
        carry = update(st, vt_scr[c], carry)
        st = st_next
    _, l, acc = update(st, vct_scr[...], carry)
    o = jnp.transpose(acc / l)
    for h in range(Q_GROUP):
        o_ref[:, h * HEAD_DIM:(h + 1) * HEAD_DIM] = o[h * tq:(h + 1) * tq].astype(BF16)


def _attention(proj, kvc, batch, seq, n_ctx, n_kv, q_off, k_off, v_off, tq, tk):
    gw = Q_GROUP * HEAD_DIM
    per_b = seq // tq
    n_chunks = seq // tk
    kern = functools.partial(_attn_kernel, tq=tq, tk=tk, n_chunks=n_chunks)
    return pl.pallas_call(
        kern,
        grid=(batch, n_kv, per_b),
        in_specs=[pl.BlockSpec((tq, gw), lambda b, g, i: (b * per_b + i, q_off // gw + g)),
                  pl.BlockSpec((seq, HEAD_DIM), lambda b, g, i: (b, k_off // HEAD_DIM + g)),
                  pl.BlockSpec((seq, HEAD_DIM), lambda b, g, i: (b, v_off // HEAD_DIM + g)),
                  pl.BlockSpec((n_ctx, HEAD_DIM), lambda b, g, i: (b, g)),
                  pl.BlockSpec((n_ctx, HEAD_DIM), lambda b, g, i: (b, n_kv + g))],
        out_specs=pl.BlockSpec((tq, gw), lambda b, g, i: (b * per_b + i, g)),
        out_shape=jax.ShapeDtypeStruct((batch * seq, n_kv * gw), BF16),
        scratch_shapes=[pltpu.VMEM((n_chunks, HEAD_DIM, tk), BF16), pltpu.VMEM((HEAD_DIM, n_ctx), BF16)],
        compiler_params=_cparams("arbitrary", "arbitrary", "arbitrary"),
        name="attention",
    )(proj, proj, proj, kvc, kvc)


def _dft_mats(n_rows, n_cols):
    def cs(k, n):
        ang = (2.0 * math.pi / n) * (k % n).astype(F32)
        return jnp.cos(ang), jnp.sin(ang)

    ch = jnp.arange(F_GROUP_DIM)
    cc, sc = cs(ch[:, None] * ch[None, :], F_GROUP_DIM)
    mat_a = jnp.concatenate([cc, -sc], axis=1).astype(BF16)
    r = jnp.arange(n_rows)
    cr, sr = cs(r[:, None] * r[None, :], n_rows)
    mat_1 = jnp.concatenate([jnp.concatenate([cr, sr], axis=1),
                             jnp.concatenate([-sr, cr], axis=1)], axis=0).astype(BF16)
    c = jnp.arange(n_cols)
    c2, s2 = cs(c[:, None] * c[None, :], n_cols)
    mat_2 = jnp.concatenate([c2, s2], axis=1).astype(BF16)
    tr, ts = cs(r[:, None] * c[None, :], n_rows * n_cols)
    tw_r = jnp.repeat(tr, LANE, axis=1)
    tw_i = jnp.repeat(-ts, LANE, axis=1)
    return mat_a, mat_1, mat_2, tw_r, tw_i


def _fourier_a_kernel(u_ref, m_ref, vr_ref, vi_ref, *, groups):
    for g in range(groups):
        sl = slice(g * F_GROUP_DIM, (g + 1) * F_GROUP_DIM)
        r = jnp.dot(u_ref[:, sl], m_ref[...], preferred_element_type=F32)
        vr_ref[:, sl] = r[:, :F_GROUP_DIM].astype(BF16)
        vi_ref[:, sl] = r[:, F_GROUP_DIM:].astype(BF16)


def _fourier_1_kernel(ar_ref, ai_ref, m_ref, twr_ref, twi_ref, pr_ref, pi_ref, *, nseg, fw, n_rows):
    rhs = jnp.concatenate([ar_ref[...], ai_ref[...]], axis=0)
    z = jnp.dot(m_ref[...], rhs, preferred_element_type=F32)
    reps = fw // LANE
    for s in range(nseg):
        zr = z[:n_rows, s * fw:(s + 1) * fw]
        zi = z[n_rows:, s * fw:(s + 1) * fw]
        tr = jnp.tile(twr_ref[:, s * LANE:(s + 1) * LANE], (1, reps))
        ti = jnp.tile(twi_ref[:, s * LANE:(s + 1) * LANE], (1, reps))
        pr_ref[s] = (zr * tr - zi * ti).astype(BF16)
        pi_ref[s] = (zr * ti + zi * tr).astype(BF16)


def _fourier_2_kernel(ar_ref, ai_ref, m_ref, o_ref, *, norm):
    rhs = jnp.concatenate([ar_ref[...], ai_ref[...]], axis=0)
    o_ref[...] = (jnp.dot(m_ref[...], rhs, preferred_element_type=F32) * norm).astype(BF16)


def _fourier_mix(proj, batch, seq, fw):
    n_cols = GRID_W
    n_rows = seq // GRID_W
    groups = fw // F_GROUP_DIM
    mat_a, mat_1, mat_2, tw_r, tw_i = _dft_mats(n_rows, n_cols)
    m = batch * seq
    tm = min(seq, 1024)
    vr, vi = pl.pallas_call(
        functools.partial(_fourier_a_kernel, groups=groups),
        grid=(m // tm,),
        in_specs=[pl.BlockSpec((tm, fw), lambda i: (i, 0)),
                  pl.BlockSpec((F_GROUP_DIM, 2 * F_GROUP_DIM), lambda i: (0, 0))],
        out_specs=[pl.BlockSpec((tm, fw), lambda i: (i, 0))] * 2,
        out_shape=[jax.ShapeDtypeStruct((m, fw), BF16)] * 2,
        compiler_params=_cparams("arbitrary"),
        name="fourier_channels",
    )(proj, mat_a)
    wide = n_cols * fw
    vr = vr.reshape(batch * n_rows, wide)
    vi = vi.reshape(batch * n_rows, wide)
    nseg = min(n_cols, 4)
    pr, pi = pl.pallas_call(
        functools.partial(_fourier_1_kernel, nseg=nseg, fw=fw, n_rows=n_rows),
        grid=(batch, n_cols // nseg),
        in_specs=[pl.BlockSpec((n_rows, nseg * fw), lambda b, j: (b, j)),
                  pl.BlockSpec((n_rows, nseg * fw), lambda b, j: (b, j)),
                  pl.BlockSpec((2 * n_rows, 2 * n_rows), lambda b, j: (0, 0)),
                  pl.BlockSpec((n_rows, nseg * LANE), lambda b, j: (0, j)),
                  pl.BlockSpec((n_rows, nseg * LANE), lambda b, j: (0, j))],
        out_specs=[pl.BlockSpec((nseg, n_rows, fw), lambda b, j: (b * (n_cols // nseg) + j, 0, 0))] * 2,
        out_shape=[jax.ShapeDtypeStruct((batch * n_cols, n_rows, fw), BF16)] * 2,
        compiler_params=_cparams("arbitrary", "arbitrary"),
        name="fourier_rows",
    )(vr, vi, mat_1, tw_r, tw_i)
    wide2 = n_rows * fw
    pr = pr.reshape(batch * n_cols, wide2)
    pi = pi.reshape(batch * n_cols, wide2)
    tc = min(wide2, 8192)
    norm = 1.0 / math.sqrt(seq * F_GROUP_DIM)
    yf = pl.pallas_call(
        functools.partial(_fourier_2_kernel, norm=norm),
        grid=(batch, wide2 // tc),
        in_specs=[pl.BlockSpec((n_cols, tc), lambda b, j: (b, j)),
                  pl.BlockSpec((n_cols, tc), lambda b, j: (b, j)),
                  pl.BlockSpec((n_cols, 2 * n_cols), lambda b, j: (0, 0))],
        out_specs=pl.BlockSpec((n_cols, tc), lambda b, j: (b, j)),
        out_shape=jax.ShapeDtypeStruct((batch * n_cols, wide2), BF16),
        compiler_params=_cparams("arbitrary", "arbitrary"),
        name="fourier_cols",
    )(pr, pi, mat_2)
    return yf.reshape(m, fw)


def _merge_kernel(yf_ref, ya_ref, wfo_ref, wao_ref, gf_ref, ga_ref, o_ref):
    a = jnp.dot(yf_ref[...], wfo_ref[...], preferred_element_type=F32)
    b = jnp.dot(ya_ref[...], wao_ref[...], preferred_element_type=F32)
    o_ref[...] = (gf_ref[...].astype(F32) * a + ga_ref[...].astype(F32) * b).astype(BF16)


def _merge(yf, ya, wfo, wao, proj, gf_off, ga_off, tm, tn):
    m, fw = yf.shape
    aw = ya.shape[1]
    d = wfo.shape[1]
    return pl.pallas_call(
        _merge_kernel,
        grid=(m // tm, d // tn),
        in_specs=[pl.BlockSpec((tm, fw), lambda i, j: (i, 0)),
                  pl.BlockSpec((tm, aw), lambda i, j: (i, 0)),
                  pl.BlockSpec((fw, tn), lambda i, j: (0, j)),
                  pl.BlockSpec((aw, tn), lambda i, j: (0, j)),
                  pl.BlockSpec((tm, tn), lambda i, j: (i, gf_off // tn + j)),
                  pl.BlockSpec((tm, tn), lambda i, j: (i, ga_off // tn + j))],
        out_specs=pl.BlockSpec((tm, tn), lambda i, j: (i, j)),
        out_shape=jax.ShapeDtypeStruct((m, d), BF16),
        compiler_params=_cparams("arbitrary", "arbitrary"),
        name="merge",
    )(yf, ya, wfo, wao, proj, proj)


def _route(logits_t, bias, n_exp):
    gsz = n_exp // N_EXPERT_GROUPS
    tm = logits_t.shape[1]
    neg = -jnp.inf
    scores = jax.nn.sigmoid(logits_t)
    biased = scores + bias
    io_g = lax.broadcasted_iota(I32, (gsz, tm), 0).astype(F32)
    gs = []
    for g in range(N_EXPERT_GROUPS):
        grp = biased[g * gsz:(g + 1) * gsz, :]
        m1 = jnp.max(grp, axis=0, keepdims=True)
        i1 = jnp.min(jnp.where(grp == m1, io_g, float(gsz)), axis=0, keepdims=True)
        m2 = jnp.max(jnp.where(io_g == i1, neg, grp), axis=0, keepdims=True)
        gs.append(m1 + m2)
    cur = jnp.concatenate(gs, axis=0)
    io_n = lax.broadcasted_iota(I32, (N_EXPERT_GROUPS, tm), 0).astype(F32)
    sel = jnp.zeros((N_EXPERT_GROUPS, tm), F32)
    for _ in range(TOPK_GROUPS):
        mx = jnp.max(cur, axis=0, keepdims=True)
        ix = jnp.min(jnp.where(cur == mx, io_n, float(N_EXPERT_GROUPS)), axis=0, keepdims=True)
        hit = io_n == ix
        sel = jnp.where(hit, 1.0, sel)
        cur = jnp.where(hit, neg, cur)
    cur = jnp.concatenate([jnp.where(sel[g:g + 1, :] > 0.0, biased[g * gsz:(g + 1) * gsz, :], neg)
                           for g in range(N_EXPERT_GROUPS)], axis=0)
    io_e = lax.broadcasted_iota(I32, (n_exp, tm), 0).astype(F32)
    eidx, wts = [], []
    for _ in range(TOP_K):
        mx = jnp.max(cur, axis=0, keepdims=True)
        ix = jnp.min(jnp.where(cur == mx, io_e, float(n_exp)), axis=0, keepdims=True)
        hit = io_e == ix
        wts.append(jnp.sum(jnp.where(hit, scores, 0.0), axis=0, keepdims=True))
        eidx.append(ix)
        cur = jnp.where(hit, neg, cur)
    eidx = jnp.concatenate(eidx, axis=0).astype(I32)
    wts = jnp.concatenate(wts, axis=0)
    wts = wts / jnp.sum(wts, axis=0, keepdims=True) * ROUTED_SCALE
    return eidx, wts


def _outproj_kernel(y_ref, wo_ref, x_ref, g1_ref, nw_ref, sh_ref, sc_ref, rw_ref, rb_ref,
                    x1_ref, h2_ref, h2p_ref, eidx_ref, wts_ref, cnt_ref, carry_scr, *, n_exp):
    i = pl.program_id(0)
    tm, d = x_ref.shape
    n_sub = d // LANE

    @pl.when(i == 0)
    def _():
        carry_scr[...] = jnp.zeros_like(carry_scr)

    half = tm // 2 if tm % (2 * LANE) == 0 else tm
    added = jnp.zeros((n_exp, 1), F32)
    for r0 in range(0, tm, half):
        rs = slice(r0, r0 + half)
        x1 = x_ref[rs, :] + g1_ref[...] * jnp.dot(y_ref[rs, :], wo_ref[...], preferred_element_type=F32)
        x1_ref[rs, :] = x1
        ms = jnp.mean(x1 * x1, axis=-1, keepdims=True)
        h2 = (x1 * lax.rsqrt(ms + EPS) * nw_ref[...]) * (1.0 + sc_ref[...]) + sh_ref[...]
        h2b = h2.astype(BF16)
        h2_ref[rs, :] = h2b
        for s in range(n_sub):
            h2p_ref[pl.ds(r0 * n_sub + s, half, stride=n_sub), :] = h2[:, s * LANE:(s + 1) * LANE]
        h_lo = (h2 - h2b.astype(F32)).astype(BF16)
        part = (jnp.dot(h2b, rw_ref[...], preferred_element_type=F32)
                + jnp.dot(h_lo, rw_ref[...], preferred_element_type=F32))
        logits = part[:, :LANE] + part[:, LANE:]
        logits_t = jnp.transpose(logits)[:n_exp, :]
        eidx, wts = _route(logits_t, rb_ref[...], n_exp)
        eidx_ref[:, rs] = eidx
        wts_ref[:, rs] = wts
        io_e = lax.broadcasted_iota(I32, (n_exp, half), 0)
        onehot = jnp.zeros((n_exp, half), F32)
        for k in range(TOP_K):
            onehot = onehot + jnp.where(io_e == eidx[k:k + 1, :], 1.0, 0.0)
        added = added + jnp.sum(onehot, axis=1, keepdims=True)
    carry_scr[...] = carry_scr[...] + added
    cnt_ref[...] = carry_scr[...].astype(I32)


def _outproj_route(y, wo, x2d, g1, norm_w, shift, scale, rw_pad, rb, seq, n_exp, tm):
    m, d = x2d.shape
    per_b = seq // tm
    n_sub = d // LANE
    bspec = pl.BlockSpec((None, 1, d), lambda i: (i // per_b, 0, 0))
    row = pl.BlockSpec((tm, d), lambda i: (i, 0))
    tok = pl.BlockSpec((TOP_K, tm), lambda i: (0, i))
    return pl.pallas_call(
        functools.partial(_outproj_kernel, n_exp=n_exp),
        grid=(m // tm,),
        in_specs=[row,
                  pl.BlockSpec((d, d), lambda i: (0, 0), pipeline_mode=pl.Buffered(1)),
                  row, bspec,
                  pl.BlockSpec((1, d), lambda i: (0, 0)),
                  bspec, bspec,
                  pl.BlockSpec((d, 2 * LANE), lambda i: (0, 0)),
                  pl.BlockSpec((n_exp, 1), lambda i: (0, 0))],
        out_specs=[row, row,
                   pl.BlockSpec((tm * n_sub, LANE), lambda i: (i, 0)),
                   tok, tok,
                   pl.BlockSpec((n_exp, LANE), lambda i: (0, 0))],
        out_shape=[jax.ShapeDtypeStruct((m, d), F32),
                   jax.ShapeDtypeStruct((m, d), BF16),
                   jax.ShapeDtypeStruct((m * n_sub, LANE), F32),
                   jax.ShapeDtypeStruct((TOP_K, m), I32),
                   jax.ShapeDtypeStruct((TOP_K, m), F32),
                   jax.ShapeDtypeStruct((n_exp, LANE), I32)],
        scratch_shapes=[pltpu.VMEM((n_exp, LANE), F32)],
        compiler_params=_cparams("arbitrary"),
        name="outproj_route",
    )(y, wo, x2d, g1, norm_w.reshape(1, d), shift, scale, rw_pad, rb.reshape(n_exp, 1))


def _expert_kernel(be_ref, bwin_ref, brem_ref, bval_ref, nact_ref, prev_win, cur_win, nxt_win,
                   h2p_hbm, wg_ref, wu_ref, wd_ref, ytk_hbm,
                   wg_s, wu_s, wd_s, xbuf0, xbuf1, ybuf0, ybuf1, sem_g, sem_s, sem_z,
                   *, n_sub, n_tok, n_blocks, trash_base):
    i = pl.program_id(0)
    nact = nact_ref[0]
    bm = EXPERT_BLOCK
    xbufs = (xbuf0, xbuf1)
    ybufs = (ybuf0, ybuf1)
    unroll = 8

    def token_of(f):
        return f & (n_tok - 1) if n_tok & (n_tok - 1) == 0 else lax.rem(f, n_tok)

    def gather_copy(tok, r, par):
        return pltpu.make_async_copy(h2p_hbm.at[pl.ds(pl.multiple_of(tok * n_sub, n_sub), n_sub), :],
                                     xbufs[par].at[pl.ds(pl.multiple_of(r * n_sub, n_sub), n_sub), :],
                                     sem_g.at[par])

    def scatter_copy(r, row, par):
        return pltpu.make_async_copy(ybufs[par].at[pl.ds(pl.multiple_of(r * n_sub, n_sub), n_sub), :],
                                     ytk_hbm.at[pl.ds(pl.multiple_of(row * n_sub, n_sub), n_sub), :],
                                     sem_s.at[par])

    def wait_gathers(par):
        @pl.loop(0, bm // unroll)
        def _(q):
            for u in range(unroll):
                gather_copy(0, 0, par).wait()

    def wait_scatters(par):
        @pl.loop(0, bm // unroll)
        def _(q):
            for u in range(unroll):
                scatter_copy(0, 0, par).wait()

    def scatter_row(win_ref, rem, valid, r, par):
        row = jnp.where(r < valid, win_ref[0, rem + r], trash_base + par * bm + r)
        return scatter_copy(r, row, par)

    @pl.when(i == 0)
    def _():
        ybuf0[...] = jnp.zeros_like(ybuf0)
        ybuf1[...] = jnp.zeros_like(ybuf1)
        for par in range(2):
            start = (trash_base + par * bm) * n_sub
            cp = pltpu.make_async_copy(ybuf0, ytk_hbm.at[pl.ds(start, bm * n_sub), :], sem_z)
            cp.start()
            cp.wait()
        rem0 = brem_ref[0]

        @pl.loop(0, bm // unroll)
        def _(q):
            for u in range(unroll):
                r = q * unroll + u
                gather_copy(token_of(cur_win[0, rem0 + r]), r, 0).start(priority=u % 2)

    def block_step(par):
        wait_gathers(par)

        @pl.when(i >= 1)
        def _():
            wait_scatters(par)

        @pl.when((i == 0) | (be_ref[i] != be_ref[jnp.maximum(i - 1, 0)]))
        def _():
            wg_s[...] = wg_ref[...].astype(BF16)
            wu_s[...] = wu_ref[...].astype(BF16)
            wd_s[...] = wd_ref[...].astype(BF16)

        nxt_rem = brem_ref[jnp.minimum(i + 1, n_blocks - 1)]
        prv = jnp.maximum(i - 1, 0)
        prv_rem = brem_ref[prv]
        prv_valid = jnp.where(i >= 1, bval_ref[prv], 0)
        for r in range(bm):
            gather_copy(token_of(nxt_win[0, nxt_rem + r]), r, 1 - par).start(priority=r % 2)
        for r in range(bm):
            scatter_row(prev_win, prv_rem, prv_valid, r, 1 - par).start(priority=r % 2)

        x = jnp.concatenate([xbufs[par][pl.ds(s, bm, stride=n_sub), :] for s in range(n_sub)],
                            axis=1).astype(BF16)
        g = jnp.dot(x, wg_s[...], preferred_element_type=F32)
        u = jnp.dot(x, wu_s[...], preferred_element_type=F32)
        y = jnp.dot((_silu(g) * u).astype(BF16), wd_s[...], preferred_element_type=F32)
        for s in range(n_sub):
            ybufs[par][pl.ds(s, bm, stride=n_sub), :] = y[:, s * LANE:(s + 1) * LANE]

        @pl.when(i == nact - 1)
        def _():
            wait_gathers(1 - par)
            wait_scatters(1 - par)
            rem = brem_ref[i]
            valid = bval_ref[i]

            @pl.loop(0, bm // unroll)
            def _(q):
                for u_ in range(unroll):
                    scatter_row(cur_win, rem, valid, q * unroll + u_, par).start(priority=u_ % 2)

            wait_scatters(par)

    for par in range(2):
        @pl.when((i < nact) & (lax.rem(i, 2) == par))
        def _(par=par):
            block_step(par)


def _experts(h2p, order, blk_expert, blk_off, blk_valid, n_active, exp_gate, exp_up, exp_down, n_blocks, n_tok):
    n_exp, d, ff = exp_gate.shape
    n_sub = d // LANE
    bm = EXPERT_BLOCK
    n_win = order.shape[0] // bm
    pieces = order.reshape(n_win, 1, bm)
    windows = jnp.concatenate([pieces, jnp.roll(pieces, -1, axis=0)], axis=-1)
    blk_win = blk_off // bm
    blk_rem = blk_off - blk_win * bm
    trash_base = TOP_K * n_tok

    def win(step_shift):
        def index_map(i, be, bwin, *_):
            return (bwin[jnp.clip(i + step_shift, 0, n_blocks - 1)], 0, 0)
        return pl.BlockSpec((None, 1, 2 * bm), index_map, memory_space=pltpu.SMEM)

    row_buf = pltpu.VMEM((bm * n_sub, LANE), F32)
    grid_spec = pltpu.PrefetchScalarGridSpec(
        num_scalar_prefetch=5,
        grid=(n_blocks,),
        in_specs=[win(-1), win(0), win(1),
                  pl.BlockSpec(memory_space=pl.ANY),
                  pl.BlockSpec((None, d, ff), lambda i, be, *_: (be[i], 0, 0)),
                  pl.BlockSpec((None, d, ff), lambda i, be, *_: (be[i], 0, 0)),
                  pl.BlockSpec((None, ff, d), lambda i, be, *_: (be[i], 0, 0))],
        out_specs=pl.BlockSpec(memory_space=pl.ANY),
        scratch_shapes=[pltpu.VMEM((d, ff), BF16), pltpu.VMEM((d, ff), BF16), pltpu.VMEM((ff, d), BF16),
                        row_buf, row_buf, row_buf, row_buf,
                        pltpu.SemaphoreType.DMA((2,)), pltpu.SemaphoreType.DMA((2,)),
                        pltpu.SemaphoreType.DMA(())],
    )
    return pl.pallas_call(
        functools.partial(_expert_kernel, n_sub=n_sub, n_tok=n_tok, n_blocks=n_blocks, trash_base=trash_base),
        grid_spec=grid_spec,
        out_shape=jax.ShapeDtypeStruct(((trash_base + 2 * bm) * n_sub, LANE), F32),
        compiler_params=_cparams("arbitrary"),
        name="experts",
    )(blk_expert, blk_win, blk_rem, blk_valid, n_active, windows, windows, windows, h2p, exp_gate, exp_up,
      exp_down)


def _combine_kernel(*refs, tmc, n_out):
    yk_refs = refs[:TOP_K]
    wt_ref, h2_ref, x1_ref, g2_ref, sg_ref, su_ref, sd_ref, o_ref, acc_scr = refs[TOP_K:]
    h = h2_ref[...]
    g = jnp.dot(h, sg_ref[...], preferred_element_type=F32)
    u = jnp.dot(h, su_ref[...], preferred_element_type=F32)
    shared = jnp.dot((_silu(g) * u).astype(BF16), sd_ref[...], preferred_element_type=F32)
    unroll = 4

    @pl.loop(0, tmc // unroll)
    def _(q):
        for j in range(unroll):
            t = q * unroll + j
            rows = pl.ds(pl.multiple_of(t * n_out, n_out), n_out)
            acc = wt_ref[0, t] * yk_refs[0][rows, :]
            for k in range(1, TOP_K):
                acc = acc + wt_ref[k, t] * yk_refs[k][rows, :]
            acc_scr[rows, :] = acc

    routed = jnp.concatenate([acc_scr[pl.ds(s, tmc, stride=n_out), :] for s in range(n_out)], axis=1)
    o_ref[...] = x1_ref[...] + g2_ref[...] * (routed + shared)


def _combine(ytk, wts, h2, x1, g2, sg, su, sd, seq, tmc):
    m, d = x1.shape
    ff = sg.shape[1]
    n_out = d // LANE
    per_b = seq // tmc
    tiles = m // tmc
    row = pl.BlockSpec((tmc, d), lambda i: (i, 0))
    yk_specs = [pl.BlockSpec((tmc * n_out, LANE), lambda i, k=k: (k * tiles + i, 0)) for k in range(TOP_K)]
    return pl.pallas_call(
        functools.partial(_combine_kernel, tmc=tmc, n_out=n_out),
        grid=(tiles,),
        in_specs=yk_specs + [pl.BlockSpec((TOP_K, tmc), lambda i: (0, i), memory_space=pltpu.SMEM),
                             row, row,
                             pl.BlockSpec((None, 1, d), lambda i: (i // per_b, 0, 0)),
                             pl.BlockSpec((d, ff), lambda i: (0, 0)),
                             pl.BlockSpec((d, ff), lambda i: (0, 0)),
                             pl.BlockSpec((ff, d), lambda i: (0, 0))],
        out_specs=row,
        out_shape=jax.ShapeDtypeStruct((m, d), F32),
        scratch_shapes=[pltpu.VMEM((tmc * n_out, LANE), F32)],
        compiler_params=_cparams("arbitrary"),
        name="combine",
    )(*([ytk] * TOP_K), wts, h2, x1, g2, sg, su, sd)


def _tiles(seq, d, kvw, n_ctx):
    return dict(
        inproj_tm=min(1024, seq), inproj_tn=min(512, kvw), ctx_tm=min(256, n_ctx),
        attn_tq=min(512, seq), attn_tk=min(512, seq),
        merge_tm=min(1024, seq), merge_tn=min(512, d),
        outproj_tm=min(512, seq), combine_tm=min(128, seq))


def kernel(x, c, ctx, c_ctx, mod_w, mod_b, norm1_w, w_in, q_norm_w, k_norm_w, w_fourier_out, w_attn_out, w_out,
           norm2_w, router_w, router_b, exp_gate, exp_up, exp_down, shared_gate, shared_up, shared_down):
    batch, seq, d = x.shape
    n_ctx = ctx.shape[1]
    assert mod_w.shape[0] == 1, "single-layer block only"
    assert batch + 1 <= 8 and seq % GRID_W == 0
    fw = w_fourier_out.shape[1]
    att = w_attn_out.shape[1]
    in_cols = w_in.shape[2]
    kvw = (in_cols - fw - att - 2 * d) // 2
    n_kv = kvw // HEAD_DIM
    n_exp = router_w.shape[2]
    assert att == n_kv * Q_GROUP * HEAD_DIM and n_exp <= LANE
    q_off, k_off = fw, fw + att
    v_off = k_off + kvw
    gf_off = v_off + kvw
    ga_off = gf_off + d
    m = batch * seq
    assert q_off % (Q_GROUP * HEAD_DIM) == 0 and kvw % HEAD_DIM == 0 and fw % F_GROUP_DIM == 0
    t = _tiles(seq, d, kvw, n_ctx)
    assert gf_off % t["merge_tn"] == 0 and d % (2 * LANE) == 0

    cond = jnp.concatenate([c, c_ctx[None], jnp.zeros((8 - batch - 1, d), F32)], axis=0)
    mod = _adaln(cond, mod_w[0], mod_b[0])
    sh1, sc1, g1, sh2, sc2, g2 = [mod[:batch, j * d:(j + 1) * d].reshape(batch, 1, d) for j in range(6)]
    csh1 = jnp.broadcast_to(mod[batch, 0:d], (batch, 1, d))
    csc1 = jnp.broadcast_to(mod[batch, d:2 * d], (batch, 1, d))

    cos_t, sin_t = _rope_tables(seq)
    tn = t["inproj_tn"]
    modes = (("plain", 0, q_off // tn), ("q", q_off // tn, k_off // tn), ("k", k_off // tn, v_off // tn),
             ("plain", v_off // tn, gf_off // tn), ("sigmoid", gf_off // tn, in_cols // tn))
    proj = _inproj(x.reshape(m, d), sh1, sc1, norm1_w[0], w_in[0], q_norm_w[0], k_norm_w[0], cos_t, sin_t,
                   modes, seq, t["inproj_tm"], tn)
    cmodes = (("k_norope", 0, kvw // tn), ("plain", kvw // tn, 2 * kvw // tn))
    tmc_ctx = t["ctx_tm"]
    kvc = _inproj(ctx.reshape(batch * n_ctx, d), csh1, csc1, norm1_w[0], w_in[0], q_norm_w[0],
                  k_norm_w[0], cos_t[:tmc_ctx], sin_t[:tmc_ctx], cmodes, n_ctx, tmc_ctx, tn, col0=k_off,
                  n_cols=2 * kvw)

    ya = _attention(proj, kvc, batch, seq, n_ctx, n_kv, q_off, k_off, v_off, tq=t["attn_tq"], tk=t["attn_tk"])
    yf = _fourier_mix(proj, batch, seq, fw)
    y = _merge(yf, ya, w_fourier_out[0].astype(BF16), w_attn_out[0].astype(BF16), proj, gf_off, ga_off,
               tm=t["merge_tm"], tn=t["merge_tn"])

    rw_pad = jnp.pad(router_w[0], ((0, 0), (0, LANE - n_exp)))
    rw_hi = rw_pad.astype(BF16)
    rw_pad = jnp.concatenate([rw_hi, (rw_pad - rw_hi.astype(F32)).astype(BF16)], axis=1)
    x1, h2, h2p, eidx, wts, cnt = _outproj_route(
        y, w_out[0].astype(BF16), x.reshape(m, d), g1, norm2_w[0], sh2, sc2, rw_pad, router_b[0], seq, n_exp,
        tm=t["outproj_tm"])

    n_assign = m * TOP_K
    assert n_assign % EXPERT_BLOCK == 0
    _, order = lax.sort((eidx.reshape(-1), jnp.arange(n_assign, dtype=I32)), num_keys=1)
    counts = cnt[:, 0]
    starts = jnp.cumsum(counts) - counts
    nb_e = (counts + EXPERT_BLOCK - 1) // EXPERT_BLOCK
    cum_nb = jnp.cumsum(nb_e)
    n_blocks = n_assign // EXPERT_BLOCK + n_exp
    n_active = jnp.maximum(cum_nb[-1], 1).astype(I32)
    blk = jnp.minimum(jnp.arange(n_blocks, dtype=I32), n_active - 1)
    blk_expert = jnp.minimum(jnp.sum(blk[:, None] >= cum_nb[None, :], axis=1), n_exp - 1).astype(I32)
    is_e = blk_expert[:, None] == jnp.arange(n_exp, dtype=I32)[None, :]
    lookup = lambda table: jnp.sum(jnp.where(is_e, table[None, :], 0), axis=1)
    blk_j = blk - lookup(cum_nb - nb_e)
    blk_off = (lookup(starts) + blk_j * EXPERT_BLOCK).astype(I32)
    blk_valid = jnp.clip(lookup(counts) - blk_j * EXPERT_BLOCK, 0, EXPERT_BLOCK).astype(I32)

    ytk = _experts(h2p, order, blk_expert, blk_off, blk_valid, n_active.reshape(1), exp_gate[0], exp_up[0],
                   exp_down[0], n_blocks, m)
    out = _combine(ytk, wts, h2, x1, g2, shared_gate[0].astype(BF16), shared_up[0].astype(BF16),
                   shared_down[0].astype(BF16), seq, tmc=t["combine_tm"])
    return out.reshape(batch, seq, d)
```

```python
import functools
import math

import jax
import jax.numpy as jnp
from jax import lax
from jax.experimental import pallas as pl
from jax.experimental.pallas import tpu as pltpu

F32 = jnp.float32
BF16 = jnp.bfloat16
I32 = jnp.int32

GRID_W = 64
HEAD_DIM = 128
Q_GROUP = 4
ROPE_FREQS = HEAD_DIM // 4
ROPE_THETA = 10000.0
F_GROUP_DIM = 128
N_EXPERT_GROUPS = 8
TOPK_GROUPS = 4
TOP_K = 8
ROUTED_SCALE = 2.5
EPS = 1e-6
LANE = 128
EXPERT_BLOCK = 256
VMEM_LIMIT = 56 * 1024 * 1024


def _cparams(*sem):
    return pltpu.CompilerParams(dimension_semantics=sem, vmem_limit_bytes=VMEM_LIMIT)


def _silu(v):
    return v * jax.nn.sigmoid(v)


def _adaln_kernel(c_ref, w_ref, b_ref, o_ref):
    a = _silu(c_ref[...])
    o_ref[...] = jnp.dot(a.astype(BF16), w_ref[...].astype(BF16), preferred_element_type=F32) + b_ref[...]


def _adaln(cond_pad, mod_w, mod_b):
    d, n = mod_w.shape
    tn = min(n, 1024)
    return pl.pallas_call(
        _adaln_kernel,
        grid=(n // tn,),
        in_specs=[pl.BlockSpec((8, d), lambda j: (0, 0)),
                  pl.BlockSpec((d, tn), lambda j: (0, j)),
                  pl.BlockSpec((1, tn), lambda j: (0, j))],
        out_specs=pl.BlockSpec((8, tn), lambda j: (0, j)),
        out_shape=jax.ShapeDtypeStruct((8, n), F32),
        compiler_params=_cparams("arbitrary"),
        name="adaln",
    )(cond_pad, mod_w, mod_b.reshape(1, n))


def _head_norm_rope(a, w, cos, sin, scale):
    ms = jnp.mean(a * a, axis=-1, keepdims=True)
    a = a * lax.rsqrt(ms + EPS) * w
    if cos is not None:
        lane = lax.broadcasted_iota(I32, a.shape, 1)
        first = (lane % (2 * ROPE_FREQS)) < ROPE_FREQS
        partner = jnp.where(first, pltpu.roll(a, HEAD_DIM - ROPE_FREQS, 1), pltpu.roll(a, ROPE_FREQS, 1))
        a = a * cos + partner * sin
    if scale is not None:
        a = a * scale
    return a


def _inproj_kernel(x_ref, sh_ref, sc_ref, nw_ref, w_ref, qw_ref, kw_ref, cos_ref, sin_ref, o_ref, h_scr, wb_scr,
                   acc_scr, *, modes, tn):
    j = pl.program_id(1)

    @pl.when(j == 0)
    def _():
        x = x_ref[...]
        ms = jnp.mean(x * x, axis=-1, keepdims=True)
        y = x * lax.rsqrt(ms + EPS) * nw_ref[...]
        h_scr[...] = (y * (1.0 + sc_ref[...]) + sh_ref[...]).astype(BF16)

    wb_scr[...] = w_ref[...].astype(BF16)

    def tile(c0, c1):
        return jnp.dot(h_scr[...], wb_scr[:, c0:c1], preferred_element_type=F32)

    piece = 2 * HEAD_DIM if tn % (2 * HEAD_DIM) == 0 else HEAD_DIM
    gated = functools.reduce(jnp.logical_or, [(j >= j0) & (j < j1) for mode, j0, j1 in modes if mode == "sigmoid"],
                             jnp.bool_(False))

    @pl.when(gated)
    def _():
        for c0 in range(0, tn, piece):
            o_ref[:, c0:c0 + piece] = jax.nn.sigmoid(tile(c0, c0 + piece)).astype(BF16)

    @pl.when(jnp.logical_not(gated))
    def _():
        acc_scr[...] = tile(0, tn)

    for mode, j0, j1 in modes:
        if mode == "sigmoid":
            continue

        @pl.when((j >= j0) & (j < j1))
        def _(mode=mode):
            if mode == "plain":
                o_ref[...] = acc_scr[...].astype(BF16)
                return
            rope = mode in ("q", "k")
            w = qw_ref[...] if mode == "q" else kw_ref[...]
            scale = HEAD_DIM ** -0.5 * math.log2(math.e) if mode == "q" else None
            for h in range(tn // HEAD_DIM):
                sl = slice(h * HEAD_DIM, (h + 1) * HEAD_DIM)
                a = _head_norm_rope(acc_scr[:, sl], w, cos_ref[...] if rope else None,
                                    sin_ref[...] if rope else None, scale)
                o_ref[:, sl] = a.astype(BF16)


def _inproj(x2d, shift, scale, norm_w, w_in, q_norm_w, k_norm_w, cos_t, sin_t, modes, seq, tm, tn, col0=0,
            n_cols=None):
    m, d = x2d.shape
    n = w_in.shape[1] if n_cols is None else n_cols
    cb0 = col0 // tn
    per_b = seq // tm
    pos_blocks = cos_t.shape[0] // tm
    kern = functools.partial(_inproj_kernel, modes=modes, tn=tn)
    return pl.pallas_call(
        kern,
        grid=(m // tm, n // tn),
        in_specs=[pl.BlockSpec((tm, d), lambda i, j: (i, 0)),
                  pl.BlockSpec((None, 1, d), lambda i, j: (i // per_b, 0, 0)),
                  pl.BlockSpec((None, 1, d), lambda i, j: (i // per_b, 0, 0)),
                  pl.BlockSpec((1, d), lambda i, j: (0, 0)),
                  pl.BlockSpec((d, tn), lambda i, j: (0, cb0 + j)),
                  pl.BlockSpec((1, HEAD_DIM), lambda i, j: (0, 0)),
                  pl.BlockSpec((1, HEAD_DIM), lambda i, j: (0, 0)),
                  pl.BlockSpec((tm, HEAD_DIM), lambda i, j: (i % pos_blocks, 0)),
                  pl.BlockSpec((tm, HEAD_DIM), lambda i, j: (i % pos_blocks, 0))],
        out_specs=pl.BlockSpec((tm, tn), lambda i, j: (i, j)),
        out_shape=jax.ShapeDtypeStruct((m, n), BF16),
        scratch_shapes=[pltpu.VMEM((tm, d), BF16), pltpu.VMEM((d, tn), BF16), pltpu.VMEM((tm, tn), F32)],
        compiler_params=_cparams("arbitrary", "arbitrary"),
        name="inproj",
    )(x2d, shift, scale, norm_w.reshape(1, d), w_in, q_norm_w.reshape(1, HEAD_DIM),
      k_norm_w.reshape(1, HEAD_DIM), cos_t, sin_t)


def _rope_tables(seq):
    rows = seq // GRID_W
    row = jnp.repeat(jnp.arange(rows), GRID_W)
    col = jnp.tile(jnp.arange(GRID_W), rows)
    pos = jnp.stack([row, col], axis=-1).astype(F32)
    inv_freq = ROPE_THETA ** (-jnp.arange(ROPE_FREQS, dtype=F32) / ROPE_FREQS)
    ang = pos[:, :, None] * inv_freq
    cos, sin = jnp.cos(ang), jnp.sin(ang)
    cos_t = jnp.concatenate([cos[:, 0], cos[:, 0], cos[:, 1], cos[:, 1]], axis=-1)
    sin_t = jnp.concatenate([-sin[:, 0], sin[:, 0], -sin[:, 1], sin[:, 1]], axis=-1)
    return cos_t, sin_t


def _attn_kernel(q_ref, k_ref, v_ref, kc_ref, vc_ref, o_ref, vt_scr, vct_scr, *, tq, tk, n_chunks):
    @pl.when(pl.program_id(2) == 0)
    def _():
        for c in range(n_chunks):
            vt_scr[c] = jnp.transpose(v_ref[c * tk:(c + 1) * tk, :].astype(F32)).astype(BF16)
        vct_scr[...] = jnp.transpose(vc_ref[...].astype(F32)).astype(BF16)

    q = jnp.concatenate([q_ref[:, h * HEAD_DIM:(h + 1) * HEAD_DIM] for h in range(Q_GROUP)], axis=0)
    qt = jnp.transpose(q.astype(F32)).astype(BF16)
    m_cols = Q_GROUP * tq

    def scores(kb):
        return jnp.dot(kb, qt, preferred_element_type=F32)

    def update(st, vtb, carry):
        m, l, acc = carry
        m_new = jnp.maximum(m, jnp.max(st, axis=0, keepdims=True))
        alpha = jnp.exp2(m - m_new)
        pt = jnp.exp2(st - m_new)
        l = alpha * l + jnp.sum(pt, axis=0, keepdims=True)
        acc = alpha * acc + jnp.dot(vtb, pt.astype(BF16), preferred_element_type=F32)
        return m_new, l, acc

    carry = (jnp.full((1, m_cols), -jnp.inf, F32), jnp.zeros((1, m_cols), F32),
             jnp.zeros((HEAD_DIM, m_cols), F32))
    st = scores(k_ref[0:tk, :])
    for c in range(n_chunks):
        st_next = scores(k_ref[(c + 1) * tk:(c + 2) * tk, :]) if c + 1 < n_chunks else scores(kc_ref[...])
        carry = update(st, vt_scr[c], carry)
        st = st_next
    _, l, acc = update(st, vct_scr[...], carry)
    o = jnp.transpose(acc / l)
    for h in range(Q_GROUP):
        o_ref[:, h * HEAD_DIM:(h + 1) * HEAD_DIM] = o[h * tq:(h + 1) * tq].astype(BF16)


def _attention(proj, kvc, batch, seq, n_ctx, n_kv, q_off, k_off, v_off, tq, tk):
    gw = Q_GROUP * HEAD_DIM
    per_b = seq // tq
    n_chunks = seq // tk
    kern = functools.partial(_attn_kernel, tq=tq, tk=tk, n_chunks=n_chunks)
    return pl.pallas_call(
        kern,
        grid=(batch, n_kv, per_b),
        in_specs=[pl.BlockSpec((tq, gw), lambda b, g, i: (b * per_b + i, q_off // gw + g)),
                  pl.BlockSpec((seq, HEAD_DIM), lambda b, g, i: (b, k_off // HEAD_DIM + g)),
                  pl.BlockSpec((seq, HEAD_DIM), lambda b, g, i: (b, v_off // HEAD_DIM + g)),
                  pl.BlockSpec((n_ctx, HEAD_DIM), lambda b, g, i: (b, g)),
                  pl.BlockSpec((n_ctx, HEAD_DIM), lambda b, g, i: (b, n_kv + g))],
        out_specs=pl.BlockSpec((tq, gw), lambda b, g, i: (b * per_b + i, g)),
        out_shape=jax.ShapeDtypeStruct((batch * seq, n_kv * gw), BF16),
        scratch_shapes=[pltpu.VMEM((n_chunks, HEAD_DIM, tk), BF16), pltpu.VMEM((HEAD_DIM, n_ctx), BF16)],
        compiler_params=_cparams("arbitrary", "arbitrary", "arbitrary"),
        name="attention",
    )(proj, proj, proj, kvc, kvc)


def _dft_mats(n_rows, n_cols):
    def cs(k, n):
        ang = (2.0 * math.pi / n) * (k % n).astype(F32)
        return jnp.cos(ang), jnp.sin(ang)

    ch = jnp.arange(F_GROUP_DIM)
    cc, sc = cs(ch[:, None] * ch[None, :], F_GROUP_DIM)
    mat_a = jnp.concatenate([cc, -sc], axis=1).astype(BF16)
    r = jnp.arange(n_rows)
    cr, sr = cs(r[:, None] * r[None, :], n_rows)
    mat_1 = jnp.concatenate([jnp.concatenate([cr, sr], axis=1),
                             jnp.concatenate([-sr, cr], axis=1)], axis=0).astype(BF16)
    c = jnp.arange(n_cols)
    c2, s2 = cs(c[:, None] * c[None, :], n_cols)
    mat_2 = jnp.concatenate([c2, s2], axis=1).astype(BF16)
    tr, ts = cs(r[:, None] * c[None, :], n_rows * n_cols)
    tw_r = jnp.repeat(tr, LANE, axis=1)
    tw_i = jnp.repeat(-ts, LANE, axis=1)
    return mat_a, mat_1, mat_2, tw_r, tw_i


def _fourier_a_kernel(u_ref, m_ref, vr_ref, vi_ref, *, groups):
    for g in range(groups):
        sl = slice(g * F_GROUP_DIM, (g + 1) * F_GROUP_DIM)
        r = jnp.dot(u_ref[:, sl], m_ref[...], preferred_element_type=F32)
        vr_ref[:, sl] = r[:, :F_GROUP_DIM].astype(BF16)
        vi_ref[:, sl] = r[:, F_GROUP_DIM:].astype(BF16)


def _fourier_1_kernel(ar_ref, ai_ref, m_ref, twr_ref, twi_ref, pr_ref, pi_ref, *, nseg, fw, n_rows):
    rhs = jnp.concatenate([ar_ref[...], ai_ref[...]], axis=0)
    z = jnp.dot(m_ref[...], rhs, preferred_element_type=F32)
    reps = fw // LANE
    for s in range(nseg):
        zr = z[:n_rows, s * fw:(s + 1) * fw]
        zi = z[n_rows:, s * fw:(s + 1) * fw]
        tr = jnp.tile(twr_ref[:, s * LANE:(s + 1) * LANE], (1, reps))
        ti = jnp.tile(twi_ref[:, s * LANE:(s + 1) * LANE], (1, reps))
        pr_ref[s] = (zr * tr - zi * ti).astype(BF16)
        pi_ref[s] = (zr * ti + zi * tr).astype(BF16)


def _fourier_2_kernel(ar_ref, ai_ref, m_ref, o_ref, *, norm):
    rhs = jnp.concatenate([ar_ref[...], ai_ref[...]], axis=0)
    o_ref[...] = (jnp.dot(m_ref[...], rhs, preferred_element_type=F32) * norm).astype(BF16)


def _fourier_mix(proj, batch, seq, fw):
    n_cols = GRID_W
    n_rows = seq // GRID_W
    groups = fw // F_GROUP_DIM
    mat_a, mat_1, mat_2, tw_r, tw_i = _dft_mats(n_rows, n_cols)
    m = batch * seq
    tm = min(seq, 1024)
    vr, vi = pl.pallas_call(
        functools.partial(_fourier_a_kernel, groups=groups),
        grid=(m // tm,),
        in_specs=[pl.BlockSpec((tm, fw), lambda i: (i, 0)),
                  pl.BlockSpec((F_GROUP_DIM, 2 * F_GROUP_DIM), lambda i: (0, 0))],
        out_specs=[pl.BlockSpec((tm, fw), lambda i: (i, 0))] * 2,
        out_shape=[jax.ShapeDtypeStruct((m, fw), BF16)] * 2,
        compiler_params=_cparams("arbitrary"),
        name="fourier_channels",
    )(proj, mat_a)
    wide = n_cols * fw
    vr = vr.reshape(batch * n_rows, wide)
    vi = vi.reshape(batch * n_rows, wide)
    nseg = min(n_cols, 4)
    pr, pi = pl.pallas_call(
        functools.partial(_fourier_1_kernel, nseg=nseg, fw=fw, n_rows=n_rows),
        grid=(batch, n_cols // nseg),
        in_specs=[pl.BlockSpec((n_rows, nseg * fw), lambda b, j: (b, j)),
                  pl.BlockSpec((n_rows, nseg * fw), lambda b, j: (b, j)),
                  pl.BlockSpec((2 * n_rows, 2 * n_rows), lambda b, j: (0, 0)),
                  pl.BlockSpec((n_rows, nseg * LANE), lambda b, j: (0, j)),
                  pl.BlockSpec((n_rows, nseg * LANE), lambda b, j: (0, j))],
        out_specs=[pl.BlockSpec((nseg, n_rows, fw), lambda b, j: (b * (n_cols // nseg) + j, 0, 0))] * 2,
        out_shape=[jax.ShapeDtypeStruct((batch * n_cols, n_rows, fw), BF16)] * 2,
        compiler_params=_cparams("arbitrary", "arbitrary"),
        name="fourier_rows",
    )(vr, vi, mat_1, tw_r, tw_i)
    wide2 = n_rows * fw
    pr = pr.reshape(batch * n_cols, wide2)
    pi = pi.reshape(batch * n_cols, wide2)
    tc = min(wide2, 8192)
    norm = 1.0 / math.sqrt(seq * F_GROUP_DIM)
    yf = pl.pallas_call(
        functools.partial(_fourier_2_kernel, norm=norm),
        grid=(batch, wide2 // tc),
        in_specs=[pl.BlockSpec((n_cols, tc), lambda b, j: (b, j)),
                  pl.BlockSpec((n_cols, tc), lambda b, j: (b, j)),
                  pl.BlockSpec((n_cols, 2 * n_cols), lambda b, j: (0, 0))],
        out_specs=pl.BlockSpec((n_cols, tc), lambda b, j: (b, j)),
        out_shape=jax.ShapeDtypeStruct((batch * n_cols, wide2), BF16),
        compiler_params=_cparams("arbitrary", "arbitrary"),
        name="fourier_cols",
    )(pr, pi, mat_2)
    return yf.reshape(m, fw)


def _merge_kernel(yf_ref, ya_ref, wfo_ref, wao_ref, gf_ref, ga_ref, o_ref):
    a = jnp.dot(yf_ref[...], wfo_ref[...], preferred_element_type=F32)
    b = jnp.dot(ya_ref[...], wao_ref[...], preferred_element_type=F32)
    o_ref[...] = (gf_ref[...].astype(F32) * a + ga_ref[...].astype(F32) * b).astype(BF16)


def _merge(yf, ya, wfo, wao, proj, gf_off, ga_off, tm, tn):
    m, fw = yf.shape
    aw = ya.shape[1]
    d = wfo.shape[1]
    return pl.pallas_call(
        _merge_kernel,
        grid=(m // tm, d // tn),
        in_specs=[pl.BlockSpec((tm, fw), lambda i, j: (i, 0)),
                  pl.BlockSpec((tm, aw), lambda i, j: (i, 0)),
                  pl.BlockSpec((fw, tn), lambda i, j: (0, j)),
                  pl.BlockSpec((aw, tn), lambda i, j: (0, j)),
                  pl.BlockSpec((tm, tn), lambda i, j: (i, gf_off // tn + j)),
                  pl.BlockSpec((tm, tn), lambda i, j: (i, ga_off // tn + j))],
        out_specs=pl.BlockSpec((tm, tn), lambda i, j: (i, j)),
        out_shape=jax.ShapeDtypeStruct((m, d), BF16),
        compiler_params=_cparams("arbitrary", "arbitrary"),
        name="merge",
    )(yf, ya, wfo, wao, proj, proj)


def _route(logits_t, bias, n_exp):
    gsz = n_exp // N_EXPERT_GROUPS
    tm = logits_t.shape[1]
    neg = -jnp.inf
    scores = jax.nn.sigmoid(logits_t)
    biased = scores + bias
    io_g = lax.broadcasted_iota(I32, (gsz, tm), 0).astype(F32)
    gs = []
    for g in range(N_EXPERT_GROUPS):
        grp = biased[g * gsz:(g + 1) * gsz, :]
        m1 = jnp.max(grp, axis=0, keepdims=True)
        i1 = jnp.min(jnp.where(grp == m1, io_g, float(gsz)), axis=0, keepdims=True)
        m2 = jnp.max(jnp.where(io_g == i1, neg, grp), axis=0, keepdims=True)
        gs.append(m1 + m2)
    cur = jnp.concatenate(gs, axis=0)
    io_n = lax.broadcasted_iota(I32, (N_EXPERT_GROUPS, tm), 0).astype(F32)
    sel = jnp.zeros((N_EXPERT_GROUPS, tm), F32)
    for _ in range(TOPK_GROUPS):
        mx = jnp.max(cur, axis=0, keepdims=True)
        ix = jnp.min(jnp.where(cur == mx, io_n, float(N_EXPERT_GROUPS)), axis=0, keepdims=True)
        hit = io_n == ix
        sel = jnp.where(hit, 1.0, sel)
        cur = jnp.where(hit, neg, cur)
    cur = jnp.concatenate([jnp.where(sel[g:g + 1, :] > 0.0, biased[g * gsz:(g + 1) * gsz, :], neg)
                           for g in range(N_EXPERT_GROUPS)], axis=0)
    io_e = lax.broadcasted_iota(I32, (n_exp, tm), 0).astype(F32)
    eidx, wts = [], []
    for _ in range(TOP_K):
        mx = jnp.max(cur, axis=0, keepdims=True)
        ix = jnp.min(jnp.where(cur == mx, io_e, float(n_exp)), axis=0, keepdims=True)
        hit = io_e == ix
        wts.append(jnp.sum(jnp.where(hit, scores, 0.0), axis=0, keepdims=True))
        eidx.append(ix)
        cur = jnp.where(hit, neg, cur)
    eidx = jnp.concatenate(eidx, axis=0).astype(I32)
    wts = jnp.concatenate(wts, axis=0)
    wts = wts / jnp.sum(wts, axis=0, keepdims=True) * ROUTED_SCALE
    return eidx, wts


def _outproj_kernel(y_ref, wo_ref, x_ref, g1_ref, nw_ref, sh_ref, sc_ref, rw_ref, rb_ref,
                    x1_ref, h2_ref, h2p_ref, eidx_ref, wts_ref, cnt_ref, carry_scr, *, n_exp):
    i = pl.program_id(0)
    tm, d = x_ref.shape
    n_sub = d // LANE

    @pl.when(i == 0)
    def _():
        carry_scr[...] = jnp.zeros_like(carry_scr)

    half = tm // 2 if tm % (2 * LANE) == 0 else tm
    added = jnp.zeros((n_exp, 1), F32)
    for r0 in range(0, tm, half):
        rs = slice(r0, r0 + half)
        x1 = x_ref[rs, :] + g1_ref[...] * jnp.dot(y_ref[rs, :], wo_ref[...], preferred_element_type=F32)
        x1_ref[rs, :] = x1
        ms = jnp.mean(x1 * x1, axis=-1, keepdims=True)
        h2 = (x1 * lax.rsqrt(ms + EPS) * nw_ref[...]) * (1.0 + sc_ref[...]) + sh_ref[...]
        h2b = h2.astype(BF16)
        h2_ref[rs, :] = h2b
        for s in range(n_sub):
            h2p_ref[pl.ds(r0 * n_sub + s, half, stride=n_sub), :] = h2[:, s * LANE:(s + 1) * LANE]
        h_lo = (h2 - h2b.astype(F32)).astype(BF16)
        part = (jnp.dot(h2b, rw_ref[...], preferred_element_type=F32)
                + jnp.dot(h_lo, rw_ref[...], preferred_element_type=F32))
        logits = part[:, :LANE] + part[:, LANE:]
        logits_t = jnp.transpose(logits)[:n_exp, :]
        eidx, wts = _route(logits_t, rb_ref[...], n_exp)
        eidx_ref[:, rs] = eidx
        wts_ref[:, rs] = wts
        io_e = lax.broadcasted_iota(I32, (n_exp, half), 0)
        onehot = jnp.zeros((n_exp, half), F32)
        for k in range(TOP_K):
            onehot = onehot + jnp.where(io_e == eidx[k:k + 1, :], 1.0, 0.0)
        added = added + jnp.sum(onehot, axis=1, keepdims=True)
    carry_scr[...] = carry_scr[...] + added
    cnt_ref[...] = carry_scr[...].astype(I32)


def _outproj_route(y, wo, x2d, g1, norm_w, shift, scale, rw_pad, rb, seq, n_exp, tm):
    m, d = x2d.shape
    per_b = seq // tm
    n_sub = d // LANE
    bspec = pl.BlockSpec((None, 1, d), lambda i: (i // per_b, 0, 0))
    row = pl.BlockSpec((tm, d), lambda i: (i, 0))
    tok = pl.BlockSpec((TOP_K, tm), lambda i: (0, i))
    return pl.pallas_call(
        functools.partial(_outproj_kernel, n_exp=n_exp),
        grid=(m // tm,),
        in_specs=[row,
                  pl.BlockSpec((d, d), lambda i: (0, 0), pipeline_mode=pl.Buffered(1)),
                  row, bspec,
                  pl.BlockSpec((1, d), lambda i: (0, 0)),
                  bspec, bspec,
                  pl.BlockSpec((d, 2 * LANE), lambda i: (0, 0)),
                  pl.BlockSpec((n_exp, 1), lambda i: (0, 0))],
        out_specs=[row, row,
                   pl.BlockSpec((tm * n_sub, LANE), lambda i: (i, 0)),
                   tok, tok,
                   pl.BlockSpec((n_exp, LANE), lambda i: (0, 0))],
        out_shape=[jax.ShapeDtypeStruct((m, d), F32),
                   jax.ShapeDtypeStruct((m, d), BF16),
                   jax.ShapeDtypeStruct((m * n_sub, LANE), F32),
                   jax.ShapeDtypeStruct((TOP_K, m), I32),
                   jax.ShapeDtypeStruct((TOP_K, m), F32),
                   jax.ShapeDtypeStruct((n_exp, LANE), I32)],
        scratch_shapes=[pltpu.VMEM((n_exp, LANE), F32)],
        compiler_params=_cparams("arbitrary"),
        name="outproj_route",
    )(y, wo, x2d, g1, norm_w.reshape(1, d), shift, scale, rw_pad, rb.reshape(n_exp, 1))


def _expert_kernel(be_ref, bwin_ref, brem_ref, bval_ref, nact_ref, prev_win, cur_win, nxt_win,
                   h2p_hbm, wg_ref, wu_ref, wd_ref, ytk_hbm,
                   wg_s, wu_s, wd_s, xbuf0, xbuf1, ybuf0, ybuf1, sem_g, sem_s, sem_z,
                   *, n_sub, n_tok, n_blocks, trash_base):
    i = pl.program_id(0)
    nact = nact_ref[0]
    bm = EXPERT_BLOCK
    xbufs = (xbuf0, xbuf1)
    ybufs = (ybuf0, ybuf1)
    unroll = 8

    def token_of(f):
        return f & (n_tok - 1) if n_tok & (n_tok - 1) == 0 else lax.rem(f, n_tok)

    def gather_copy(tok, r, par):
        return pltpu.make_async_copy(h2p_hbm.at[pl.ds(pl.multiple_of(tok * n_sub, n_sub), n_sub), :],
                                     xbufs[par].at[pl.ds(pl.multiple_of(r * n_sub, n_sub), n_sub), :],
                                     sem_g.at[par])

    def scatter_copy(r, row, par):
        return pltpu.make_async_copy(ybufs[par].at[pl.ds(pl.multiple_of(r * n_sub, n_sub), n_sub), :],
                                     ytk_hbm.at[pl.ds(pl.multiple_of(row * n_sub, n_sub), n_sub), :],
                                     sem_s.at[par])

    def wait_gathers(par):
        @pl.loop(0, bm // unroll)
        def _(q):
            for u in range(unroll):
                gather_copy(0, 0, par).wait()

    def wait_scatters(par):
        @pl.loop(0, bm // unroll)
        def _(q):
            for u in range(unroll):
                scatter_copy(0, 0, par).wait()

    def scatter_row(win_ref, rem, valid, r, par):
        row = jnp.where(r < valid, win_ref[0, rem + r], trash_base + par * bm + r)
        return scatter_copy(r, row, par)

    @pl.when(i == 0)
    def _():
        ybuf0[...] = jnp.zeros_like(ybuf0)
        ybuf1[...] = jnp.zeros_like(ybuf1)
        for par in range(2):
            start = (trash_base + par * bm) * n_sub
            cp = pltpu.make_async_copy(ybuf0, ytk_hbm.at[pl.ds(start, bm * n_sub), :], sem_z)
            cp.start()
            cp.wait()
        rem0 = brem_ref[0]

        @pl.loop(0, bm // unroll)
        def _(q):
            for u in range(unroll):
                r = q * unroll + u
                gather_copy(token_of(cur_win[0, rem0 + r]), r, 0).start(priority=u % 2)

    def block_step(par):
        wait_gathers(par)

        @pl.when(i >= 1)
        def _():
            wait_scatters(par)

        @pl.when((i == 0) | (be_ref[i] != be_ref[jnp.maximum(i - 1, 0)]))
        def _():
            wg_s[...] = wg_ref[...].astype(BF16)
            wu_s[...] = wu_ref[...].astype(BF16)
            wd_s[...] = wd_ref[...].astype(BF16)

        nxt_rem = brem_ref[jnp.minimum(i + 1, n_blocks - 1)]
        prv = jnp.maximum(i - 1, 0)
        prv_rem = brem_ref[prv]
        prv_valid = jnp.where(i >= 1, bval_ref[prv], 0)
        for r in range(bm):
            gather_copy(token_of(nxt_win[0, nxt_rem + r]), r, 1 - par).start(priority=r % 2)
        for r in range(bm):
            scatter_row(prev_win, prv_rem, prv_valid, r, 1 - par).start(priority=r % 2)

        x = jnp.concatenate([xbufs[par][pl.ds(s, bm, stride=n_sub), :] for s in range(n_sub)],
                            axis=1).astype(BF16)
        g = jnp.dot(x, wg_s[...], preferred_element_type=F32)
        u = jnp.dot(x, wu_s[...], preferred_element_type=F32)
        y = jnp.dot((_silu(g) * u).astype(BF16), wd_s[...], preferred_element_type=F32)
        for s in range(n_sub):
            ybufs[par][pl.ds(s, bm, stride=n_sub), :] = y[:, s * LANE:(s + 1) * LANE]

        @pl.when(i == nact - 1)
        def _():
            wait_gathers(1 - par)
            wait_scatters(1 - par)
            rem = brem_ref[i]
            valid = bval_ref[i]

            @pl.loop(0, bm // unroll)
            def _(q):
                for u_ in range(unroll):
                    scatter_row(cur_win, rem, valid, q * unroll + u_, par).start(priority=u_ % 2)

            wait_scatters(par)

    for par in range(2):
        @pl.when((i < nact) & (lax.rem(i, 2) == par))
        def _(par=par):
            block_step(par)


def _experts(h2p, order, blk_expert, blk_off, blk_valid, n_active, exp_gate, exp_up, exp_down, n_blocks, n_tok):
    n_exp, d, ff = exp_gate.shape
    n_sub = d // LANE
    bm = EXPERT_BLOCK
    n_win = order.shape[0] // bm
    pieces = order.reshape(n_win, 1, bm)
    windows = jnp.concatenate([pieces, jnp.roll(pieces, -1, axis=0)], axis=-1)
    blk_win = blk_off // bm
    blk_rem = blk_off - blk_win * bm
    trash_base = TOP_K * n_tok

    def win(step_shift):
        def index_map(i, be, bwin, *_):
            return (bwin[jnp.clip(i + step_shift, 0, n_blocks - 1)], 0, 0)
        return pl.BlockSpec((None, 1, 2 * bm), index_map, memory_space=pltpu.SMEM)

    row_buf = pltpu.VMEM((bm * n_sub, LANE), F32)
    grid_spec = pltpu.PrefetchScalarGridSpec(
        num_scalar_prefetch=5,
        grid=(n_blocks,),
        in_specs=[win(-1), win(0), win(1),
                  pl.BlockSpec(memory_space=pl.ANY),
                  pl.BlockSpec((None, d, ff), lambda i, be, *_: (be[i], 0, 0)),
                  pl.BlockSpec((None, d, ff), lambda i, be, *_: (be[i], 0, 0)),
                  pl.BlockSpec((None, ff, d), lambda i, be, *_: (be[i], 0, 0))],
        out_specs=pl.BlockSpec(memory_space=pl.ANY),
        scratch_shapes=[pltpu.VMEM((d, ff), BF16), pltpu.VMEM((d, ff), BF16), pltpu.VMEM((ff, d), BF16),
                        row_buf, row_buf, row_buf, row_buf,
                        pltpu.SemaphoreType.DMA((2,)), pltpu.SemaphoreType.DMA((2,)),
                        pltpu.SemaphoreType.DMA(())],
    )
    return pl.pallas_call(
        functools.partial(_expert_kernel, n_sub=n_sub, n_tok=n_tok, n_blocks=n_blocks, trash_base=trash_base),
        grid_spec=grid_spec,
        out_shape=jax.ShapeDtypeStruct(((trash_base + 2 * bm) * n_sub, LANE), F32),
        compiler_params=_cparams("arbitrary"),
        name="experts",
    )(blk_expert, blk_win, blk_rem, blk_valid, n_active, windows, windows, windows, h2p, exp_gate, exp_up,
      exp_down)


def _combine_kernel(*refs, tmc, n_out):
    yk_refs = refs[:TOP_K]
    wt_ref, h2_ref, x1_ref, g2_ref, sg_ref, su_ref, sd_ref, o_ref, acc_scr = refs[TOP_K:]
    h = h2_ref[...]
    g = jnp.dot(h, sg_ref[...], preferred_element_type=F32)
    u = jnp.dot(h, su_ref[...], preferred_element_type=F32)
    shared = jnp.dot((_silu(g) * u).astype(BF16), sd_ref[...], preferred_element_type=F32)
    unroll = 4

    @pl.loop(0, tmc // unroll)
    def _(q):
        for j in range(unroll):
            t = q * unroll + j
            rows = pl.ds(pl.multiple_of(t * n_out, n_out), n_out)
            acc = wt_ref[0, t] * yk_refs[0][rows, :]
            for k in range(1, TOP_K):
                acc = acc + wt_ref[k, t] * yk_refs[k][rows, :]
            acc_scr[rows, :] = acc

    routed = jnp.concatenate([acc_scr[pl.ds(s, tmc, stride=n_out), :] for s in range(n_out)], axis=1)
    o_ref[...] = x1_ref[...] + g2_ref[...] * (routed + shared)


def _combine(ytk, wts, h2, x1, g2, sg, su, sd, seq, tmc):
    m, d = x1.shape
    ff = sg.shape[1]
    n_out = d // LANE
    per_b = seq // tmc
    tiles = m // tmc
    row = pl.BlockSpec((tmc, d), lambda i: (i, 0))
    yk_specs = [pl.BlockSpec((tmc * n_out, LANE), lambda i, k=k: (k * tiles + i, 0)) for k in range(TOP_K)]
    return pl.pallas_call(
        functools.partial(_combine_kernel, tmc=tmc, n_out=n_out),
        grid=(tiles,),
        in_specs=yk_specs + [pl.BlockSpec((TOP_K, tmc), lambda i: (0, i), memory_space=pltpu.SMEM),
                             row, row,
                             pl.BlockSpec((None, 1, d), lambda i: (i // per_b, 0, 0)),
                             pl.BlockSpec((d, ff), lambda i: (0, 0)),
                             pl.BlockSpec((d, ff), lambda i: (0, 0)),
                             pl.BlockSpec((ff, d), lambda i: (0, 0))],
        out_specs=row,
        out_shape=jax.ShapeDtypeStruct((m, d), F32),
        scratch_shapes=[pltpu.VMEM((tmc * n_out, LANE), F32)],
        compiler_params=_cparams("arbitrary"),
        name="combine",
    )(*([ytk] * TOP_K), wts, h2, x1, g2, sg, su, sd)


def _tiles(seq, d, kvw, n_ctx):
    return dict(
        inproj_tm=min(1024, seq), inproj_tn=min(512, kvw), ctx_tm=min(256, n_ctx),
        attn_tq=min(512, seq), attn_tk=min(512, seq),
        merge_tm=min(1024, seq), merge_tn=min(512, d),
        outproj_tm=min(512, seq), combine_tm=min(128, seq))


def kernel(x, c, ctx, c_ctx, mod_w, mod_b, norm1_w, w_in, q_norm_w, k_norm_w, w_fourier_out, w_attn_out, w_out,
           norm2_w, router_w, router_b, exp_gate, exp_up, exp_down, shared_gate, shared_up, shared_down):
    batch, seq, d = x.shape
    n_ctx = ctx.shape[1]
    assert mod_w.shape[0] == 1, "single-layer block only"
    assert batch + 1 <= 8 and seq % GRID_W == 0
    fw = w_fourier_out.shape[1]
    att = w_attn_out.shape[1]
    in_cols = w_in.shape[2]
    kvw = (in_cols - fw - att - 2 * d) // 2
    n_kv = kvw // HEAD_DIM
    n_exp = router_w.shape[2]
    assert att == n_kv * Q_GROUP * HEAD_DIM and n_exp <= LANE
    q_off, k_off = fw, fw + att
    v_off = k_off + kvw
    gf_off = v_off + kvw
    ga_off = gf_off + d
    m = batch * seq
    assert q_off % (Q_GROUP * HEAD_DIM) == 0 and kvw % HEAD_DIM == 0 and fw % F_GROUP_DIM == 0
    t = _tiles(seq, d, kvw, n_ctx)
    assert gf_off % t["merge_tn"] == 0 and d % (2 * LANE) == 0

    cond = jnp.concatenate([c, c_ctx[None], jnp.zeros((8 - batch - 1, d), F32)], axis=0)
    mod = _adaln(cond, mod_w[0], mod_b[0])
    sh1, sc1, g1, sh2, sc2, g2 = [mod[:batch, j * d:(j + 1) * d].reshape(batch, 1, d) for j in range(6)]
    csh1 = jnp.broadcast_to(mod[batch, 0:d], (batch, 1, d))
    csc1 = jnp.broadcast_to(mod[batch, d:2 * d], (batch, 1, d))

    cos_t, sin_t = _rope_tables(seq)
    tn = t["inproj_tn"]
    modes = (("plain", 0, q_off // tn), ("q", q_off // tn, k_off // tn), ("k", k_off // tn, v_off // tn),
             ("plain", v_off // tn, gf_off // tn), ("sigmoid", gf_off // tn, in_cols // tn))
    proj = _inproj(x.reshape(m, d), sh1, sc1, norm1_w[0], w_in[0], q_norm_w[0], k_norm_w[0], cos_t, sin_t,
                   modes, seq, t["inproj_tm"], tn)
    cmodes = (("k_norope", 0, kvw // tn), ("plain", kvw // tn, 2 * kvw // tn))
    tmc_ctx = t["ctx_tm"]
    kvc = _inproj(ctx.reshape(batch * n_ctx, d), csh1, csc1, norm1_w[0], w_in[0], q_norm_w[0],
                  k_norm_w[0], cos_t[:tmc_ctx], sin_t[:tmc_ctx], cmodes, n_ctx, tmc_ctx, tn, col0=k_off,
                  n_cols=2 * kvw)

    ya = _attention(proj, kvc, batch, seq, n_ctx, n_kv, q_off, k_off, v_off, tq=t["attn_tq"], tk=t["attn_tk"])
    yf = _fourier_mix(proj, batch, seq, fw)
    y = _merge(yf, ya, w_fourier_out[0].astype(BF16), w_attn_out[0].astype(BF16), proj, gf_off, ga_off,
               tm=t["merge_tm"], tn=t["merge_tn"])

    rw_pad = jnp.pad(router_w[0], ((0, 0), (0, LANE - n_exp)))
    rw_hi = rw_pad.astype(BF16)
    rw_pad = jnp.concatenate([rw_hi, (rw_pad - rw_hi.astype(F32)).astype(BF16)], axis=1)
    x1, h2, h2p, eidx, wts, cnt = _outproj_route(
        y, w_out[0].astype(BF16), x.reshape(m, d), g1, norm2_w[0], sh2, sc2, rw_pad, router_b[0], seq, n_exp,
        tm=t["outproj_tm"])

    n_assign = m * TOP_K
    assert n_assign % EXPERT_BLOCK == 0
    assert n_exp * n_assign < 2 ** 31
    order = lax.sort(eidx.reshape(-1) * n_assign + jnp.arange(n_assign, dtype=I32)) % n_assign
    counts = cnt[:, 0]
    starts = jnp.cumsum(counts) - counts
    nb_e = (counts + EXPERT_BLOCK - 1) // EXPERT_BLOCK
    cum_nb = jnp.cumsum(nb_e)
    n_blocks = n_assign // EXPERT_BLOCK + n_exp
    n_active = jnp.maximum(cum_nb[-1], 1).astype(I32)
    blk = jnp.minimum(jnp.arange(n_blocks, dtype=I32), n_active - 1)
    blk_expert = jnp.minimum(jnp.sum(blk[:, None] >= cum_nb[None, :], axis=1), n_exp - 1).astype(I32)
    is_e = blk_expert[:, None] == jnp.arange(n_exp, dtype=I32)[None, :]
    lookup = lambda table: jnp.sum(jnp.where(is_e, table[None, :], 0), axis=1)
    blk_j = blk - lookup(cum_nb - nb_e)
    blk_off = (lookup(starts) + blk_j * EXPERT_BLOCK).astype(I32)
    blk_valid = jnp.clip(lookup(counts) - blk_j * EXPERT_BLOCK, 0, EXPERT_BLOCK).astype(I32)

    ytk = _experts(h2p, order, blk_expert, blk_off, blk_valid, n_active.reshape(1), exp_gate[0], exp_up[0],
                   exp_down[0], n_blocks, m)
    out = _combine(ytk, wts, h2, x1, g2, shared_gate[0].astype(BF16), shared_up[0].astype(BF16),
                   shared_down[0].astype(BF16), seq, tmc=t["combine_tm"])
    return out.reshape(batch, seq, d)
```

```python
import functools
import math

import jax
import jax.numpy as jnp
from jax import lax
from jax.experimental import pallas as pl
from jax.experimental.pallas import tpu as pltpu

F32 = jnp.float32
BF16 = jnp.bfloat16
I32 = jnp.int32

GRID_W = 64
HEAD_DIM = 128
Q_GROUP = 4
ROPE_FREQS = HEAD_DIM // 4
ROPE_THETA = 10000.0
F_GROUP_DIM = 128
N_EXPERT_GROUPS = 8
TOPK_GROUPS = 4
TOP_K = 8
ROUTED_SCALE = 2.5
EPS = 1e-6
LANE = 128
EXPERT_BLOCK = 256
VMEM_LIMIT = 56 * 1024 * 1024


def _cparams(*sem):
    return pltpu.CompilerParams(dimension_semantics=sem, vmem_limit_bytes=VMEM_LIMIT)


def _silu(v):
    return v * jax.nn.sigmoid(v)


def _adaln_kernel(c_ref, w_ref, b_ref, o_ref):
    a = _silu(c_ref[...])
    o_ref[...] = jnp.dot(a.astype(BF16), w_ref[...].astype(BF16), preferred_element_type=F32) + b_ref[...]


def _adaln(cond_pad, mod_w, mod_b):
    d, n = mod_w.shape
    tn = min(n, 1024)
    return pl.pallas_call(
        _adaln_kernel,
        grid=(n // tn,),
        in_specs=[pl.BlockSpec((8, d), lambda j: (0, 0)),
                  pl.BlockSpec((d, tn), lambda j: (0, j)),
                  pl.BlockSpec((1, tn), lambda j: (0, j))],
        out_specs=pl.BlockSpec((8, tn), lambda j: (0, j)),
        out_shape=jax.ShapeDtypeStruct((8, n), F32),
        compiler_params=_cparams("arbitrary"),
        name="adaln",
    )(cond_pad, mod_w, mod_b.reshape(1, n))


def _head_norm_rope(a, w, cos, sin, scale):
    sq = a * a
    sq_hi = sq.astype(BF16)
    sq_lo = (sq - sq_hi.astype(F32)).astype(BF16)
    avg = jnp.full((HEAD_DIM, HEAD_DIM), 1.0 / HEAD_DIM, BF16)
    ms = (jnp.dot(sq_hi, avg, preferred_element_type=F32) + jnp.dot(sq_lo, avg, preferred_element_type=F32))
    a = a * lax.rsqrt(ms + EPS) * w
    if cos is not None:
        lane = lax.broadcasted_iota(I32, a.shape, 1)
        first = (lane % (2 * ROPE_FREQS)) < ROPE_FREQS
        partner = jnp.where(first, pltpu.roll(a, HEAD_DIM - ROPE_FREQS, 1), pltpu.roll(a, ROPE_FREQS, 1))
        a = a * cos + partner * sin
    if scale is not None:
        a = a * scale
    return a


def _inproj_kernel(x_ref, sh_ref, sc_ref, nw_ref, w_ref, qw_ref, kw_ref, cos_ref, sin_ref, o_ref, h_scr, wb_scr,
                   acc_scr, *, modes, tn):
    j = pl.program_id(1)

    @pl.when(j == 0)
    def _():
        x = x_ref[...]
        ms = jnp.mean(x * x, axis=-1, keepdims=True)
        y = x * lax.rsqrt(ms + EPS) * nw_ref[...]
        h_scr[...] = (y * (1.0 + sc_ref[...]) + sh_ref[...]).astype(BF16)

    wb_scr[...] = w_ref[...].astype(BF16)

    def tile(c0, c1):
        return jnp.dot(h_scr[...], wb_scr[:, c0:c1], preferred_element_type=F32)

    piece = 2 * HEAD_DIM if tn % (2 * HEAD_DIM) == 0 else HEAD_DIM
    gated = functools.reduce(jnp.logical_or, [(j >= j0) & (j < j1) for mode, j0, j1 in modes if mode == "sigmoid"],
                             jnp.bool_(False))

    @pl.when(gated)
    def _():
        for c0 in range(0, tn, piece):
            o_ref[:, c0:c0 + piece] = jax.nn.sigmoid(tile(c0, c0 + piece)).astype(BF16)

    @pl.when(jnp.logical_not(gated))
    def _():
        acc_scr[...] = tile(0, tn)

    for mode, j0, j1 in modes:
        if mode == "sigmoid":
            continue

        @pl.when((j >= j0) & (j < j1))
        def _(mode=mode):
            if mode == "plain":
                o_ref[...] = acc_scr[...].astype(BF16)
                return
            rope = mode in ("q", "k")
            w = qw_ref[...] if mode == "q" else kw_ref[...]
            scale = HEAD_DIM ** -0.5 * math.log2(math.e) if mode == "q" else None
            for h in range(tn // HEAD_DIM):
                sl = slice(h * HEAD_DIM, (h + 1) * HEAD_DIM)
                a = _head_norm_rope(acc_scr[:, sl], w, cos_ref[...] if rope else None,
                                    sin_ref[...] if rope else None, scale)
                o_ref[:, sl] = a.astype(BF16)


def _inproj(x2d, shift, scale, norm_w, w_in, q_norm_w, k_norm_w, cos_t, sin_t, modes, seq, tm, tn, col0=0,
            n_cols=None):
    m, d = x2d.shape
    n = w_in.shape[1] if n_cols is None else n_cols
    cb0 = col0 // tn
    per_b = seq // tm
    pos_blocks = cos_t.shape[0] // tm
    kern = functools.partial(_inproj_kernel, modes=modes, tn=tn)
    return pl.pallas_call(
        kern,
        grid=(m // tm, n // tn),
        in_specs=[pl.BlockSpec((tm, d), lambda i, j: (i, 0)),
                  pl.BlockSpec((None, 1, d), lambda i, j: (i // per_b, 0, 0)),
                  pl.BlockSpec((None, 1, d), lambda i, j: (i // per_b, 0, 0)),
                  pl.BlockSpec((1, d), lambda i, j: (0, 0)),
                  pl.BlockSpec((d, tn), lambda i, j: (0, cb0 + j)),
                  pl.BlockSpec((1, HEAD_DIM), lambda i, j: (0, 0)),
                  pl.BlockSpec((1, HEAD_DIM), lambda i, j: (0, 0)),
                  pl.BlockSpec((tm, HEAD_DIM), lambda i, j: (i % pos_blocks, 0)),
                  pl.BlockSpec((tm, HEAD_DIM), lambda i, j: (i % pos_blocks, 0))],
        out_specs=pl.BlockSpec((tm, tn), lambda i, j: (i, j)),
        out_shape=jax.ShapeDtypeStruct((m, n), BF16),
        scratch_shapes=[pltpu.VMEM((tm, d), BF16), pltpu.VMEM((d, tn), BF16), pltpu.VMEM((tm, tn), F32)],
        compiler_params=_cparams("arbitrary", "arbitrary"),
        name="inproj",
    )(x2d, shift, scale, norm_w.reshape(1, d), w_in, q_norm_w.reshape(1, HEAD_DIM),
      k_norm_w.reshape(1, HEAD_DIM), cos_t, sin_t)


def _rope_tables(seq):
    rows = seq // GRID_W
    row = jnp.repeat(jnp.arange(rows), GRID_W)
    col = jnp.tile(jnp.arange(GRID_W), rows)
    pos = jnp.stack([row, col], axis=-1).astype(F32)
    inv_freq = ROPE_THETA ** (-jnp.arange(ROPE_FREQS, dtype=F32) / ROPE_FREQS)
    ang = pos[:, :, None] * inv_freq
    cos, sin = jnp.cos(ang), jnp.sin(ang)
    cos_t = jnp.concatenate([cos[:, 0], cos[:, 0], cos[:, 1], cos[:, 1]], axis=-1)
    sin_t = jnp.concatenate([-sin[:, 0], sin[:, 0], -sin[:, 1], sin[:, 1]], axis=-1)
    return cos_t, sin_t


def _attn_kernel(q_ref, k_ref, v_ref, kc_ref, vc_ref, o_ref, vt_scr, vct_scr, *, tq, tk, n_chunks):
    @pl.when(pl.program_id(2) == 0)
    def _():
        for c in range(n_chunks):
            vt_scr[c] = jnp.transpose(v_ref[c * tk:(c + 1) * tk, :].astype(F32)).astype(BF16)
        vct_scr[...] = jnp.transpose(vc_ref[...].astype(F32)).astype(BF16)

    q = jnp.concatenate([q_ref[:, h * HEAD_DIM:(h + 1) * HEAD_DIM] for h in range(Q_GROUP)], axis=0)
    qt = jnp.transpose(q.astype(F32)).astype(BF16)
    m_cols = Q_GROUP * tq

    def scores(kb):
        return jnp.dot(kb, qt, preferred_element_type=F32)

    def update(st, vtb, carry):
        m, l, acc = carry
        m_new = jnp.maximum(m, jnp.max(st, axis=0, keepdims=True))
        alpha = jnp.exp2(m - m_new)
        pt = jnp.exp2(st - m_new)
        l = alpha * l + jnp.sum(pt, axis=0, keepdims=True)
        acc = alpha * acc + jnp.dot(vtb, pt.astype(BF16), preferred_element_type=F32)
        return m_new, l, acc

    carry = (jnp.full((1, m_cols), -jnp.inf, F32), jnp.zeros((1, m_cols), F32),
             jnp.zeros((HEAD_DIM, m_cols), F32))
    st = scores(k_ref[0:tk, :])
    for c in range(n_chunks):
        st_next = scores(k_ref[(c + 1) * tk:(c + 2) * tk, :]) if c + 1 < n_chunks else scores(kc_ref[...])
        carry = update(st, vt_scr[c], carry)
        st = st_next
    _, l, acc = update(st, vct_scr[...], carry)
    o = jnp.transpose(acc / l)
    for h in range(Q_GROUP):
        o_ref[:, h * HEAD_DIM:(h + 1) * HEAD_DIM] = o[h * tq:(h + 1) * tq].astype(BF16)


def _attention(proj, kvc, batch, seq, n_ctx, n_kv, q_off, k_off, v_off, tq, tk):
    gw = Q_GROUP * HEAD_DIM
    per_b = seq // tq
    n_chunks = seq // tk
    kern = functools.partial(_attn_kernel, tq=tq, tk=tk, n_chunks=n_chunks)
    return pl.pallas_call(
        kern,
        grid=(batch, n_kv, per_b),
        in_specs=[pl.BlockSpec((tq, gw), lambda b, g, i: (b * per_b + i, q_off // gw + g)),
                  pl.BlockSpec((seq, HEAD_DIM), lambda b, g, i: (b, k_off // HEAD_DIM + g)),
                  pl.BlockSpec((seq, HEAD_DIM), lambda b, g, i: (b, v_off // HEAD_DIM + g)),
                  pl.BlockSpec((n_ctx, HEAD_DIM), lambda b, g, i: (b, g)),
                  pl.BlockSpec((n_ctx, HEAD_DIM), lambda b, g, i: (b, n_kv + g))],
        out_specs=pl.BlockSpec((tq, gw), lambda b, g, i: (b * per_b + i, g)),
        out_shape=jax.ShapeDtypeStruct((batch * seq, n_kv * gw), BF16),
        scratch_shapes=[pltpu.VMEM((n_chunks, HEAD_DIM, tk), BF16), pltpu.VMEM((HEAD_DIM, n_ctx), BF16)],
        compiler_params=_cparams("arbitrary", "arbitrary", "arbitrary"),
        name="attention",
    )(proj, proj, proj, kvc, kvc)


def _dft_mats(n_rows, n_cols):
    def cs(k, n):
        ang = (2.0 * math.pi / n) * (k % n).astype(F32)
        return jnp.cos(ang), jnp.sin(ang)

    ch = jnp.arange(F_GROUP_DIM)
    cc, sc = cs(ch[:, None] * ch[None, :], F_GROUP_DIM)
    mat_a = jnp.concatenate([cc, -sc], axis=1).astype(BF16)
    r = jnp.arange(n_rows)
    cr, sr = cs(r[:, None] * r[None, :], n_rows)
    mat_1 = jnp.concatenate([jnp.concatenate([cr, sr], axis=1),
                             jnp.concatenate([-sr, cr], axis=1)], axis=0).astype(BF16)
    c = jnp.arange(n_cols)
    c2, s2 = cs(c[:, None] * c[None, :], n_cols)
    mat_2 = jnp.concatenate([c2, s2], axis=1).astype(BF16)
    tr, ts = cs(r[:, None] * c[None, :], n_rows * n_cols)
    tw_r = jnp.repeat(tr, LANE, axis=1)
    tw_i = jnp.repeat(-ts, LANE, axis=1)
    return mat_a, mat_1, mat_2, tw_r, tw_i


def _fourier_a_kernel(u_ref, m_ref, vr_ref, vi_ref, *, groups):
    for g in range(groups):
        sl = slice(g * F_GROUP_DIM, (g + 1) * F_GROUP_DIM)
        r = jnp.dot(u_ref[:, sl], m_ref[...], preferred_element_type=F32)
        vr_ref[:, sl] = r[:, :F_GROUP_DIM].astype(BF16)
        vi_ref[:, sl] = r[:, F_GROUP_DIM:].astype(BF16)


def _fourier_1_kernel(ar_ref, ai_ref, m_ref, twr_ref, twi_ref, pr_ref, pi_ref, *, nseg, fw, n_rows):
    rhs = jnp.concatenate([ar_ref[...], ai_ref[...]], axis=0)
    z = jnp.dot(m_ref[...], rhs, preferred_element_type=F32)
    reps = fw // LANE
    for s in range(nseg):
        zr = z[:n_rows, s * fw:(s + 1) * fw]
        zi = z[n_rows:, s * fw:(s + 1) * fw]
        tr = jnp.tile(twr_ref[:, s * LANE:(s + 1) * LANE], (1, reps))
        ti = jnp.tile(twi_ref[:, s * LANE:(s + 1) * LANE], (1, reps))
        pr_ref[s] = (zr * tr - zi * ti).astype(BF16)
        pi_ref[s] = (zr * ti + zi * tr).astype(BF16)


def _fourier_2_kernel(ar_ref, ai_ref, m_ref, o_ref, *, norm):
    rhs = jnp.concatenate([ar_ref[...], ai_ref[...]], axis=0)
    o_ref[...] = (jnp.dot(m_ref[...], rhs, preferred_element_type=F32) * norm).astype(BF16)


def _fourier_mix(proj, batch, seq, fw):
    n_cols = GRID_W
    n_rows = seq // GRID_W
    groups = fw // F_GROUP_DIM
    mat_a, mat_1, mat_2, tw_r, tw_i = _dft_mats(n_rows, n_cols)
    m = batch * seq
    tm = min(seq, 1024)
    vr, vi = pl.pallas_call(
        functools.partial(_fourier_a_kernel, groups=groups),
        grid=(m // tm,),
        in_specs=[pl.BlockSpec((tm, fw), lambda i: (i, 0)),
                  pl.BlockSpec((F_GROUP_DIM, 2 * F_GROUP_DIM), lambda i: (0, 0))],
        out_specs=[pl.BlockSpec((tm, fw), lambda i: (i, 0))] * 2,
        out_shape=[jax.ShapeDtypeStruct((m, fw), BF16)] * 2,
        compiler_params=_cparams("arbitrary"),
        name="fourier_channels",
    )(proj, mat_a)
    wide = n_cols * fw
    vr = vr.reshape(batch * n_rows, wide)
    vi = vi.reshape(batch * n_rows, wide)
    nseg = min(n_cols, 4)
    pr, pi = pl.pallas_call(
        functools.partial(_fourier_1_kernel, nseg=nseg, fw=fw, n_rows=n_rows),
        grid=(batch, n_cols // nseg),
        in_specs=[pl.BlockSpec((n_rows, nseg * fw), lambda b, j: (b, j)),
                  pl.BlockSpec((n_rows, nseg * fw), lambda b, j: (b, j)),
                  pl.BlockSpec((2 * n_rows, 2 * n_rows), lambda b, j: (0, 0)),
                  pl.BlockSpec((n_rows, nseg * LANE), lambda b, j: (0, j)),
                  pl.BlockSpec((n_rows, nseg * LANE), lambda b, j: (0, j))],
        out_specs=[pl.BlockSpec((nseg, n_rows, fw), lambda b, j: (b * (n_cols // nseg) + j, 0, 0))] * 2,
        out_shape=[jax.ShapeDtypeStruct((batch * n_cols, n_rows, fw), BF16)] * 2,
        compiler_params=_cparams("arbitrary", "arbitrary"),
        name="fourier_rows",
    )(vr, vi, mat_1, tw_r, tw_i)
    wide2 = n_rows * fw
    pr = pr.reshape(batch * n_cols, wide2)
    pi = pi.reshape(batch * n_cols, wide2)
    tc = min(wide2, 8192)
    norm = 1.0 / math.sqrt(seq * F_GROUP_DIM)
    yf = pl.pallas_call(
        functools.partial(_fourier_2_kernel, norm=norm),
        grid=(batch, wide2 // tc),
        in_specs=[pl.BlockSpec((n_cols, tc), lambda b, j: (b, j)),
                  pl.BlockSpec((n_cols, tc), lambda b, j: (b, j)),
                  pl.BlockSpec((n_cols, 2 * n_cols), lambda b, j: (0, 0))],
        out_specs=pl.BlockSpec((n_cols, tc), lambda b, j: (b, j)),
        out_shape=jax.ShapeDtypeStruct((batch * n_cols, wide2), BF16),
        compiler_params=_cparams("arbitrary", "arbitrary"),
        name="fourier_cols",
    )(pr, pi, mat_2)
    return yf.reshape(m, fw)


def _merge_kernel(yf_ref, ya_ref, wfo_ref, wao_ref, gf_ref, ga_ref, o_ref):
    a = jnp.dot(yf_ref[...], wfo_ref[...], preferred_element_type=F32)
    b = jnp.dot(ya_ref[...], wao_ref[...], preferred_element_type=F32)
    o_ref[...] = (gf_ref[...].astype(F32) * a + ga_ref[...].astype(F32) * b).astype(BF16)


def _merge(yf, ya, wfo, wao, proj, gf_off, ga_off, tm, tn):
    m, fw = yf.shape
    aw = ya.shape[1]
    d = wfo.shape[1]
    return pl.pallas_call(
        _merge_kernel,
        grid=(m // tm, d // tn),
        in_specs=[pl.BlockSpec((tm, fw), lambda i, j: (i, 0)),
                  pl.BlockSpec((tm, aw), lambda i, j: (i, 0)),
                  pl.BlockSpec((fw, tn), lambda i, j: (0, j)),
                  pl.BlockSpec((aw, tn), lambda i, j: (0, j)),
                  pl.BlockSpec((tm, tn), lambda i, j: (i, gf_off // tn + j)),
                  pl.BlockSpec((tm, tn), lambda i, j: (i, ga_off // tn + j))],
        out_specs=pl.BlockSpec((tm, tn), lambda i, j: (i, j)),
        out_shape=jax.ShapeDtypeStruct((m, d), BF16),
        compiler_params=_cparams("arbitrary", "arbitrary"),
        name="merge",
    )(yf, ya, wfo, wao, proj, proj)


def _route(logits_t, bias, n_exp):
    gsz = n_exp // N_EXPERT_GROUPS
    tm = logits_t.shape[1]
    neg = -jnp.inf
    scores = jax.nn.sigmoid(logits_t)
    biased = scores + bias
    io_g = lax.broadcasted_iota(I32, (gsz, tm), 0).astype(F32)
    gs = []
    for g in range(N_EXPERT_GROUPS):
        grp = biased[g * gsz:(g + 1) * gsz, :]
        m1 = jnp.max(grp, axis=0, keepdims=True)
        i1 = jnp.min(jnp.where(grp == m1, io_g, float(gsz)), axis=0, keepdims=True)
        m2 = jnp.max(jnp.where(io_g == i1, neg, grp), axis=0, keepdims=True)
        gs.append(m1 + m2)
    cur = jnp.concatenate(gs, axis=0)
    io_n = lax.broadcasted_iota(I32, (N_EXPERT_GROUPS, tm), 0).astype(F32)
    sel = jnp.zeros((N_EXPERT_GROUPS, tm), F32)
    for _ in range(TOPK_GROUPS):
        mx = jnp.max(cur, axis=0, keepdims=True)
        ix = jnp.min(jnp.where(cur == mx, io_n, float(N_EXPERT_GROUPS)), axis=0, keepdims=True)
        hit = io_n == ix
        sel = jnp.where(hit, 1.0, sel)
        cur = jnp.where(hit, neg, cur)
    cur = jnp.concatenate([jnp.where(sel[g:g + 1, :] > 0.0, biased[g * gsz:(g + 1) * gsz, :], neg)
                           for g in range(N_EXPERT_GROUPS)], axis=0)
    io_e = lax.broadcasted_iota(I32, (n_exp, tm), 0).astype(F32)
    eidx, wts = [], []
    for _ in range(TOP_K):
        mx = jnp.max(cur, axis=0, keepdims=True)
        ix = jnp.min(jnp.where(cur == mx, io_e, float(n_exp)), axis=0, keepdims=True)
        hit = io_e == ix
        wts.append(jnp.sum(jnp.where(hit, scores, 0.0), axis=0, keepdims=True))
        eidx.append(ix)
        cur = jnp.where(hit, neg, cur)
    eidx = jnp.concatenate(eidx, axis=0).astype(I32)
    wts = jnp.concatenate(wts, axis=0)
    wts = wts / jnp.sum(wts, axis=0, keepdims=True) * ROUTED_SCALE
    return eidx, wts


def _outproj_kernel(y_ref, wo_ref, x_ref, g1_ref, nw_ref, sh_ref, sc_ref, rw_ref, rb_ref,
                    x1_ref, h2_ref, h2p_ref, eidx_ref, wts_ref, cnt_ref, carry_scr, *, n_exp):
    i = pl.program_id(0)
    tm, d = x_ref.shape
    n_sub = d // LANE

    @pl.when(i == 0)
    def _():
        carry_scr[...] = jnp.zeros_like(carry_scr)

    half = tm // 2 if tm % (2 * LANE) == 0 else tm
    added = jnp.zeros((n_exp, 1), F32)
    for r0 in range(0, tm, half):
        rs = slice(r0, r0 + half)
        x1 = x_ref[rs, :] + g1_ref[...] * jnp.dot(y_ref[rs, :], wo_ref[...], preferred_element_type=F32)
        x1_ref[rs, :] = x1
        ms = jnp.mean(x1 * x1, axis=-1, keepdims=True)
        h2 = (x1 * lax.rsqrt(ms + EPS) * nw_ref[...]) * (1.0 + sc_ref[...]) + sh_ref[...]
        h2b = h2.astype(BF16)
        h2_ref[rs, :] = h2b
        for s in range(n_sub):
            h2p_ref[pl.ds(r0 * n_sub + s, half, stride=n_sub), :] = h2[:, s * LANE:(s + 1) * LANE]
        h_lo = (h2 - h2b.astype(F32)).astype(BF16)
        part = (jnp.dot(h2b, rw_ref[...], preferred_element_type=F32)
                + jnp.dot(h_lo, rw_ref[...], preferred_element_type=F32))
        logits = part[:, :LANE] + part[:, LANE:]
        logits_t = jnp.transpose(logits)[:n_exp, :]
        eidx, wts = _route(logits_t, rb_ref[...], n_exp)
        eidx_ref[:, rs] = eidx
        wts_ref[:, rs] = wts
        io_e = lax.broadcasted_iota(I32, (n_exp, half), 0)
        onehot = jnp.zeros((n_exp, half), F32)
        for k in range(TOP_K):
            onehot = onehot + jnp.where(io_e == eidx[k:k + 1, :], 1.0, 0.0)
        added = added + jnp.sum(onehot, axis=1, keepdims=True)
    carry_scr[...] = carry_scr[...] + added
    cnt_ref[...] = carry_scr[...].astype(I32)


def _outproj_route(y, wo, x2d, g1, norm_w, shift, scale, rw_pad, rb, seq, n_exp, tm):
    m, d = x2d.shape
    per_b = seq // tm
    n_sub = d // LANE
    bspec = pl.BlockSpec((None, 1, d), lambda i: (i // per_b, 0, 0))
    row = pl.BlockSpec((tm, d), lambda i: (i, 0))
    tok = pl.BlockSpec((TOP_K, tm), lambda i: (0, i))
    return pl.pallas_call(
        functools.partial(_outproj_kernel, n_exp=n_exp),
        grid=(m // tm,),
        in_specs=[row,
                  pl.BlockSpec((d, d), lambda i: (0, 0), pipeline_mode=pl.Buffered(1)),
                  row, bspec,
                  pl.BlockSpec((1, d), lambda i: (0, 0)),
                  bspec, bspec,
                  pl.BlockSpec((d, 2 * LANE), lambda i: (0, 0)),
                  pl.BlockSpec((n_exp, 1), lambda i: (0, 0))],
        out_specs=[row, row,
                   pl.BlockSpec((tm * n_sub, LANE), lambda i: (i, 0)),
                   tok, tok,
                   pl.BlockSpec((n_exp, LANE), lambda i: (0, 0))],
        out_shape=[jax.ShapeDtypeStruct((m, d), F32),
                   jax.ShapeDtypeStruct((m, d), BF16),
                   jax.ShapeDtypeStruct((m * n_sub, LANE), F32),
                   jax.ShapeDtypeStruct((TOP_K, m), I32),
                   jax.ShapeDtypeStruct((TOP_K, m), F32),
                   jax.ShapeDtypeStruct((n_exp, LANE), I32)],
        scratch_shapes=[pltpu.VMEM((n_exp, LANE), F32)],
        compiler_params=_cparams("arbitrary"),
        name="outproj_route",
    )(y, wo, x2d, g1, norm_w.reshape(1, d), shift, scale, rw_pad, rb.reshape(n_exp, 1))


def _expert_kernel(be_ref, bwin_ref, brem_ref, bval_ref, nact_ref, prev_win, cur_win, nxt_win,
                   h2p_hbm, wg_ref, wu_ref, wd_ref, ytk_hbm,
                   wg_s, wu_s, wd_s, xbuf0, xbuf1, ybuf0, ybuf1, sem_g, sem_s, sem_z,
                   *, n_sub, n_tok, n_blocks, trash_base):
    i = pl.program_id(0)
    nact = nact_ref[0]
    bm = EXPERT_BLOCK
    xbufs = (xbuf0, xbuf1)
    ybufs = (ybuf0, ybuf1)
    unroll = 8

    def token_of(f):
        return f & (n_tok - 1) if n_tok & (n_tok - 1) == 0 else lax.rem(f, n_tok)

    def gather_copy(tok, r, par):
        return pltpu.make_async_copy(h2p_hbm.at[pl.ds(pl.multiple_of(tok * n_sub, n_sub), n_sub), :],
                                     xbufs[par].at[pl.ds(pl.multiple_of(r * n_sub, n_sub), n_sub), :],
                                     sem_g.at[par])

    def scatter_copy(r, row, par):
        return pltpu.make_async_copy(ybufs[par].at[pl.ds(pl.multiple_of(r * n_sub, n_sub), n_sub), :],
                                     ytk_hbm.at[pl.ds(pl.multiple_of(row * n_sub, n_sub), n_sub), :],
                                     sem_s.at[par])

    def wait_gathers(par):
        @pl.loop(0, bm // unroll)
        def _(q):
            for u in range(unroll):
                gather_copy(0, 0, par).wait()

    def wait_scatters(par):
        @pl.loop(0, bm // unroll)
        def _(q):
            for u in range(unroll):
                scatter_copy(0, 0, par).wait()

    def scatter_row(win_ref, rem, valid, r, par):
        row = jnp.where(r < valid, win_ref[0, rem + r], trash_base + par * bm + r)
        return scatter_copy(r, row, par)

    @pl.when(i == 0)
    def _():
        ybuf0[...] = jnp.zeros_like(ybuf0)
        ybuf1[...] = jnp.zeros_like(ybuf1)
        for par in range(2):
            start = (trash_base + par * bm) * n_sub
            cp = pltpu.make_async_copy(ybuf0, ytk_hbm.at[pl.ds(start, bm * n_sub), :], sem_z)
            cp.start()
            cp.wait()
        rem0 = brem_ref[0]

        @pl.loop(0, bm // unroll)
        def _(q):
            for u in range(unroll):
                r = q * unroll + u
                gather_copy(token_of(cur_win[0, rem0 + r]), r, 0).start(priority=u % 2)

    def block_step(par):
        wait_gathers(par)

        @pl.when(i >= 1)
        def _():
            wait_scatters(par)

        @pl.when((i == 0) | (be_ref[i] != be_ref[jnp.maximum(i - 1, 0)]))
        def _():
            wg_s[...] = wg_ref[...].astype(BF16)
            wu_s[...] = wu_ref[...].astype(BF16)
            wd_s[...] = wd_ref[...].astype(BF16)

        nxt_rem = brem_ref[jnp.minimum(i + 1, n_blocks - 1)]
        prv = jnp.maximum(i - 1, 0)
        prv_rem = brem_ref[prv]
        prv_valid = jnp.where(i >= 1, bval_ref[prv], 0)
        for r in range(bm):
            gather_copy(token_of(nxt_win[0, nxt_rem + r]), r, 1 - par).start(priority=r % 2)
        for r in range(bm):
            scatter_row(prev_win, prv_rem, prv_valid, r, 1 - par).start(priority=r % 2)

        x = jnp.concatenate([xbufs[par][pl.ds(s, bm, stride=n_sub), :] for s in range(n_sub)],
                            axis=1).astype(BF16)
        g = jnp.dot(x, wg_s[...], preferred_element_type=F32)
        u = jnp.dot(x, wu_s[...], preferred_element_type=F32)
        y = jnp.dot((_silu(g) * u).astype(BF16), wd_s[...], preferred_element_type=F32)
        for s in range(n_sub):
            ybufs[par][pl.ds(s, bm, stride=n_sub), :] = y[:, s * LANE:(s + 1) * LANE]

        @pl.when(i == nact - 1)
        def _():
            wait_gathers(1 - par)
            wait_scatters(1 - par)
            rem = brem_ref[i]
            valid = bval_ref[i]

            @pl.loop(0, bm // unroll)
            def _(q):
                for u_ in range(unroll):
                    scatter_row(cur_win, rem, valid, q * unroll + u_, par).start(priority=u_ % 2)

            wait_scatters(par)

    for par in range(2):
        @pl.when((i < nact) & (lax.rem(i, 2) == par))
        def _(par=par):
            block_step(par)


def _experts(h2p, order, blk_expert, blk_off, blk_valid, n_active, exp_gate, exp_up, exp_down, n_blocks, n_tok):
    n_exp, d, ff = exp_gate.shape
    n_sub = d // LANE
    bm = EXPERT_BLOCK
    n_win = order.shape[0] // bm
    pieces = order.reshape(n_win, 1, bm)
    windows = jnp.concatenate([pieces, jnp.roll(pieces, -1, axis=0)], axis=-1)
    blk_win = blk_off // bm
    blk_rem = blk_off - blk_win * bm
    trash_base = TOP_K * n_tok

    def win(step_shift):
        def index_map(i, be, bwin, *_):
            return (bwin[jnp.clip(i + step_shift, 0, n_blocks - 1)], 0, 0)
        return pl.BlockSpec((None, 1, 2 * bm), index_map, memory_space=pltpu.SMEM)

    row_buf = pltpu.VMEM((bm * n_sub, LANE), F32)
    grid_spec = pltpu.PrefetchScalarGridSpec(
        num_scalar_prefetch=5,
        grid=(n_blocks,),
        in_specs=[win(-1), win(0), win(1),
                  pl.BlockSpec(memory_space=pl.ANY),
                  pl.BlockSpec((None, d, ff), lambda i, be, *_: (be[i], 0, 0)),
                  pl.BlockSpec((None, d, ff), lambda i, be, *_: (be[i], 0, 0)),
                  pl.BlockSpec((None, ff, d), lambda i, be, *_: (be[i], 0, 0))],
        out_specs=pl.BlockSpec(memory_space=pl.ANY),
        scratch_shapes=[pltpu.VMEM((d, ff), BF16), pltpu.VMEM((d, ff), BF16), pltpu.VMEM((ff, d), BF16),
                        row_buf, row_buf, row_buf, row_buf,
                        pltpu.SemaphoreType.DMA((2,)), pltpu.SemaphoreType.DMA((2,)),
                        pltpu.SemaphoreType.DMA(())],
    )
    return pl.pallas_call(
        functools.partial(_expert_kernel, n_sub=n_sub, n_tok=n_tok, n_blocks=n_blocks, trash_base=trash_base),
        grid_spec=grid_spec,
        out_shape=jax.ShapeDtypeStruct(((trash_base + 2 * bm) * n_sub, LANE), F32),
        compiler_params=_cparams("arbitrary"),
        name="experts",
    )(blk_expert, blk_win, blk_rem, blk_valid, n_active, windows, windows, windows, h2p, exp_gate, exp_up,
      exp_down)


def _combine_kernel(*refs, tmc, n_out):
    yk_refs = refs[:TOP_K]
    wt_ref, h2_ref, x1_ref, g2_ref, sg_ref, su_ref, sd_ref, o_ref, acc_scr = refs[TOP_K:]
    h = h2_ref[...]
    g = jnp.dot(h, sg_ref[...], preferred_element_type=F32)
    u = jnp.dot(h, su_ref[...], preferred_element_type=F32)
    shared = jnp.dot((_silu(g) * u).astype(BF16), sd_ref[...], preferred_element_type=F32)
    unroll = 4

    @pl.loop(0, tmc // unroll)
    def _(q):
        for j in range(unroll):
            t = q * unroll + j
            rows = pl.ds(pl.multiple_of(t * n_out, n_out), n_out)
            acc = wt_ref[0, t] * yk_refs[0][rows, :]
            for k in range(1, TOP_K):
                acc = acc + wt_ref[k, t] * yk_refs[k][rows, :]
            acc_scr[rows, :] = acc

    routed = jnp.concatenate([acc_scr[pl.ds(s, tmc, stride=n_out), :] for s in range(n_out)], axis=1)
    o_ref[...] = x1_ref[...] + g2_ref[...] * (routed + shared)


def _combine(ytk, wts, h2, x1, g2, sg, su, sd, seq, tmc):
    m, d = x1.shape
    ff = sg.shape[1]
    n_out = d // LANE
    per_b = seq // tmc
    tiles = m // tmc
    row = pl.BlockSpec((tmc, d), lambda i: (i, 0))
    yk_specs = [pl.BlockSpec((tmc * n_out, LANE), lambda i, k=k: (k * tiles + i, 0)) for k in range(TOP_K)]
    return pl.pallas_call(
        functools.partial(_combine_kernel, tmc=tmc, n_out=n_out),
        grid=(tiles,),
        in_specs=yk_specs + [pl.BlockSpec((TOP_K, tmc), lambda i: (0, i), memory_space=pltpu.SMEM),
                             row, row,
                             pl.BlockSpec((None, 1, d), lambda i: (i // per_b, 0, 0)),
                             pl.BlockSpec((d, ff), lambda i: (0, 0)),
                             pl.BlockSpec((d, ff), lambda i: (0, 0)),
                             pl.BlockSpec((ff, d), lambda i: (0, 0))],
        out_specs=row,
        out_shape=jax.ShapeDtypeStruct((m, d), F32),
        scratch_shapes=[pltpu.VMEM((tmc * n_out, LANE), F32)],
        compiler_params=_cparams("arbitrary"),
        name="combine",
    )(*([ytk] * TOP_K), wts, h2, x1, g2, sg, su, sd)


def _tiles(seq, d, kvw, n_ctx):
    return dict(
        inproj_tm=min(1024, seq), inproj_tn=min(512, kvw), ctx_tm=min(256, n_ctx),
        attn_tq=min(512, seq), attn_tk=min(512, seq),
        merge_tm=min(1024, seq), merge_tn=min(512, d),
        outproj_tm=min(512, seq), combine_tm=min(128, seq))


def kernel(x, c, ctx, c_ctx, mod_w, mod_b, norm1_w, w_in, q_norm_w, k_norm_w, w_fourier_out, w_attn_out, w_out,
           norm2_w, router_w, router_b, exp_gate, exp_up, exp_down, shared_gate, shared_up, shared_down):
    batch, seq, d = x.shape
    n_ctx = ctx.shape[1]
    assert mod_w.shape[0] == 1, "single-layer block only"
    assert batch + 1 <= 8 and seq % GRID_W == 0
    fw = w_fourier_out.shape[1]
    att = w_attn_out.shape[1]
    in_cols = w_in.shape[2]
    kvw = (in_cols - fw - att - 2 * d) // 2
    n_kv = kvw // HEAD_DIM
    n_exp = router_w.shape[2]
    assert att == n_kv * Q_GROUP * HEAD_DIM and n_exp <= LANE
    q_off, k_off = fw, fw + att
    v_off = k_off + kvw
    gf_off = v_off + kvw
    ga_off = gf_off + d
    m = batch * seq
    assert q_off % (Q_GROUP * HEAD_DIM) == 0 and kvw % HEAD_DIM == 0 and fw % F_GROUP_DIM == 0
    t = _tiles(seq, d, kvw, n_ctx)
    assert gf_off % t["merge_tn"] == 0 and d % (2 * LANE) == 0

    cond = jnp.concatenate([c, c_ctx[None], jnp.zeros((8 - batch - 1, d), F32)], axis=0)
    mod = _adaln(cond, mod_w[0], mod_b[0])
    sh1, sc1, g1, sh2, sc2, g2 = [mod[:batch, j * d:(j + 1) * d].reshape(batch, 1, d) for j in range(6)]
    csh1 = jnp.broadcast_to(mod[batch, 0:d], (batch, 1, d))
    csc1 = jnp.broadcast_to(mod[batch, d:2 * d], (batch, 1, d))

    cos_t, sin_t = _rope_tables(seq)
    tn = t["inproj_tn"]
    modes = (("plain", 0, q_off // tn), ("q", q_off // tn, k_off // tn), ("k", k_off // tn, v_off // tn),
             ("plain", v_off // tn, gf_off // tn), ("sigmoid", gf_off // tn, in_cols // tn))
    proj = _inproj(x.reshape(m, d), sh1, sc1, norm1_w[0], w_in[0], q_norm_w[0], k_norm_w[0], cos_t, sin_t,
                   modes, seq, t["inproj_tm"], tn)
    cmodes = (("k_norope", 0, kvw // tn), ("plain", kvw // tn, 2 * kvw // tn))
    tmc_ctx = t["ctx_tm"]
    kvc = _inproj(ctx.reshape(batch * n_ctx, d), csh1, csc1, norm1_w[0], w_in[0], q_norm_w[0],
                  k_norm_w[0], cos_t[:tmc_ctx], sin_t[:tmc_ctx], cmodes, n_ctx, tmc_ctx, tn, col0=k_off,
                  n_cols=2 * kvw)

    ya = _attention(proj, kvc, batch, seq, n_ctx, n_kv, q_off, k_off, v_off, tq=t["attn_tq"], tk=t["attn_tk"])
    yf = _fourier_mix(proj, batch, seq, fw)
    y = _merge(yf, ya, w_fourier_out[0].astype(BF16), w_attn_out[0].astype(BF16), proj, gf_off, ga_off,
               tm=t["merge_tm"], tn=t["merge_tn"])

    rw_pad = jnp.pad(router_w[0], ((0, 0), (0, LANE - n_exp)))
    rw_hi = rw_pad.astype(BF16)
    rw_pad = jnp.concatenate([rw_hi, (rw_pad - rw_hi.astype(F32)).astype(BF16)], axis=1)
    x1, h2, h2p, eidx, wts, cnt = _outproj_route(
        y, w_out[0].astype(BF16), x.reshape(m, d), g1, norm2_w[0], sh2, sc2, rw_pad, router_b[0], seq, n_exp,
        tm=t["outproj_tm"])

    n_assign = m * TOP_K
    assert n_assign % EXPERT_BLOCK == 0
    _, order = lax.sort((eidx.reshape(-1), jnp.arange(n_assign, dtype=I32)), num_keys=1)
    counts = cnt[:, 0]
    starts = jnp.cumsum(counts) - counts
    nb_e = (counts + EXPERT_BLOCK - 1) // EXPERT_BLOCK
    cum_nb = jnp.cumsum(nb_e)
    n_blocks = n_assign // EXPERT_BLOCK + n_exp
    n_active = jnp.maximum(cum_nb[-1], 1).astype(I32)
    blk = jnp.minimum(jnp.arange(n_blocks, dtype=I32), n_active - 1)
    blk_expert = jnp.minimum(jnp.sum(blk[:, None] >= cum_nb[None, :], axis=1), n_exp - 1).astype(I32)
    is_e = blk_expert[:, None] == jnp.arange(n_exp, dtype=I32)[None, :]
    lookup = lambda table: jnp.sum(jnp.where(is_e, table[None, :], 0), axis=1)
    blk_j = blk - lookup(cum_nb - nb_e)
    blk_off = (lookup(starts) + blk_j * EXPERT_BLOCK).astype(I32)
    blk_valid = jnp.clip(lookup(counts) - blk_j * EXPERT_BLOCK, 0, EXPERT_BLOCK).astype(I32)

    ytk = _experts(h2p, order, blk_expert, blk_off, blk_valid, n_active.reshape(1), exp_gate[0], exp_up[0],
                   exp_down[0], n_blocks, m)
    out = _combine(ytk, wts, h2, x1, g2, shared_gate[0].astype(BF16), shared_up[0].astype(BF16),
                   shared_down[0].astype(BF16), seq, tmc=t["combine_tm"])
    return out.reshape(batch, seq, d)
```
